```python
import math
import jax
import jax.numpy as jnp
from jax import lax
import numpy as np

D_MODEL = 1024
BATCH = 4
SEQ = 8192
DEPTH = 4

GRID_W = 64
CTX_LEN = 256
N_MIXERS = 3
DN_ALPHA = (2 * DEPTH) ** 0.25
DN_BETA = (8 * DEPTH) ** -0.25
LN_EPS = 1e-5

RK_HEAD = 64
RK_HEADS = D_MODEL // RK_HEAD
RK_DECAY_LORA = 64
RK_ICLR_LORA = 64
RK_GATE_LORA = 128
RK_DECAY_SCALE = math.exp(-0.5)
RK_GN_EPS = 64e-5

POOL_WINDOWS = (2, 4, 8, 16)
POOL_GROUP = D_MODEL // len(POOL_WINDOWS)

GDN_HEAD = 128
GDN_HEADS = D_MODEL // GDN_HEAD
GDN_CONV = 4
GDN_CHUNK = 64
GDN_NORM_EPS = 1e-6
GDN_IN = 4 * D_MODEL + 4 * GDN_HEADS

D_FF = 2816
N_EXPERTS = 8
TOP_K = 2
D_FF_EXPERT = 1408

N_RWKV_LAYERS = len(range(0, DEPTH, N_MIXERS))
N_POOL_LAYERS = len(range(1, DEPTH, N_MIXERS))
N_GDN_LAYERS = len(range(2, DEPTH, N_MIXERS))
N_DENSE_LAYERS = len(range(0, DEPTH, 2))
N_MOE_LAYERS = len(range(1, DEPTH, 2))

kernel_name = 'hybrid_rwkv7_pool_gdn_moe_diffusion_trunk'


def layer_norm(x, g, b):
    xf = x.astype(jnp.float32)
    mean = jnp.mean(xf, axis=-1, keepdims=True)
    var = jnp.mean(jnp.square(xf - mean), axis=-1, keepdims=True)
    return ((xf - mean) * lax.rsqrt(var + LN_EPS) * g + b).astype(x.dtype)


def flip_time(t):
    return jnp.flip(t, axis=1)


def identity(t):
    return t


def qshift_grid(h, n_rows):
    b, n, d = h.shape
    q = d // 4
    grid = h.reshape(b, n_rows, GRID_W, d)
    left = jnp.pad(grid[:, :, :-1, :q], ((0, 0), (0, 0), (1, 0), (0, 0)))
    right = jnp.pad(grid[:, :, 1:, q:2 * q], ((0, 0), (0, 0), (0, 1), (0, 0)))
    up = jnp.pad(grid[:, :-1, :, 2 * q:3 * q], ((0, 0), (1, 0), (0, 0), (0, 0)))
    down = jnp.pad(grid[:, 1:, :, 3 * q:], ((0, 0), (0, 1), (0, 0), (0, 0)))
    return jnp.concatenate([left, right, up, down], axis=-1).reshape(b, n, d)


def bishift_seq(h):
    half = h.shape[-1] // 2
    prev = jnp.pad(h[:, :-1, :half], ((0, 0), (1, 0), (0, 0)))
    nxt = jnp.pad(h[:, 1:, half:], ((0, 0), (0, 1), (0, 0)))
    return jnp.concatenate([prev, nxt], axis=-1)


def to_heads(t, n_heads, head_dim):
    return t.reshape(*t.shape[:-1], n_heads, head_dim)


def rwkv7_branch(h, shifted, mu, w_rkv, w0, w1, w2, a0, a1, a2, k_k, k_a):
    f32 = jnp.float32
    xx = shifted - h
    xr, xw, xk, xv, xa, xg = (h + xx * mu[m] for m in range(6))
    r = (xr @ w_rkv[0]).astype(f32)
    k = (xk @ w_rkv[1]).astype(f32)
    v = (xv @ w_rkv[2]).astype(f32)
    d_lora = jnp.einsum('zbtr,zrd->zbtd', jnp.tanh(jnp.einsum('btd,zdr->zbtr', xw, w1)), w2)
    log_w = -RK_DECAY_SCALE * jax.nn.sigmoid((w0[:, None, None, :] + d_lora).astype(f32))
    a_lora = jnp.einsum('zbtr,zrd->zbtd', jnp.einsum('btd,zdr->zbtr', xa, a1), a2)
    a = jax.nn.sigmoid((a0[:, None, None, :] + a_lora).astype(f32))
    kk = to_heads(k * k_k.astype(f32), RK_HEADS, RK_HEAD)
    kk = kk / jnp.maximum(jnp.sqrt(jnp.sum(kk * kk, axis=-1, keepdims=True)), 1e-12)
    k_dir = k[None] * (1.0 + (a - 1.0) * k_a.astype(f32))
    hd = lambda t: to_heads(t, RK_HEADS, RK_HEAD)
    return (hd(r), hd(log_w), kk, hd(a), hd(k_dir), hd(v), xg)


def wkv7_scan(s0, r, w, kk, a, k, v):
    def step(s, inp):
        r_t, w_t, kk_t, a_t, k_t, v_t = inp
        s_kk = jnp.einsum('bhvk,bhk->bhv', s, kk_t)
        s = (s * w_t[:, :, None, :] - s_kk[..., None] * (kk_t * a_t)[:, :, None, :]
             + v_t[..., None] * k_t[:, :, None, :])
        return s, jnp.einsum('bhvk,bhk->bhv', s, r_t)
    xs = tuple(jnp.moveaxis(t, 1, 0) for t in (r, w, kk, a, k, v))
    s_final, ys = lax.scan(step, s0, xs)
    return jnp.moveaxis(ys, 0, 1), s_final


def rwkv7_dir_inputs(feats, d, flip):
    r, log_w, kk, a, k_dir, v, _ = feats
    return [flip(t) for t in (r, jnp.exp(log_w[d]), kk, a[d], k_dir[d], v)]


def rwkv7_out(y, feats, r_k, g1, g2, lnx_g, lnx_b, w_o):
    r, _, _, _, k_dir, v, xg = feats
    b, n = y.shape[:2]
    mean = jnp.mean(y, axis=-1, keepdims=True)
    var = jnp.mean(jnp.square(y - mean), axis=-1, keepdims=True)
    yn = ((y - mean) * lax.rsqrt(var + RK_GN_EPS)).reshape(b, n, D_MODEL) * lnx_g + lnx_b
    coef = jnp.einsum('bthn,zbthn,hn->bth', r, k_dir, r_k.astype(jnp.float32))
    bonus = (coef[..., None] * v).reshape(b, n, D_MODEL)
    gate = jax.nn.sigmoid(xg @ g1) @ g2
    return ((yn + bonus).astype(xg.dtype) * gate) @ w_o


def rwkv7_mixer(h, hc, n_rows, mu, w_rkv, w0, w1, w2, a0, a1, a2, g1, g2, k_k, k_a, r_k,
                lnx_g, lnx_b, w_o, want_ctx):
    lat = rwkv7_branch(h, qshift_grid(h, n_rows), mu, w_rkv, w0, w1, w2, a0, a1, a2, k_k, k_a)
    cx = rwkv7_branch(hc, bishift_seq(hc), mu, w_rkv, w0, w1, w2, a0, a1, a2, k_k, k_a)
    zero_state = jnp.zeros((h.shape[0], RK_HEADS, RK_HEAD, RK_HEAD), jnp.float32)
    y_lat = 0.0
    y_ctx = 0.0
    for d in range(2):
        flip = flip_time if d == 1 else identity
        yc, s_ctx = wkv7_scan(zero_state, *rwkv7_dir_inputs(cx, d, flip))
        yl, _ = wkv7_scan(s_ctx, *rwkv7_dir_inputs(lat, d, flip))
        y_lat = y_lat + flip(yl)
        y_ctx = y_ctx + flip(yc)
    out_lat = rwkv7_out(y_lat, lat, r_k, g1, g2, lnx_g, lnx_b, w_o)
    out_ctx = rwkv7_out(y_ctx, cx, r_k, g1, g2, lnx_g, lnx_b, w_o) if want_ctx else None
    return out_lat, out_ctx


def centred_pool_minus_self(x, window):
    n = x.shape[-2]
    xf = x.astype(jnp.float32)
    csum = jnp.cumsum(xf, axis=-2)
    csum = jnp.concatenate([jnp.zeros_like(csum[..., :1, :]), csum], axis=-2)
    t = jnp.arange(n)
    lo = jnp.clip(t - window // 2, 0, n)
    hi = jnp.clip(t + window // 2, 0, n)
    total = jnp.take(csum, hi, axis=-2) - jnp.take(csum, lo, axis=-2)
    count = (hi - lo).astype(jnp.float32)[:, None]
    return (total / count - xf).astype(x.dtype)


def pool_mixer(h, pool_w, pool_scale):
    outs = []
    for gi, win in enumerate(POOL_WINDOWS):
        xg = h[..., gi * POOL_GROUP:(gi + 1) * POOL_GROUP]
        outs.append(centred_pool_minus_self(xg, win) @ pool_w[gi])
    return jnp.concatenate(outs, axis=-1) * pool_scale


def depthwise_conv_centred(x, w):
    left = GDN_CONV // 2
    right = GDN_CONV - 1 - left
    return lax.conv_general_dilated(x, w[:, None, :], window_strides=(1,), padding=[(left, right)],
                                    dimension_numbers=('NWC', 'WIO', 'NWC'),
                                    feature_group_count=x.shape[-1])


def l2_normalize(t, eps):
    return t * lax.rsqrt(jnp.sum(t * t, axis=-1, keepdims=True) + eps)


def gdn_features(h, w_in, conv_w, a_log, dt_bias):
    f32 = jnp.float32
    b, n, d = h.shape
    proj = h @ w_in
    qkv = jax.nn.silu(depthwise_conv_centred(proj[..., :3 * d], conv_w)).astype(f32)
    q, k, v = (to_heads(t, GDN_HEADS, GDN_HEAD) for t in jnp.split(qkv, 3, axis=-1))
    q = l2_normalize(q, 1e-6) * GDN_HEAD ** -0.5
    k = l2_normalize(k, 1e-6)
    z = proj[..., 3 * d:4 * d]
    a_pre = proj[..., 4 * d:4 * d + 2 * GDN_HEADS].astype(f32).reshape(b, n, 2, GDN_HEADS)
    b_pre = proj[..., 4 * d + 2 * GDN_HEADS:].astype(f32).reshape(b, n, 2, GDN_HEADS)
    g = -jnp.exp(a_log.astype(f32)) * jax.nn.softplus(a_pre + dt_bias.astype(f32))
    beta = jax.nn.sigmoid(b_pre)
    return q, k, v, z, g, beta


def gated_delta_chunked(q, k, v, g, beta, s0):
    b, n, nh, dk = k.shape
    c = GDN_CHUNK
    nc = n // c

    def chunks(t):
        t = jnp.moveaxis(t, 2, 1)
        return t.reshape(b, nh, nc, c, *t.shape[3:])

    q, k, v, g, beta = chunks(q), chunks(k), chunks(v), chunks(g), chunks(beta)
    gc = jnp.cumsum(g, axis=-1)
    causal = jnp.tril(jnp.ones((c, c), bool))
    strict = jnp.tril(jnp.ones((c, c), bool), -1)
    decay = jnp.exp(jnp.where(causal, gc[..., :, None] - gc[..., None, :], -jnp.inf))
    kb = k * beta[..., None]
    vb = v * beta[..., None]
    lower = jnp.where(strict, jnp.einsum('bhnid,bhnjd->bhnij', kb, k) * decay, 0.0)
    eye = jnp.eye(c, dtype=lower.dtype)
    tmat = lax.linalg.triangular_solve(eye + lower, jnp.broadcast_to(eye, lower.shape),
                                       left_side=True, lower=True)
    u = tmat @ vb
    w = tmat @ (kb * jnp.exp(gc)[..., None])
    a_qk = jnp.where(causal, jnp.einsum('bhnid,bhnjd->bhnij', q, k) * decay, 0.0)
    q_dec = q * jnp.exp(gc)[..., None]
    k_dec = k * jnp.exp(gc[..., -1:] - gc)[..., None]
    g_last = jnp.exp(gc[..., -1])

    def step(s, inp):
        u_i, w_i, q_i, k_i, a_i, gl_i = inp
        v_new = u_i - w_i @ s
        o_i = q_i @ s + a_i @ v_new
        s = s * gl_i[..., None, None] + jnp.swapaxes(k_i, -1, -2) @ v_new
        return s, o_i

    xs = tuple(jnp.moveaxis(t, 2, 0) for t in (u, w, q_dec, k_dec, a_qk, g_last))
    s_final, o = lax.scan(step, s0, xs)
    o = jnp.moveaxis(o, 0, 2).reshape(b, nh, n, -1)
    return jnp.moveaxis(o, 1, 2), s_final


def gdn_out(o, z, norm_w, w_o):
    b, n = z.shape[:2]
    on = o * lax.rsqrt(jnp.mean(o * o, axis=-1, keepdims=True) + GDN_NORM_EPS) * norm_w.astype(jnp.float32)
    zf = to_heads(z.astype(jnp.float32), GDN_HEADS, GDN_HEAD)
    y = (on * jax.nn.silu(zf)).reshape(b, n, D_MODEL).astype(z.dtype)
    return y @ w_o


def gdn_mixer(h, hc, w_in, conv_w, a_log, dt_bias, norm_w, w_o, want_ctx):
    lq, lk, lv, lz, lg, lb = gdn_features(h, w_in, conv_w, a_log, dt_bias)
    cq, ck, cv, cz, cg, cb = gdn_features(hc, w_in, conv_w, a_log, dt_bias)
    zero_state = jnp.zeros((h.shape[0], GDN_HEADS, GDN_HEAD, GDN_HEAD), jnp.float32)
    o_lat = 0.0
    o_ctx = 0.0
    for d in range(2):
        flip = flip_time if d == 1 else identity
        oc, s_ctx = gated_delta_chunked(flip(cq), flip(ck), flip(cv), flip(cg[:, :, d]), flip(cb[:, :, d]),
                                        zero_state)
        ol, _ = gated_delta_chunked(flip(lq), flip(lk), flip(lv), flip(lg[:, :, d]), flip(lb[:, :, d]), s_ctx)
        o_lat = o_lat + flip(ol)
        o_ctx = o_ctx + flip(oc)
    y_lat = gdn_out(o_lat, lz, norm_w, w_o)
    y_ctx = gdn_out(o_ctx, cz, norm_w, w_o) if want_ctx else None
    return y_lat, y_ctx


def swiglu(h, w1, w3, w2):
    return (jax.nn.silu(h @ w1) * (h @ w3)) @ w2


def moe_swiglu(h, router_w, router_b, w1, w3, w2):
    logits = (h @ router_w).astype(jnp.float32) + router_b.astype(jnp.float32)
    top_logit, top_idx = lax.top_k(logits, TOP_K)
    top_p = jax.nn.softmax(top_logit, axis=-1)
    gates = jnp.einsum('btk,btke->bte', top_p,
                       jax.nn.one_hot(top_idx, N_EXPERTS, dtype=jnp.float32)).astype(h.dtype)
    y = jnp.zeros_like(h)
    for e in range(N_EXPERTS):
        y = y + gates[..., e:e + 1] * swiglu(h, w1[e], w3[e], w2[e])
    return y


def setup_inputs(seed: int = 0) -> dict:
    key = jax.random.key(seed)
    ks = iter(jax.random.split(key, 64))
    f32 = jnp.float32

    def nrm(shape, std):
        return jax.random.normal(next(ks), shape, f32) * std

    def near_one(shape):
        return 1.0 + nrm(shape, 0.1)

    D, H, N = D_MODEL, RK_HEADS, RK_HEAD
    NA, NB, NC, ND, NM = N_RWKV_LAYERS, N_POOL_LAYERS, N_GDN_LAYERS, N_DENSE_LAYERS, N_MOE_LAYERS
    dt = jnp.exp(jax.random.uniform(next(ks), (NC, 2, GDN_HEADS), f32, math.log(1e-3), math.log(1e-1)))
    gdn_dt_bias = dt + jnp.log(-jnp.expm1(-dt))
    return {
        'x': nrm((BATCH, SEQ, D), 1.0),
        'c': nrm((BATCH, D), 1.0),
        'ctx': nrm((BATCH, CTX_LEN, D), 1.0),
        'c_ctx': nrm((D,), 1.0),
        'ada_w': nrm((DEPTH, D, 6 * D), D ** -0.5),
        'ada_b': nrm((DEPTH, 6 * D), 0.02),
        'ln_g': near_one((DEPTH, 2, D)),
        'ln_b': nrm((DEPTH, 2, D), 0.02),
        'rk_mu': jax.random.uniform(next(ks), (NA, 6, D), f32),
        'rk_w_rkv': nrm((NA, 3, D, D), D ** -0.5),
        'rk_w0': nrm((NA, 2, D), 1.0),
        'rk_w1': nrm((NA, 2, D, RK_DECAY_LORA), D ** -0.5),
        'rk_w2': nrm((NA, 2, RK_DECAY_LORA, D), 0.5 * RK_DECAY_LORA ** -0.5),
        'rk_a0': nrm((NA, 2, D), 0.5),
        'rk_a1': nrm((NA, 2, D, RK_ICLR_LORA), D ** -0.5),
        'rk_a2': nrm((NA, 2, RK_ICLR_LORA, D), 0.5 * RK_ICLR_LORA ** -0.5),
        'rk_g1': nrm((NA, D, RK_GATE_LORA), D ** -0.5),
        'rk_g2': nrm((NA, RK_GATE_LORA, D), RK_GATE_LORA ** -0.5),
        'rk_k_k': near_one((NA, D)),
        'rk_k_a': near_one((NA, D)),
        'rk_r_k': nrm((NA, H, N), 0.1),
        'rk_lnx_g': near_one((NA, D)),
        'rk_lnx_b': nrm((NA, D), 0.02),
        'rk_w_o': nrm((NA, D, D), DN_BETA * D ** -0.5),
        'pool_w': nrm((NB, len(POOL_WINDOWS), POOL_GROUP, POOL_GROUP), DN_BETA * POOL_GROUP ** -0.5),
        'pool_scale': near_one((NB, D)),
        'gdn_w_in': nrm((NC, D, GDN_IN), D ** -0.5),
        'gdn_conv_w': nrm((NC, GDN_CONV, 3 * D), GDN_CONV ** -0.5),
        'gdn_a_log': jnp.log(jax.random.uniform(next(ks), (NC, 2, GDN_HEADS), f32, 1.0, 16.0)),
        'gdn_dt_bias': gdn_dt_bias,
        'gdn_norm_w': near_one((NC, GDN_HEAD)),
        'gdn_w_o': nrm((NC, D, D), DN_BETA * D ** -0.5),
        'ffn_w1': nrm((ND, D, D_FF), D ** -0.5),
        'ffn_w3': nrm((ND, D, D_FF), D ** -0.5),
        'ffn_w2': nrm((ND, D_FF, D), DN_BETA * D_FF ** -0.5),
        'moe_router_w': nrm((NM, D, N_EXPERTS), D ** -0.5),
        'moe_router_b': nrm((NM, N_EXPERTS), 0.01),
        'moe_w1': nrm((NM, N_EXPERTS, D, D_FF_EXPERT), D ** -0.5),
        'moe_w3': nrm((NM, N_EXPERTS, D, D_FF_EXPERT), D ** -0.5),
        'moe_w2': nrm((NM, N_EXPERTS, D_FF_EXPERT, D), DN_BETA * D_FF_EXPERT ** -0.5),
    }


def reference(x, c, ctx, c_ctx, ada_w, ada_b, ln_g, ln_b,
              rk_mu, rk_w_rkv, rk_w0, rk_w1, rk_w2, rk_a0, rk_a1, rk_a2, rk_g1, rk_g2,
              rk_k_k, rk_k_a, rk_r_k, rk_lnx_g, rk_lnx_b, rk_w_o,
              pool_w, pool_scale,
              gdn_w_in, gdn_conv_w, gdn_a_log, gdn_dt_bias, gdn_norm_w, gdn_w_o,
              ffn_w1, ffn_w3, ffn_w2,
              moe_router_w, moe_router_b, moe_w1, moe_w3, moe_w2):
    b, n, d = x.shape
    n_rows = n // GRID_W
    for i in range(DEPTH):
        want_ctx = i < DEPTH - 1
        mod = jax.nn.silu(c) @ ada_w[i] + ada_b[i]
        mod_c = jax.nn.silu(c_ctx) @ ada_w[i] + ada_b[i]
        sh1, sc1, gt1, sh2, sc2, gt2 = jnp.split(mod[:, None, :], 6, axis=-1)
        csh1, csc1, cgt1, csh2, csc2, cgt2 = jnp.split(mod_c, 6, axis=-1)
        h = x * (1.0 + sc1) + sh1
        hc = ctx * (1.0 + csc1) + csh1
        kind, j = i % N_MIXERS, i // N_MIXERS
        if kind == 0:
            y, yc = rwkv7_mixer(h, hc, n_rows, rk_mu[j], rk_w_rkv[j], rk_w0[j], rk_w1[j], rk_w2[j],
                                rk_a0[j], rk_a1[j], rk_a2[j], rk_g1[j], rk_g2[j], rk_k_k[j], rk_k_a[j],
                                rk_r_k[j], rk_lnx_g[j], rk_lnx_b[j], rk_w_o[j], want_ctx)
        elif kind == 1:
            y = pool_mixer(h.reshape(b, n_rows, GRID_W, d), pool_w[j], pool_scale[j]).reshape(b, n, d)
            yc = pool_mixer(hc, pool_w[j], pool_scale[j]) if want_ctx else None
        else:
            y, yc = gdn_mixer(h, hc, gdn_w_in[j], gdn_conv_w[j], gdn_a_log[j], gdn_dt_bias[j],
                              gdn_norm_w[j], gdn_w_o[j], want_ctx)
        if i % 2 == 0:
            e = i // 2
            channel_mixer = lambda t, e=e: swiglu(t, ffn_w1[e], ffn_w3[e], ffn_w2[e])
        else:
            e = i // 2
            channel_mixer = lambda t, e=e: moe_swiglu(t, moe_router_w[e], moe_router_b[e],
                                                      moe_w1[e], moe_w3[e], moe_w2[e])
        x = layer_norm(DN_ALPHA * x + gt1 * y, ln_g[i, 0], ln_b[i, 0])
        x = layer_norm(DN_ALPHA * x + gt2 * channel_mixer(x * (1.0 + sc2) + sh2), ln_g[i, 1], ln_b[i, 1])
        if want_ctx:
            ctx = layer_norm(DN_ALPHA * ctx + cgt1 * yc, ln_g[i, 0], ln_b[i, 0])
            ctx = layer_norm(DN_ALPHA * ctx + cgt2 * channel_mixer(ctx * (1.0 + csc2) + csh2),
                             ln_g[i, 1], ln_b[i, 1])
    return x
```

```python
import functools
import math

import jax
import jax.numpy as jnp
import numpy as np
from jax import lax
from jax.experimental import pallas as pl
from jax.experimental.pallas import tpu as pltpu

F32 = jnp.float32
BF16 = jnp.bfloat16

GRID_W = 64
CHUNK = 64
LANES = 128
POOL_WINDOWS = (2, 4, 8, 16)
LN_EPS = 1e-5
RK_DECAY_SCALE = math.exp(-0.5)
RK_GN_EPS = 64e-5
GDN_NORM_EPS = 1e-6
GDN_L2_EPS = 1e-6
TOP_K = 2
VMEM_LIMIT = 56 * 1024 * 1024


def _sigmoid(x):
    return 1.0 / (1.0 + jnp.exp(-x))


def _silu(x):
    return x * _sigmoid(x)


def _softplus(x):
    return jnp.maximum(x, 0.0) + jnp.log(1.0 + jnp.exp(-jnp.abs(x)))


def _dot(a, b):
    return jnp.dot(a.astype(BF16), b.astype(BF16), preferred_element_type=F32)


def _dot_nt(a, b):
    return lax.dot_general(a.astype(BF16), b.astype(BF16), (((1,), (1,)), ((), ())),
                           preferred_element_type=F32)


def _dot_tn(a, b):
    return lax.dot_general(a.astype(BF16), b.astype(BF16), (((0,), (0,)), ((), ())),
                           preferred_element_type=F32)


def _split(x):
    hi = x.astype(BF16)
    lo = (x - hi.astype(F32)).astype(BF16)
    return hi, lo


def _dot_lhs2(a, b_exact):
    hi, lo = _split(a)
    return (jnp.dot(hi, b_exact, preferred_element_type=F32)
            + jnp.dot(lo, b_exact, preferred_element_type=F32))


def _dot_rhs2(a_exact, b):
    hi, lo = _split(b)
    return (jnp.dot(a_exact, hi, preferred_element_type=F32)
            + jnp.dot(a_exact, lo, preferred_element_type=F32))


def _dot3(a, b):
    ah, al = _split(a)
    bh, bl = _split(b)
    return (jnp.dot(ah, bh, preferred_element_type=F32)
            + jnp.dot(ah, bl, preferred_element_type=F32)
            + jnp.dot(al, bh, preferred_element_type=F32))


def _segsum(x, e_ref, et_ref):
    s = _dot_lhs2(x, e_ref[...])
    return _dot_lhs2(s, et_ref[...])


def _layer_norm(v, g, b):
    mean = jnp.mean(v, axis=-1, keepdims=True)
    c = v - mean
    var = jnp.mean(c * c, axis=-1, keepdims=True)
    return c * lax.rsqrt(var + LN_EPS) * g + b


def _head_indicator(d_model, head):
    n_heads = d_model // head
    e = np.zeros((d_model, LANES), np.float32)
    e[np.arange(d_model), np.arange(d_model) // head] = 1.0
    assert n_heads <= LANES
    return jnp.asarray(e, BF16), jnp.asarray(e.T.copy(), BF16)


def _full(shape):
    n = len(shape)
    return pl.BlockSpec(shape, lambda *_: (0,) * n)


def _resident(shape):
    n = len(shape)
    return pl.BlockSpec(shape, lambda *_: (0,) * n, pipeline_mode=pl.Buffered(1))


def _params(sem):
    return pltpu.CompilerParams(dimension_semantics=sem, vmem_limit_bytes=VMEM_LIMIT)


class _Geom:
    def __init__(self, batch, seq, ctx_len, d_model):
        self.b, self.seq, self.ctx, self.d = batch, seq, ctx_len, d_model
        self.n_lat = batch * seq
        self.t = batch * (seq + ctx_len)

    def mod_row(self, i, tm):
        return jnp.where(i < self.n_lat // tm, i // (self.seq // tm), self.b)


def _ada_body(c_ref, w_ref, b_ref, o_ref):
    o_ref[...] = _dot3(_silu(c_ref[...]), w_ref[...]) + b_ref[...]


def _ada_table(cvec, ada_w, ada_b):
    depth, d, d6 = ada_w.shape
    rows = cvec.shape[0]
    return pl.pallas_call(
        _ada_body,
        grid=(depth, d6 // d),
        in_specs=[_full((rows, d)),
                  pl.BlockSpec((None, d, d), lambda l, j: (l, 0, j)),
                  pl.BlockSpec((None, 1, d), lambda l, j: (l, 0, j))],
        out_specs=pl.BlockSpec((None, rows, d), lambda l, j: (l, 0, j)),
        out_shape=jax.ShapeDtypeStruct((depth, rows, d6), F32),
        compiler_params=_params(("parallel", "parallel")),
        name="ada_table",
    )(cvec, ada_w, ada_b.reshape(depth, 1, d6))


def _ffn_body(x_ref, mod_ref, w1_ref, w3_ref, w2_ref, lng_ref, lnb_ref, o_ref, *, alpha, n_split):
    x = x_ref[...]
    sh, sc, gt = mod_ref[3:4, :], mod_ref[4:5, :], mod_ref[5:6, :]
    h = (x * (1.0 + sc) + sh).astype(BF16)
    ff = w1_ref.shape[1] // n_split
    acc = jnp.zeros(x.shape, F32)
    for s in range(n_split):
        a = jnp.dot(h, w1_ref[:, s * ff:(s + 1) * ff], preferred_element_type=F32)
        g = jnp.dot(h, w3_ref[:, s * ff:(s + 1) * ff], preferred_element_type=F32)
        acc = acc + jnp.dot((_silu(a) * g).astype(BF16), w2_ref[s * ff:(s + 1) * ff, :],
                            preferred_element_type=F32)
    o_ref[...] = _layer_norm(alpha * x + gt * acc, lng_ref[...], lnb_ref[...])


def _ffn_layer(geom, x, mods, w1, w3, w2, ln_g, ln_b, alpha, tm=256):
    t, d = x.shape
    ff = w1.shape[1]
    return pl.pallas_call(
        functools.partial(_ffn_body, alpha=alpha, n_split=2),
        grid=(t // tm,),
        in_specs=[pl.BlockSpec((tm, d), lambda i: (i, 0)),
                  pl.BlockSpec((None, 6, d), lambda i: (geom.mod_row(i, tm), 0, 0)),
                  _resident((d, ff)), _resident((d, ff)), _resident((ff, d)),
                  _full((1, d)), _full((1, d))],
        out_specs=pl.BlockSpec((tm, d), lambda i: (i, 0)),
        out_shape=jax.ShapeDtypeStruct((t, d), F32),
        compiler_params=_params(("parallel",)),
        name="ffn",
    )(x, mods, w1.astype(BF16), w3.astype(BF16), w2.astype(BF16),
      ln_g.reshape(1, d), ln_b.reshape(1, d))


def _moe_body(x_ref, mod_ref, rw_ref, rb_ref, w1_ref, w3_ref, w2_ref, lng_ref, lnb_ref, o_ref,
              h_scr, gate_scr, acc_scr, *, alpha, n_experts):
    e = pl.program_id(1)

    @pl.when(e == 0)
    def _():
        x = x_ref[...]
        sh, sc = mod_ref[3:4, :], mod_ref[4:5, :]
        h = x * (1.0 + sc) + sh
        h_scr[...] = h.astype(BF16)
        logits = _dot3(h, rw_ref[...]) + rb_ref[...]
        lane = lax.broadcasted_iota(jnp.int32, logits.shape, 1)
        neg = jnp.float32(-jnp.inf)
        logits = jnp.where(lane < n_experts, logits, neg)
        m1 = jnp.max(logits, axis=-1, keepdims=True)
        i1 = jnp.min(jnp.where(logits == m1, lane, LANES), axis=-1, keepdims=True)
        rest = jnp.where(lane == i1, neg, logits)
        m2 = jnp.max(rest, axis=-1, keepdims=True)
        i2 = jnp.min(jnp.where(rest == m2, lane, LANES), axis=-1, keepdims=True)
        e2 = jnp.exp(m2 - m1)
        p1 = 1.0 / (1.0 + e2)
        p2 = e2 / (1.0 + e2)
        gate_scr[...] = jnp.where(lane == i1, p1, 0.0) + jnp.where(lane == i2, p2, 0.0)
        acc_scr[...] = jnp.zeros(acc_scr.shape, F32)

    h = h_scr[...]
    lane = lax.broadcasted_iota(jnp.int32, gate_scr.shape, 1)
    gate = jnp.sum(jnp.where(lane == e, gate_scr[...], 0.0), axis=-1, keepdims=True)
    a = jnp.dot(h, w1_ref[...], preferred_element_type=F32)
    g = jnp.dot(h, w3_ref[...], preferred_element_type=F32)
    y = jnp.dot((_silu(a) * g).astype(BF16), w2_ref[...], preferred_element_type=F32)
    acc_scr[...] += gate * y

    @pl.when(e == n_experts - 1)
    def _():
        x = x_ref[...]
        gt = mod_ref[5:6, :]
        o_ref[...] = _layer_norm(alpha * x + gt * acc_scr[...], lng_ref[...], lnb_ref[...])


def _moe_layer(geom, x, mods, router_w, router_b, w1, w3, w2, ln_g, ln_b, alpha, tm=512):
    t, d = x.shape
    n_e, _, ffe = w1.shape
    rw = jnp.zeros((d, LANES), F32).at[:, :n_e].set(router_w)
    rb = jnp.zeros((1, LANES), F32).at[0, :n_e].set(router_b)
    return pl.pallas_call(
        functools.partial(_moe_body, alpha=alpha, n_experts=n_e),
        grid=(t // tm, n_e),
        in_specs=[pl.BlockSpec((tm, d), lambda i, e: (i, 0)),
                  pl.BlockSpec((None, 6, d), lambda i, e: (geom.mod_row(i, tm), 0, 0)),
                  _full((d, LANES)), _full((1, LANES)),
                  pl.BlockSpec((None, d, ffe), lambda i, e: (e, 0, 0)),
                  pl.BlockSpec((None, d, ffe), lambda i, e: (e, 0, 0)),
                  pl.BlockSpec((None, ffe, d), lambda i, e: (e, 0, 0)),
                  _full((1, d)), _full((1, d))],
        out_specs=pl.BlockSpec((tm, d), lambda i, e: (i, 0)),
        out_shape=jax.ShapeDtypeStruct((t, d), F32),
        scratch_shapes=[pltpu.VMEM((tm, d), BF16), pltpu.VMEM((tm, LANES), F32),
                        pltpu.VMEM((tm, d), F32)],
        compiler_params=_params(("parallel", "arbitrary")),
        name="moe",
    )(x, mods, rw, rb, w1.astype(BF16), w3.astype(BF16), w2.astype(BF16),
      ln_g.reshape(1, d), ln_b.reshape(1, d))


def _pool_matrices(tm, seq_len):
    t = np.arange(tm)
    pos = t % seq_len
    mats, inv = [], []
    for win in POOL_WINDOWS:
        lo = np.clip(pos - win // 2, 0, seq_len)
        hi = np.clip(pos + win // 2, 0, seq_len)
        base = t - pos
        j = t[None, :]
        mats.append(((j >= (base + lo)[:, None]) & (j < (base + hi)[:, None])).astype(np.float32))
        inv.append((hi - lo).astype(np.float32))
    return np.stack(mats), np.stack(inv)


def _pool_body(x_ref, mod_ref, pm_ref, cnt_ref, pw_ref, ps_ref, lng_ref, lnb_ref, o_ref, *, alpha):
    x = x_ref[...]
    sh, sc, gt = mod_ref[0:1, :], mod_ref[1:2, :], mod_ref[2:3, :]
    h = x * (1.0 + sc) + sh
    n_g = pm_ref.shape[0]
    gw = x.shape[1] // n_g
    outs = []
    for g in range(n_g):
        hg = h[:, g * gw:(g + 1) * gw]
        total = _dot_rhs2(pm_ref[g], hg)
        pooled = total / cnt_ref[g] - hg
        outs.append(_dot(pooled, pw_ref[g]))
    y = jnp.concatenate(outs, axis=-1) * ps_ref[...]
    o_ref[...] = _layer_norm(alpha * x + gt * y, lng_ref[...], lnb_ref[...])


def _pool_layer(geom, x, mods, pool_w, pool_scale, ln_g, ln_b, alpha, tm=256):
    t, d = x.shape
    n_g = len(POOL_WINDOWS)
    m_lat, c_lat = _pool_matrices(tm, GRID_W)
    m_ctx, c_ctx = _pool_matrices(tm, geom.ctx)
    pm = jnp.asarray(np.stack([m_lat, m_ctx]), BF16)
    cnt = jnp.asarray(np.stack([c_lat, c_ctx])[..., None], F32)
    n_lat_tiles = geom.n_lat // tm
    kind = lambda i: jnp.where(i < n_lat_tiles, 0, 1)
    return pl.pallas_call(
        functools.partial(_pool_body, alpha=alpha),
        grid=(t // tm,),
        in_specs=[pl.BlockSpec((tm, d), lambda i: (i, 0)),
                  pl.BlockSpec((None, 6, d), lambda i: (geom.mod_row(i, tm), 0, 0)),
                  pl.BlockSpec((None, n_g, tm, tm), lambda i: (kind(i), 0, 0, 0)),
                  pl.BlockSpec((None, n_g, tm, 1), lambda i: (kind(i), 0, 0, 0)),
                  _full((n_g, d // n_g, d // n_g)), _full((1, d)), _full((1, d)), _full((1, d))],
        out_specs=pl.BlockSpec((tm, d), lambda i: (i, 0)),
        out_shape=jax.ShapeDtypeStruct((t, d), F32),
        compiler_params=_params(("parallel",)),
        name="pool",
    )(x, mods, pm, cnt, pool_w.astype(BF16), pool_scale.reshape(1, d),
      ln_g.reshape(1, d), ln_b.reshape(1, d))


def _scan_chunk_index(geom, b, d, s):
    nc_ctx = geom.ctx // CHUNK
    nc_lat = geom.seq // CHUNK
    c_ctx = jnp.where(d == 0, s, nc_ctx - 1 - s)
    c_lat = jnp.where(d == 0, s - nc_ctx, nc_lat - 1 - (s - nc_ctx))
    return jnp.where(s < nc_ctx, geom.b * nc_lat + b * nc_ctx + c_ctx, b * nc_lat + c_lat)


def _order_masks(d):
    n = 2 * CHUNK
    ri = lax.broadcasted_iota(jnp.int32, (n, n), 0)
    ci = lax.broadcasted_iota(jnp.int32, (n, n), 1)
    rt, ct = ri & (CHUNK - 1), ci & (CHUNK - 1)
    ahead = jnp.where(d == 0, rt - ct, ct - rt)
    return ahead > 0, ahead >= 0, ri < CHUNK, ci < CHUNK


INV_BASE = 8


def _inverse_masks(n):
    ri = lax.broadcasted_iota(jnp.int32, (n, n), 0)
    ci = lax.broadcasted_iota(jnp.int32, (n, n), 1)
    eye = (ri == ci).astype(F32)
    same = lambda size: (ri // size) == (ci // size)
    base = same(INV_BASE)
    levels = []
    size = INV_BASE
    while size < CHUNK:
        levels.append(same(2 * size) & ~same(size))
        size *= 2
    return eye, base, levels


def _unit_tri_inverse(a, masks):
    eye, base, levels = masks
    p = jnp.where(base, -a, 0.0)
    t = eye + p
    for _ in range(int(math.log2(INV_BASE)) - 1):
        p = _dot(p, p)
        t = t + _dot(t, p)
    for off in levels:
        t = t - _dot(t, _dot(jnp.where(off, a, 0.0), t))
    return t


def _cum_matrices():
    i = np.arange(CHUNK)
    fwd = (i[None, :] <= i[:, None]).astype(np.float32)
    return jnp.asarray(np.stack([fwd, fwd.T]), BF16)


def _rk_feat_body(x_ref, xp_ref, xn_ref, mod_ref, mu_ref, wrkv_ref, w1_ref, w2_ref, w0_ref,
                  a1_ref, a2_ref, a0_ref, g1_ref, g2_ref, kk_ref, ka_ref, rk_ref, e_ref, et_ref,
                  r_out, kkn_out, v_out, gate_out, bonus_out, lw_out, b_out, kd_out,
                  *, n_lat_tiles, tiles_per_batch, seq):
    i = pl.program_id(0)
    tm, d = x_ref.shape
    q = d // 4
    sh, sc = mod_ref[0:1, :], mod_ref[1:2, :]
    h = x_ref[...] * (1.0 + sc) + sh
    hp = xp_ref[...] * (1.0 + sc) + sh
    hn = xn_ref[...] * (1.0 + sc) + sh
    is_ctx = i >= n_lat_tiles
    t = lax.broadcasted_iota(jnp.int32, (tm, 1), 0)
    col = t & (GRID_W - 1)
    pos = (i % tiles_per_batch) * tm + t
    to_end = (tm - 1) - t

    def prev_tok(z):
        return pltpu.roll(z, 1, 0)

    def next_tok(z):
        return pltpu.roll(z, tm - 1, 0)

    h0, h1, h2, h3 = (h[:, k * q:(k + 1) * q] for k in range(4))
    up = jnp.concatenate([hp[:, 2 * q:3 * q], h2[:tm - GRID_W]], axis=0)
    down = jnp.concatenate([h3[GRID_W:], hn[:, 3 * q:]], axis=0)
    edge0 = jnp.where(is_ctx, t, col)
    edge1 = jnp.where(is_ctx, t, (GRID_W - 1) - col)
    edge2 = jnp.where(is_ctx, to_end, jnp.maximum(pos - (GRID_W - 1), 0))
    edge3 = jnp.where(is_ctx, to_end, jnp.maximum((seq - GRID_W) - pos, 0))
    s0 = jnp.where(edge0 == 0, 0.0, prev_tok(h0))
    s1 = jnp.where(edge1 == 0, 0.0, jnp.where(is_ctx, prev_tok(h1), next_tok(h1)))
    s2 = jnp.where(edge2 == 0, 0.0, jnp.where(is_ctx, next_tok(h2), up))
    s3 = jnp.where(edge3 == 0, 0.0, jnp.where(is_ctx, next_tok(h3), down))
    xx = jnp.concatenate([s0, s1, s2, s3], axis=-1) - h
    xr, xw, xk, xv, xa, xg = (h + xx * mu_ref[m:m + 1, :] for m in range(6))

    r = _dot(xr, wrkv_ref[0])
    k = _dot(xk, wrkv_ref[1])
    v = _dot(xv, wrkv_ref[2])
    dl = jnp.tanh(_dot(xw, w1_ref[...]))
    al = _dot(xa, a1_ref[...])
    kkr = k * kk_ref[...]
    ss = _segsum(kkr * kkr, e_ref, et_ref)
    kkn = kkr / jnp.maximum(jnp.sqrt(ss), 1e-12)
    kd_sum = jnp.zeros_like(k)
    for z in range(2):
        lw = -RK_DECAY_SCALE * _sigmoid(w0_ref[z:z + 1, :] + _dot(dl, w2_ref[z]))
        a = _sigmoid(a0_ref[z:z + 1, :] + _dot(al, a2_ref[z]))
        kd = k * (1.0 + (a - 1.0) * ka_ref[...])
        lw_out[z] = lw
        b_out[z] = kkn * a
        kd_out[z] = kd
        kd_sum = kd_sum + kd
    coef = _segsum(r * kd_sum * rk_ref[...], e_ref, et_ref)
    r_out[...] = r
    kkn_out[...] = kkn
    v_out[...] = v
    bonus_out[...] = coef * v
    gate_out[...] = _dot(_sigmoid(_dot(xg, g1_ref[...])), g2_ref[...])


def _rk_scan_body(r_ref, kk_ref, v_ref, lw_ref, b_ref, kd_ref, cum_ref, y_ref, s_scr):
    d = pl.program_id(1)
    step = pl.program_id(2)
    c = CHUNK

    @pl.when(step == 0)
    def _():
        s_scr[...] = jnp.zeros(s_scr.shape, F32)

    lw = lw_ref[...]
    cs = _dot_rhs2(cum_ref[...], lw)
    cs_last = jnp.where(d == 0, cs[c - 1:c, :], cs[0:1, :])
    eg = jnp.exp(cs)
    en = jnp.exp(-cs)
    el = jnp.exp(cs_last - cs)
    rt = r_ref[...] * eg
    kkt = kk_ref[...] * jnp.exp(cs - lw)
    b, kd, v = b_ref[...], kd_ref[...], v_ref[...]
    bt, kt = b * en, kd * en
    bh, kh = b * el, kd * el
    gl = jnp.exp(cs_last)

    strict, incl, top, left = _order_masks(d)
    inv_masks = _inverse_masks(2 * c)
    lane = lax.broadcasted_iota(jnp.int32, (1, LANES), 1)
    m0, m1 = lane < c, lane >= c
    diag_blk = top == left
    zero = jnp.float32(0.0)
    tl, br = top & left, (~top) & (~left)
    tr, bl = top & (~left), (~top) & left
    m_bk0, m_bk1 = tl & strict, br & strict
    m_kk0, m_kk1 = tr & strict, bl & strict
    m_rb0, m_rb1 = bl & incl, tr & incl
    m_rk0, m_rk1 = br & incl, tl & incl

    n_pairs = r_ref.shape[1] // LANES
    for p in range(n_pairs):
        sl = slice(p * LANES, (p + 1) * LANES)
        rt_p, kkt_p, bt_p, kt_p, v_p = rt[:, sl], kkt[:, sl], bt[:, sl], kt[:, sl], v[:, sl]
        r0 = _dot_nt(jnp.concatenate([jnp.where(m0, kkt_p, zero), jnp.where(m0, rt_p, zero)], 0),
                     jnp.concatenate([bt_p, kt_p], 0))
        r1 = _dot_nt(jnp.concatenate([jnp.where(m1, rt_p, zero), jnp.where(m1, kkt_p, zero)], 0),
                     jnp.concatenate([kt_p, bt_p], 0))
        a_bk = jnp.where(m_bk0, r0, zero) + jnp.where(m_bk1, r1, zero)
        a_kk = jnp.where(m_kk0, r0, zero) + jnp.where(m_kk1, r1, zero)
        a_rb = jnp.where(m_rb0, r0, zero) + jnp.where(m_rb1, r1, zero)
        a_rk = jnp.where(m_rk0, r0, zero) + jnp.where(m_rk1, r1, zero)
        t_inv = _unit_tri_inverse(a_bk, inv_masks)

        s_p = s_scr[p]
        proj = _dot_nt(jnp.concatenate([kkt_p, rt_p], 0), s_p)
        ks, rs = proj[:c], proj[c:]
        v_sw = jnp.concatenate([jnp.where(m1, v_p, zero), jnp.where(m0, v_p, zero)], 0)
        ks_sm = jnp.concatenate([jnp.where(m0, ks, zero), jnp.where(m1, ks, zero)], 0)
        u_sm = -_dot(t_inv, ks_sm + _dot(a_kk, v_sw))
        y_sm = _dot(jnp.concatenate([a_rb, a_rk], 1), jnp.concatenate([u_sm, v_sw], 0))
        y_ref[:, sl] = rs + y_sm[:c] + y_sm[c:]
        u_p = u_sm[:c] + u_sm[c:]
        upd = _dot_tn(jnp.concatenate([u_p, v_p], 0), jnp.concatenate([bh[:, sl], kh[:, sl]], 0))
        s_scr[p] = s_p * gl[:, sl] + jnp.where(diag_blk, upd, zero)


def _rk_out_body(x_ref, y_ref, bonus_ref, gate_ref, mod_ref, lxg_ref, lxb_ref, wo_ref, e_ref, et_ref,
                 lng_ref, lnb_ref, o_ref, *, alpha, head):
    x = x_ref[...]
    y = y_ref[0] + y_ref[1]
    mean = _segsum(y, e_ref, et_ref) * (1.0 / head)
    cen = y - mean
    var = _segsum(cen * cen, e_ref, et_ref) * (1.0 / head)
    yn = cen * lax.rsqrt(var + RK_GN_EPS) * lxg_ref[...] + lxb_ref[...]
    out = _dot((yn + bonus_ref[...]) * gate_ref[...], wo_ref[...])
    gt = mod_ref[2:3, :]
    o_ref[...] = _layer_norm(alpha * x + gt * out, lng_ref[...], lnb_ref[...])


def _rwkv_layer(geom, x, mods, mu, w_rkv, w0, w1, w2, a0, a1, a2, g1, g2, k_k, k_a, r_k,
                lnx_g, lnx_b, w_o, ln_g, ln_b, alpha, tm=256):
    t, d = x.shape
    head = r_k.shape[-1]
    assert 2 * head == LANES and geom.ctx == tm and geom.seq % tm == 0
    lora_d, lora_a, lora_g = w1.shape[-1], a1.shape[-1], g1.shape[-1]
    e, et = _head_indicator(d, head)

    def cat_dirs(w):
        return jnp.concatenate([w[0], w[1]], axis=-1).astype(BF16)

    def pad_dirs(w):
        z = jnp.zeros_like(w[0])
        return jnp.stack([jnp.concatenate([w[0], z], 0), jnp.concatenate([z, w[1]], 0)]).astype(BF16)

    nb = t // GRID_W
    row = lambda i: (i, 0)
    tile = pl.BlockSpec((tm, d), row)
    tile2 = pl.BlockSpec((2, tm, d), lambda i: (0, i, 0))
    mod_spec = pl.BlockSpec((None, 6, d), lambda i: (geom.mod_row(i, tm), 0, 0))
    per = tm // GRID_W
    feat = pl.pallas_call(
        functools.partial(_rk_feat_body, n_lat_tiles=geom.n_lat // tm,
                          tiles_per_batch=geom.seq // tm, seq=geom.seq),
        grid=(t // tm,),
        in_specs=[tile,
                  pl.BlockSpec((GRID_W, d), lambda i: (jnp.maximum(i * per - 1, 0), 0)),
                  pl.BlockSpec((GRID_W, d), lambda i: (jnp.minimum(i * per + per, nb - 1), 0)),
                  mod_spec, _full((6, d)), _resident((3, d, d)),
                  _full((d, 2 * lora_d)), _full((2, 2 * lora_d, d)), _full((2, d)),
                  _full((d, 2 * lora_a)), _full((2, 2 * lora_a, d)), _full((2, d)),
                  _full((d, lora_g)), _full((lora_g, d)),
                  _full((1, d)), _full((1, d)), _full((1, d)), _full((d, LANES)), _full((LANES, d))],
        out_specs=[tile, tile, tile, tile, tile, tile2, tile2, tile2],
        out_shape=[jax.ShapeDtypeStruct((t, d), F32)] * 5 + [jax.ShapeDtypeStruct((2, t, d), F32)] * 3,
        compiler_params=_params(("parallel",)),
        name="rwkv_feat",
    )
    r, kkn, v, gate, bonus, lw, bb, kd = feat(
        x, x, x, mods, mu, w_rkv.astype(BF16), cat_dirs(w1), pad_dirs(w2), w0,
        cat_dirs(a1), pad_dirs(a2), a0, g1.astype(BF16), g2.astype(BF16),
        k_k.reshape(1, d), k_a.reshape(1, d), r_k.reshape(1, d), e, et)

    n_steps = (geom.ctx + geom.seq) // CHUNK
    cidx = functools.partial(_scan_chunk_index, geom)
    shared = pl.BlockSpec((CHUNK, d), lambda b, z, s: (cidx(b, z, s), 0))
    per_dir = pl.BlockSpec((None, CHUNK, d), lambda b, z, s: (z, cidx(b, z, s), 0))
    y = pl.pallas_call(
        _rk_scan_body,
        grid=(geom.b, 2, n_steps),
        in_specs=[shared, shared, shared, per_dir, per_dir, per_dir,
                  pl.BlockSpec((None, CHUNK, CHUNK), lambda b, z, s: (z, 0, 0))],
        out_specs=per_dir,
        out_shape=jax.ShapeDtypeStruct((2, t, d), F32),
        scratch_shapes=[pltpu.VMEM((d // LANES, LANES, LANES), F32)],
        compiler_params=_params(("arbitrary", "arbitrary", "arbitrary")),
        name="rwkv_scan",
    )(r, kkn, v, lw, bb, kd, _cum_matrices())

    return pl.pallas_call(
        functools.partial(_rk_out_body, alpha=alpha, head=head),
        grid=(t // tm,),
        in_specs=[tile, tile2, tile, tile, mod_spec, _full((1, d)), _full((1, d)), _resident((d, d)),
                  _full((d, LANES)), _full((LANES, d)), _full((1, d)), _full((1, d))],
        out_specs=tile,
        out_shape=jax.ShapeDtypeStruct((t, d), F32),
        compiler_params=_params(("parallel",)),
        name="rwkv_out",
    )(x, y, bonus, gate, mods, lnx_g.reshape(1, d), lnx_b.reshape(1, d), w_o.astype(BF16), e, et,
      ln_g.reshape(1, d), ln_b.reshape(1, d))


HALO = 8


def _gdn_feat_body(x_ref, xp_ref, xn_ref, mod_ref, wq_ref, wz_ref, wab_ref, cw_ref, alog_ref, dtb_ref,
                   q_out, k_out, v_out, z_out, gb_out, proj_scr,
                   *, n_lat_tiles, tiles_per_batch, head, n_taps):
    i = pl.program_id(0)
    tm, d = x_ref.shape
    sh, sc = mod_ref[0:1, :], mod_ref[1:2, :]
    h = x_ref[...] * (1.0 + sc) + sh
    is_ctx = i >= n_lat_tiles
    j = i % tiles_per_batch
    has_prev = jnp.logical_and(jnp.logical_not(is_ctx), j > 0)
    has_next = jnp.logical_and(jnp.logical_not(is_ctx), j < tiles_per_batch - 1)
    hp = jnp.where(has_prev, xp_ref[...] * (1.0 + sc) + sh, 0.0)
    hn = jnp.where(has_next, xn_ref[...] * (1.0 + sc) + sh, 0.0)
    hb = h.astype(BF16)
    proj_scr[0:HALO, :] = _dot(hp, wq_ref[...])
    proj_scr[HALO:HALO + tm, :] = jnp.dot(hb, wq_ref[...], preferred_element_type=F32)
    proj_scr[HALO + tm:, :] = _dot(hn, wq_ref[...])
    left = n_taps // 2
    conv = jnp.zeros((tm, 3 * d), F32)
    for tap in range(n_taps):
        conv = conv + proj_scr[pl.ds(HALO + tap - left, tm), :] * cw_ref[tap:tap + 1, :]
    qkv = _silu(conv)
    n_heads = d // head
    for hd in range(n_heads):
        qs = qkv[:, hd * head:(hd + 1) * head]
        ks = qkv[:, d + hd * head:d + (hd + 1) * head]
        q_out[:, hd * head:(hd + 1) * head] = (
            qs * lax.rsqrt(jnp.sum(qs * qs, axis=-1, keepdims=True) + GDN_L2_EPS) * head ** -0.5)
        k_out[:, hd * head:(hd + 1) * head] = (
            ks * lax.rsqrt(jnp.sum(ks * ks, axis=-1, keepdims=True) + GDN_L2_EPS))
    v_out[...] = qkv[:, 2 * d:]
    z_out[...] = jnp.dot(hb, wz_ref[...], preferred_element_type=F32)
    ab = _dot3(h, wab_ref[...])
    lane = lax.broadcasted_iota(jnp.int32, ab.shape, 1)
    g = -jnp.exp(alog_ref[...]) * _softplus(ab + dtb_ref[...])
    gb_out[...] = jnp.where(lane < 2 * n_heads, g, _sigmoid(ab))


def _gdn_scan_body(q_ref, k_ref, v_ref, gb_ref, y_ref, s_scr, *, n_heads):
    d = pl.program_id(1)
    step = pl.program_id(2)
    c = CHUNK
    head = q_ref.shape[1] // n_heads

    @pl.when(step == 0)
    def _():
        s_scr[...] = jnp.zeros(s_scr.shape, F32)

    strict, incl, top, left = _order_masks(d)
    inv_masks = _inverse_masks(2 * c)
    same = top == left
    cum_bd = jnp.where(same & incl, 1.0, 0.0).astype(BF16)
    nxt_bd = jnp.where(same & strict, 1.0, 0.0)
    zero = jnp.float32(0.0)
    gb = gb_ref[...]
    row = lax.broadcasted_iota(jnp.int32, (2 * c, 1), 0)
    last_row = jnp.where(d == 0, c - 1, 0)

    for p in range(n_heads // 2):
        h0, h1 = 2 * p, 2 * p + 1
        g_col = jnp.concatenate([_lane_col(gb, d * n_heads + h0), _lane_col(gb, d * n_heads + h1)], 0)
        beta = jnp.concatenate([_lane_col(gb, (2 + d) * n_heads + h0),
                                _lane_col(gb, (2 + d) * n_heads + h1)], 0)
        stack = lambda ref: jnp.concatenate(
            [ref[:, h0 * head:(h0 + 1) * head], ref[:, h1 * head:(h1 + 1) * head]], 0)
        q, k, v = stack(q_ref), stack(k_ref), stack(v_ref)
        cs = _dot_rhs2(cum_bd, jnp.broadcast_to(g_col, (2 * c, LANES)))[:, 0:1]
        dlog = _dot_rhs2(cum_bd, g_col * nxt_bd)
        decay = jnp.exp(jnp.where(same & incl, dlog, -jnp.inf))
        cs_l0 = jnp.sum(jnp.where(row == last_row, cs, zero), axis=0, keepdims=True)
        cs_l1 = jnp.sum(jnp.where(row == last_row + c, cs, zero), axis=0, keepdims=True)
        cs_end = jnp.where(row < c, cs_l0, cs_l1)
        kb, vb = k * beta, v * beta
        kkt = _dot_nt(kb, k)
        lower = jnp.where(same & strict, kkt * decay, zero)
        t_inv = _unit_tri_inverse(lower, inv_masks)
        eg = jnp.exp(cs)
        uw = _dot(t_inv, jnp.concatenate([vb, kb * eg], axis=1))
        u, w = uw[:, :head], uw[:, head:]
        a_qk = jnp.where(same & incl, _dot_nt(q, k) * decay, zero)
        q_dec = q * eg
        k_dec = k * jnp.exp(cs_end - cs)
        s0, s1 = s_scr[h0], s_scr[h1]
        ws = jnp.concatenate([_dot(w[:c], s0), _dot(w[c:], s1)], 0)
        v_new = u - ws
        qs = jnp.concatenate([_dot(q_dec[:c], s0), _dot(q_dec[c:], s1)], 0)
        o = qs + _dot(a_qk, v_new)
        y_ref[:, h0 * head:(h0 + 1) * head] = o[:c]
        y_ref[:, h1 * head:(h1 + 1) * head] = o[c:]
        s_scr[h0] = s0 * jnp.exp(cs_l0) + _dot_tn(k_dec[:c], v_new[:c])
        s_scr[h1] = s1 * jnp.exp(cs_l1) + _dot_tn(k_dec[c:], v_new[c:])


def _lane_col(x, idx):
    lane = lax.broadcasted_iota(jnp.int32, x.shape, 1)
    return jnp.sum(jnp.where(lane == idx, x, 0.0), axis=-1, keepdims=True)


def _gdn_out_body(x_ref, o_ref_in, z_ref, mod_ref, nw_ref, wo_ref, lng_ref, lnb_ref, out_ref,
                  *, alpha, head):
    x = x_ref[...]
    o = o_ref_in[0] + o_ref_in[1]
    z = z_ref[...]
    d = x.shape[1]
    parts = []
    for hd in range(d // head):
        oh = o[:, hd * head:(hd + 1) * head]
        on = oh * lax.rsqrt(jnp.mean(oh * oh, axis=-1, keepdims=True) + GDN_NORM_EPS) * nw_ref[...]
        parts.append(on * _silu(z[:, hd * head:(hd + 1) * head]))
    y = _dot(jnp.concatenate(parts, axis=-1), wo_ref[...])
    gt = mod_ref[2:3, :]
    out_ref[...] = _layer_norm(alpha * x + gt * y, lng_ref[...], lnb_ref[...])


def _gdn_layer(geom, x, mods, w_in, conv_w, a_log, dt_bias, norm_w, w_o, ln_g, ln_b, alpha, tm=256):
    t, d = x.shape
    n_heads = a_log.shape[-1]
    head = d // n_heads
    n_taps = conv_w.shape[0]
    assert head == LANES and geom.ctx == tm and geom.seq % tm == 0 and n_heads % 2 == 0
    assert 4 * n_heads <= LANES
    w_qkv = w_in[:, :3 * d].astype(BF16)
    w_z = w_in[:, 3 * d:4 * d].astype(BF16)
    w_ab = jnp.zeros((d, LANES), F32).at[:, :4 * n_heads].set(w_in[:, 4 * d:])
    alog = jnp.zeros((1, LANES), F32).at[0, :2 * n_heads].set(a_log.reshape(-1))
    dtb = jnp.zeros((1, LANES), F32).at[0, :2 * n_heads].set(dt_bias.reshape(-1))

    nb = t // HALO
    per = tm // HALO
    tile = pl.BlockSpec((tm, d), lambda i: (i, 0))
    tile2 = pl.BlockSpec((2, tm, d), lambda i: (0, i, 0))
    small = pl.BlockSpec((tm, LANES), lambda i: (i, 0))
    mod_spec = pl.BlockSpec((None, 6, d), lambda i: (geom.mod_row(i, tm), 0, 0))
    q, k, v, z, gb = pl.pallas_call(
        functools.partial(_gdn_feat_body, n_lat_tiles=geom.n_lat // tm,
                          tiles_per_batch=geom.seq // tm, head=head, n_taps=n_taps),
        grid=(t // tm,),
        in_specs=[tile,
                  pl.BlockSpec((HALO, d), lambda i: (jnp.maximum(i * per - 1, 0), 0)),
                  pl.BlockSpec((HALO, d), lambda i: (jnp.minimum(i * per + per, nb - 1), 0)),
                  mod_spec, _resident((d, 3 * d)), _resident((d, d)), _full((d, LANES)),
                  _full((n_taps, 3 * d)), _full((1, LANES)), _full((1, LANES))],
        out_specs=[tile, tile, tile, tile, small],
        out_shape=[jax.ShapeDtypeStruct((t, d), F32)] * 4 + [jax.ShapeDtypeStruct((t, LANES), F32)],
        scratch_shapes=[pltpu.VMEM((tm + 2 * HALO, 3 * d), F32)],
        compiler_params=_params(("parallel",)),
        name="gdn_feat",
    )(x, x, x, mods, w_qkv, w_z, w_ab, conv_w, alog, dtb)

    n_steps = (geom.ctx + geom.seq) // CHUNK
    cidx = functools.partial(_scan_chunk_index, geom)
    shared = pl.BlockSpec((CHUNK, d), lambda b, zz, s: (cidx(b, zz, s), 0))
    o = pl.pallas_call(
        functools.partial(_gdn_scan_body, n_heads=n_heads),
        grid=(geom.b, 2, n_steps),
        in_specs=[shared, shared, shared,
                  pl.BlockSpec((CHUNK, LANES), lambda b, zz, s: (cidx(b, zz, s), 0))],
        out_specs=pl.BlockSpec((None, CHUNK, d), lambda b, zz, s: (zz, cidx(b, zz, s), 0)),
        out_shape=jax.ShapeDtypeStruct((2, t, d), F32),
        scratch_shapes=[pltpu.VMEM((n_heads, head, head), F32)],
        compiler_params=_params(("arbitrary", "arbitrary", "arbitrary")),
        name="gdn_scan",
    )(q, k, v, gb)

    return pl.pallas_call(
        functools.partial(_gdn_out_body, alpha=alpha, head=head),
        grid=(t // tm,),
        in_specs=[tile, tile2, tile, mod_spec, _full((1, head)), _resident((d, d)),
                  _full((1, d)), _full((1, d))],
        out_specs=tile,
        out_shape=jax.ShapeDtypeStruct((t, d), F32),
        compiler_params=_params(("parallel",)),
        name="gdn_out",
    )(x, o, z, mods, norm_w.reshape(1, head), w_o.astype(BF16), ln_g.reshape(1, d), ln_b.reshape(1, d))


def kernel(x, c, ctx, c_ctx, ada_w, ada_b, ln_g, ln_b, rk_mu, rk_w_rkv, rk_w0, rk_w1, rk_w2, rk_a0, rk_a1, rk_a2, rk_g1, rk_g2, rk_k_k, rk_k_a, rk_r_k, rk_lnx_g, rk_lnx_b, rk_w_o, pool_w, pool_scale, gdn_w_in, gdn_conv_w, gdn_a_log, gdn_dt_bias, gdn_norm_w, gdn_w_o, ffn_w1, ffn_w3, ffn_w2, moe_router_w, moe_router_b, moe_w1, moe_w3, moe_w2):
    batch, seq, d = x.shape
    ctx_len = ctx.shape[1]
    depth = ada_w.shape[0]
    geom = _Geom(batch, seq, ctx_len, d)
    alpha = (2 * depth) ** 0.25
    n_mixers = 3

    rows = -(-(batch + 1) // 8) * 8
    cvec = jnp.zeros((rows, d), F32).at[:batch].set(c).at[batch].set(c_ctx)
    table = _ada_table(cvec, ada_w, ada_b)
    mods_all = table[:, :batch + 1].reshape(depth, batch + 1, 6, d)

    xs = jnp.concatenate([x.reshape(batch * seq, d), ctx.reshape(batch * ctx_len, d)], axis=0)
    for i in range(depth):
        mods = mods_all[i]
        kind, j = i % n_mixers, i // n_mixers
        if kind == 0:
            xs = _rwkv_layer(geom, xs, mods, rk_mu[j], rk_w_rkv[j], rk_w0[j], rk_w1[j], rk_w2[j],
                             rk_a0[j], rk_a1[j], rk_a2[j], rk_g1[j], rk_g2[j], rk_k_k[j], rk_k_a[j],
                             rk_r_k[j], rk_lnx_g[j], rk_lnx_b[j], rk_w_o[j], ln_g[i, 0], ln_b[i, 0], alpha)
        elif kind == 1:
            xs = _pool_layer(geom, xs, mods, pool_w[j], pool_scale[j], ln_g[i, 0], ln_b[i, 0], alpha)
        else:
            xs = _gdn_layer(geom, xs, mods, gdn_w_in[j], gdn_conv_w[j], gdn_a_log[j], gdn_dt_bias[j],
                            gdn_norm_w[j], gdn_w_o[j], ln_g[i, 0], ln_b[i, 0], alpha)
        e = i // 2
        if i % 2 == 0:
            xs = _ffn_layer(geom, xs, mods, ffn_w1[e], ffn_w3[e], ffn_w2[e], ln_g[i, 1], ln_b[i, 1], alpha)
        else:
            xs = _moe_layer(geom, xs, mods, moe_router_w[e], moe_router_b[e], moe_w1[e], moe_w3[e],
                            moe_w2[e], ln_g[i, 1], ln_b[i, 1], alpha)
    return xs[:batch * seq].reshape(batch, seq, d)
```

```python
import functools
import math

import jax
import jax.numpy as jnp
import numpy as np
from jax import lax
from jax.experimental import pallas as pl
from jax.experimental.pallas import tpu as pltpu

F32 = jnp.float32
BF16 = jnp.bfloat16

GRID_W = 64
CHUNK = 64
LANES = 128
POOL_WINDOWS = (2, 4, 8, 16)
LN_EPS = 1e-5
RK_DECAY_SCALE = math.exp(-0.5)
RK_GN_EPS = 64e-5
GDN_NORM_EPS = 1e-6
GDN_L2_EPS = 1e-6
TOP_K = 2
VMEM_LIMIT = 56 * 1024 * 1024


def _sigmoid(x):
    return 1.0 / (1.0 + jnp.exp(-x))


def _silu(x):
    return x * _sigmoid(x)


def _softplus(x):
    return jnp.maximum(x, 0.0) + jnp.log(1.0 + jnp.exp(-jnp.abs(x)))


def _dot(a, b):
    return jnp.dot(a.astype(BF16), b.astype(BF16), preferred_element_type=F32)


def _dot_nt(a, b):
    return lax.dot_general(a.astype(BF16), b.astype(BF16), (((1,), (1,)), ((), ())),
                           preferred_element_type=F32)


def _dot_tn(a, b):
    return lax.dot_general(a.astype(BF16), b.astype(BF16), (((0,), (0,)), ((), ())),
                           preferred_element_type=F32)


def _split(x):
    hi = x.astype(BF16)
    lo = (x - hi.astype(F32)).astype(BF16)
    return hi, lo


def _dot_lhs2(a, b_exact):
    hi, lo = _split(a)
    return (jnp.dot(hi, b_exact, preferred_element_type=F32)
            + jnp.dot(lo, b_exact, preferred_element_type=F32))


def _dot_rhs2(a_exact, b):
    hi, lo = _split(b)
    return (jnp.dot(a_exact, hi, preferred_element_type=F32)
            + jnp.dot(a_exact, lo, preferred_element_type=F32))


def _dot3(a, b):
    ah, al = _split(a)
    bh, bl = _split(b)
    return (jnp.dot(ah, bh, preferred_element_type=F32)
            + jnp.dot(ah, bl, preferred_element_type=F32)
            + jnp.dot(al, bh, preferred_element_type=F32))


def _segsum(x, e_ref, et_ref):
    s = _dot_lhs2(x, e_ref[...])
    return _dot_lhs2(s, et_ref[...])


def _layer_norm(v, g, b):
    mean = jnp.mean(v, axis=-1, keepdims=True)
    c = v - mean
    var = jnp.mean(c * c, axis=-1, keepdims=True)
    return c * lax.rsqrt(var + LN_EPS) * g + b


def _head_indicator(d_model, head):
    n_heads = d_model // head
    e = np.zeros((d_model, LANES), np.float32)
    e[np.arange(d_model), np.arange(d_model) // head] = 1.0
    assert n_heads <= LANES
    return jnp.asarray(e, BF16), jnp.asarray(e.T.copy(), BF16)


def _full(shape):
    n = len(shape)
    return pl.BlockSpec(shape, lambda *_: (0,) * n)


def _resident(shape):
    n = len(shape)
    return pl.BlockSpec(shape, lambda *_: (0,) * n, pipeline_mode=pl.Buffered(1))


def _params(sem):
    return pltpu.CompilerParams(dimension_semantics=sem, vmem_limit_bytes=VMEM_LIMIT)


class _Geom:
    def __init__(self, batch, seq, ctx_len, d_model):
        self.b, self.seq, self.ctx, self.d = batch, seq, ctx_len, d_model
        self.n_lat = batch * seq
        self.t = batch * (seq + ctx_len)

    def mod_row(self, i, tm):
        return jnp.where(i < self.n_lat // tm, i // (self.seq // tm), self.b)


def _ada_body(c_ref, w_ref, b_ref, o_ref):
    o_ref[...] = _dot3(_silu(c_ref[...]), w_ref[...]) + b_ref[...]


def _ada_table(cvec, ada_w, ada_b):
    depth, d, d6 = ada_w.shape
    rows = cvec.shape[0]
    return pl.pallas_call(
        _ada_body,
        grid=(depth, d6 // d),
        in_specs=[_full((rows, d)),
                  pl.BlockSpec((None, d, d), lambda l, j: (l, 0, j)),
                  pl.BlockSpec((None, 1, d), lambda l, j: (l, 0, j))],
        out_specs=pl.BlockSpec((None, rows, d), lambda l, j: (l, 0, j)),
        out_shape=jax.ShapeDtypeStruct((depth, rows, d6), F32),
        compiler_params=_params(("parallel", "parallel")),
        name="ada_table",
    )(cvec, ada_w, ada_b.reshape(depth, 1, d6))


def _ffn_body(x_ref, mod_ref, w1_ref, w3_ref, w2_ref, lng_ref, lnb_ref, o_ref, *, alpha, n_split):
    x = x_ref[...]
    sh, sc, gt = mod_ref[3:4, :], mod_ref[4:5, :], mod_ref[5:6, :]
    h = (x * (1.0 + sc) + sh).astype(BF16)
    ff = w1_ref.shape[1] // n_split
    acc = jnp.zeros(x.shape, F32)
    for s in range(n_split):
        a = jnp.dot(h, w1_ref[:, s * ff:(s + 1) * ff], preferred_element_type=F32)
        g = jnp.dot(h, w3_ref[:, s * ff:(s + 1) * ff], preferred_element_type=F32)
        acc = acc + jnp.dot((_silu(a) * g).astype(BF16), w2_ref[s * ff:(s + 1) * ff, :],
                            preferred_element_type=F32)
    o_ref[...] = _layer_norm(alpha * x + gt * acc, lng_ref[...], lnb_ref[...])


def _ffn_layer(geom, x, mods, w1, w3, w2, ln_g, ln_b, alpha, tm=256):
    t, d = x.shape
    ff = w1.shape[1]
    return pl.pallas_call(
        functools.partial(_ffn_body, alpha=alpha, n_split=2),
        grid=(t // tm,),
        in_specs=[pl.BlockSpec((tm, d), lambda i: (i, 0)),
                  pl.BlockSpec((None, 6, d), lambda i: (geom.mod_row(i, tm), 0, 0)),
                  _resident((d, ff)), _resident((d, ff)), _resident((ff, d)),
                  _full((1, d)), _full((1, d))],
        out_specs=pl.BlockSpec((tm, d), lambda i: (i, 0)),
        out_shape=jax.ShapeDtypeStruct((t, d), F32),
        compiler_params=_params(("parallel",)),
        name="ffn",
    )(x, mods, w1.astype(BF16), w3.astype(BF16), w2.astype(BF16),
      ln_g.reshape(1, d), ln_b.reshape(1, d))


def _moe_body(x_ref, mod_ref, rw_ref, rb_ref, w1_ref, w3_ref, w2_ref, lng_ref, lnb_ref, o_ref,
              h_scr, gate_scr, acc_scr, *, alpha, n_experts):
    e = pl.program_id(1)

    @pl.when(e == 0)
    def _():
        x = x_ref[...]
        sh, sc = mod_ref[3:4, :], mod_ref[4:5, :]
        h = x * (1.0 + sc) + sh
        h_scr[...] = h.astype(BF16)
        logits = _dot3(h, rw_ref[...]) + rb_ref[...]
        lane = lax.broadcasted_iota(jnp.int32, logits.shape, 1)
        neg = jnp.float32(-jnp.inf)
        logits = jnp.where(lane < n_experts, logits, neg)
        m1 = jnp.max(logits, axis=-1, keepdims=True)
        i1 = jnp.min(jnp.where(logits == m1, lane, LANES), axis=-1, keepdims=True)
        rest = jnp.where(lane == i1, neg, logits)
        m2 = jnp.max(rest, axis=-1, keepdims=True)
        i2 = jnp.min(jnp.where(rest == m2, lane, LANES), axis=-1, keepdims=True)
        e2 = jnp.exp(m2 - m1)
        p1 = 1.0 / (1.0 + e2)
        p2 = e2 / (1.0 + e2)
        gate_scr[...] = jnp.where(lane == i1, p1, 0.0) + jnp.where(lane == i2, p2, 0.0)
        acc_scr[...] = jnp.zeros(acc_scr.shape, F32)

    h = h_scr[...]
    lane = lax.broadcasted_iota(jnp.int32, gate_scr.shape, 1)
    gate = jnp.sum(jnp.where(lane == e, gate_scr[...], 0.0), axis=-1, keepdims=True)
    a = jnp.dot(h, w1_ref[...], preferred_element_type=F32)
    g = jnp.dot(h, w3_ref[...], preferred_element_type=F32)
    y = jnp.dot((_silu(a) * g).astype(BF16), w2_ref[...], preferred_element_type=F32)
    acc_scr[...] += gate * y

    @pl.when(e == n_experts - 1)
    def _():
        x = x_ref[...]
        gt = mod_ref[5:6, :]
        o_ref[...] = _layer_norm(alpha * x + gt * acc_scr[...], lng_ref[...], lnb_ref[...])


def _moe_layer(geom, x, mods, router_w, router_b, w1, w3, w2, ln_g, ln_b, alpha, tm=512):
    t, d = x.shape
    n_e, _, ffe = w1.shape
    rw = jnp.zeros((d, LANES), F32).at[:, :n_e].set(router_w)
    rb = jnp.zeros((1, LANES), F32).at[0, :n_e].set(router_b)
    return pl.pallas_call(
        functools.partial(_moe_body, alpha=alpha, n_experts=n_e),
        grid=(t // tm, n_e),
        in_specs=[pl.BlockSpec((tm, d), lambda i, e: (i, 0)),
                  pl.BlockSpec((None, 6, d), lambda i, e: (geom.mod_row(i, tm), 0, 0)),
                  _full((d, LANES)), _full((1, LANES)),
                  pl.BlockSpec((None, d, ffe), lambda i, e: (e, 0, 0)),
                  pl.BlockSpec((None, d, ffe), lambda i, e: (e, 0, 0)),
                  pl.BlockSpec((None, ffe, d), lambda i, e: (e, 0, 0)),
                  _full((1, d)), _full((1, d))],
        out_specs=pl.BlockSpec((tm, d), lambda i, e: (i, 0)),
        out_shape=jax.ShapeDtypeStruct((t, d), F32),
        scratch_shapes=[pltpu.VMEM((tm, d), BF16), pltpu.VMEM((tm, LANES), F32),
                        pltpu.VMEM((tm, d), F32)],
        compiler_params=_params(("parallel", "arbitrary")),
        name="moe",
    )(x, mods, rw, rb, w1.astype(BF16), w3.astype(BF16), w2.astype(BF16),
      ln_g.reshape(1, d), ln_b.reshape(1, d))


def _pool_matrices(tm, seq_len):
    t = np.arange(tm)
    pos = t % seq_len
    mats, inv = [], []
    for win in POOL_WINDOWS:
        lo = np.clip(pos - win // 2, 0, seq_len)
        hi = np.clip(pos + win // 2, 0, seq_len)
        base = t - pos
        j = t[None, :]
        mats.append(((j >= (base + lo)[:, None]) & (j < (base + hi)[:, None])).astype(np.float32))
        inv.append((hi - lo).astype(np.float32))
    return np.stack(mats), np.stack(inv)


def _pool_body(x_ref, mod_ref, pm_ref, cnt_ref, pw_ref, ps_ref, lng_ref, lnb_ref, o_ref, *, alpha):
    x = x_ref[...]
    sh, sc, gt = mod_ref[0:1, :], mod_ref[1:2, :], mod_ref[2:3, :]
    h = x * (1.0 + sc) + sh
    n_g = pm_ref.shape[0]
    gw = x.shape[1] // n_g
    outs = []
    for g in range(n_g):
        hg = h[:, g * gw:(g + 1) * gw]
        total = _dot_rhs2(pm_ref[g], hg)
        pooled = total / cnt_ref[g] - hg
        outs.append(_dot(pooled, pw_ref[g]))
    y = jnp.concatenate(outs, axis=-1) * ps_ref[...]
    o_ref[...] = _layer_norm(alpha * x + gt * y, lng_ref[...], lnb_ref[...])


def _pool_layer(geom, x, mods, pool_w, pool_scale, ln_g, ln_b, alpha, tm=256):
    t, d = x.shape
    n_g = len(POOL_WINDOWS)
    m_lat, c_lat = _pool_matrices(tm, GRID_W)
    m_ctx, c_ctx = _pool_matrices(tm, geom.ctx)
    pm = jnp.asarray(np.stack([m_lat, m_ctx]), BF16)
    cnt = jnp.asarray(np.stack([c_lat, c_ctx])[..., None], F32)
    n_lat_tiles = geom.n_lat // tm
    kind = lambda i: jnp.where(i < n_lat_tiles, 0, 1)
    return pl.pallas_call(
        functools.partial(_pool_body, alpha=alpha),
        grid=(t // tm,),
        in_specs=[pl.BlockSpec((tm, d), lambda i: (i, 0)),
                  pl.BlockSpec((None, 6, d), lambda i: (geom.mod_row(i, tm), 0, 0)),
                  pl.BlockSpec((None, n_g, tm, tm), lambda i: (kind(i), 0, 0, 0)),
                  pl.BlockSpec((None, n_g, tm, 1), lambda i: (kind(i), 0, 0, 0)),
                  _full((n_g, d // n_g, d // n_g)), _full((1, d)), _full((1, d)), _full((1, d))],
        out_specs=pl.BlockSpec((tm, d), lambda i: (i, 0)),
        out_shape=jax.ShapeDtypeStruct((t, d), F32),
        compiler_params=_params(("parallel",)),
        name="pool",
    )(x, mods, pm, cnt, pool_w.astype(BF16), pool_scale.reshape(1, d),
      ln_g.reshape(1, d), ln_b.reshape(1, d))


def _scan_tile_index(geom, i, tm):
    n_lat_tiles = geom.n_lat // tm
    tpb = geom.seq // tm
    is_lat = i < n_lat_tiles
    return jnp.where(is_lat, i // tpb, i - n_lat_tiles), jnp.where(is_lat, geom.ctx // tm + i % tpb, 0)


def _scan_step_chunk(geom, direction, s):
    if direction == 0:
        return s
    nc_ctx = geom.ctx // CHUNK
    nc = (geom.ctx + geom.seq) // CHUNK
    return jnp.where(s < nc_ctx, nc_ctx - 1 - s, nc - 1 + nc_ctx - s)


def _order_masks(direction):
    n = 2 * CHUNK
    ri = lax.broadcasted_iota(jnp.int32, (n, n), 0)
    ci = lax.broadcasted_iota(jnp.int32, (n, n), 1)
    rt, ct = ri & (CHUNK - 1), ci & (CHUNK - 1)
    ahead = rt - ct if direction == 0 else ct - rt
    return ahead > 0, ahead >= 0, ri < CHUNK, ci < CHUNK


INV_BASE = 8


def _inverse_masks(n):
    ri = lax.broadcasted_iota(jnp.int32, (n, n), 0)
    ci = lax.broadcasted_iota(jnp.int32, (n, n), 1)
    eye = (ri == ci).astype(F32)
    same = lambda size: (ri // size) == (ci // size)
    base = same(INV_BASE)
    levels = []
    size = INV_BASE
    while size < CHUNK:
        levels.append(same(2 * size) & ~same(size))
        size *= 2
    return eye, base, levels


def _unit_tri_inverse(mats, masks):
    eye, base, levels = masks
    ps = [jnp.where(base, -a, 0.0) for a in mats]
    ts = [eye + p for p in ps]
    for _ in range(int(math.log2(INV_BASE)) - 1):
        ps = [_dot(p, p) for p in ps]
        ts = [t + _dot(t, p) for t, p in zip(ts, ps)]
    for off in levels:
        cs = [_dot(jnp.where(off, a, 0.0), t) for a, t in zip(mats, ts)]
        ts = [t - _dot(t, c) for t, c in zip(ts, cs)]
    return ts


def _cum_matrices():
    i = np.arange(CHUNK)
    fwd = (i[None, :] <= i[:, None]).astype(np.float32)
    return jnp.asarray(np.stack([fwd, fwd.T]), BF16)


def _rk_feat_body(x_ref, xp_ref, xn_ref, mod_ref, mu_ref, wrkv_ref, w1_ref, w2_ref, w0_ref,
                  a1_ref, a2_ref, a0_ref, g1_ref, g2_ref, kk_ref, ka_ref, rk_ref, e_ref, et_ref,
                  r_out, kkn_out, v_out, gate_out, bonus_out, lw_out, b_out, kd_out,
                  *, n_lat_tiles, tiles_per_batch, seq):
    i = pl.program_id(0)
    tm, d = x_ref.shape
    q = d // 4
    sh, sc = mod_ref[0:1, :], mod_ref[1:2, :]
    h = x_ref[...] * (1.0 + sc) + sh
    hp = xp_ref[...] * (1.0 + sc) + sh
    hn = xn_ref[...] * (1.0 + sc) + sh
    is_ctx = i >= n_lat_tiles
    t = lax.broadcasted_iota(jnp.int32, (tm, 1), 0)
    col = t & (GRID_W - 1)
    pos = (i % tiles_per_batch) * tm + t
    to_end = (tm - 1) - t

    def prev_tok(z):
        return pltpu.roll(z, 1, 0)

    def next_tok(z):
        return pltpu.roll(z, tm - 1, 0)

    h0, h1, h2, h3 = (h[:, k * q:(k + 1) * q] for k in range(4))
    up = jnp.concatenate([hp[:, 2 * q:3 * q], h2[:tm - GRID_W]], axis=0)
    down = jnp.concatenate([h3[GRID_W:], hn[:, 3 * q:]], axis=0)
    edge0 = jnp.where(is_ctx, t, col)
    edge1 = jnp.where(is_ctx, t, (GRID_W - 1) - col)
    edge2 = jnp.where(is_ctx, to_end, jnp.maximum(pos - (GRID_W - 1), 0))
    edge3 = jnp.where(is_ctx, to_end, jnp.maximum((seq - GRID_W) - pos, 0))
    s0 = jnp.where(edge0 == 0, 0.0, prev_tok(h0))
    s1 = jnp.where(edge1 == 0, 0.0, jnp.where(is_ctx, prev_tok(h1), next_tok(h1)))
    s2 = jnp.where(edge2 == 0, 0.0, jnp.where(is_ctx, next_tok(h2), up))
    s3 = jnp.where(edge3 == 0, 0.0, jnp.where(is_ctx, next_tok(h3), down))
    xx = jnp.concatenate([s0, s1, s2, s3], axis=-1) - h
    xr, xw, xk, xv, xa, xg = (h + xx * mu_ref[m:m + 1, :] for m in range(6))

    r = _dot(xr, wrkv_ref[0])
    k = _dot(xk, wrkv_ref[1])
    v = _dot(xv, wrkv_ref[2])
    dl = jnp.tanh(_dot(xw, w1_ref[...]))
    al = _dot(xa, a1_ref[...])
    kkr = k * kk_ref[...]
    ss = _segsum(kkr * kkr, e_ref, et_ref)
    kkn = kkr / jnp.maximum(jnp.sqrt(ss), 1e-12)
    kd_sum = jnp.zeros_like(k)
    for z in range(2):
        lw = -RK_DECAY_SCALE * _sigmoid(w0_ref[z:z + 1, :] + _dot(dl, w2_ref[z]))
        a = _sigmoid(a0_ref[z:z + 1, :] + _dot(al, a2_ref[z]))
        kd = k * (1.0 + (a - 1.0) * ka_ref[...])
        lw_out[z] = lw
        b_out[z] = kkn * a
        kd_out[z] = kd
        kd_sum = kd_sum + kd
    coef = _segsum(r * kd_sum * rk_ref[...], e_ref, et_ref)
    r_out[...] = r
    kkn_out[...] = kkn
    v_out[...] = v
    bonus_out[...] = coef * v
    gate_out[...] = _dot(_sigmoid(_dot(xg, g1_ref[...])), g2_ref[...])


def _rk_scan_chunk(direction, b, r_ref, kk_ref, v_ref, lw_ref, b_ref, kd_ref, cum_ref, y_ref, s_scr):
    c = CHUNK
    lw = lw_ref[b]
    cs = _dot_rhs2(cum_ref[direction], lw)
    cs_last = cs[c - 1:c, :] if direction == 0 else cs[0:1, :]
    eg = jnp.exp(cs)
    en = jnp.exp(-cs)
    el = jnp.exp(cs_last - cs)
    rt = r_ref[b] * eg
    kkt = kk_ref[b] * jnp.exp(cs - lw)
    bb, kd, v = b_ref[b], kd_ref[b], v_ref[b]
    bt, kt = bb * en, kd * en
    bh, kh = bb * el, kd * el
    gl = jnp.exp(cs_last)

    strict, incl, top, left = _order_masks(direction)
    inv_masks = _inverse_masks(2 * c)
    lane = lax.broadcasted_iota(jnp.int32, (1, LANES), 1)
    m0, m1 = lane < c, lane >= c
    diag_blk = top == left
    zero = jnp.float32(0.0)
    tl, br = top & left, (~top) & (~left)
    tr, bl = top & (~left), (~top) & left

    def pick(m_first, m_second, first, second):
        return jnp.where(m_first, first, jnp.where(m_second, second, zero))

    pairs = range(r_ref.shape[-1] // LANES)
    sl = [slice(p * LANES, (p + 1) * LANES) for p in pairs]
    r0 = [_dot_nt(jnp.concatenate([jnp.where(m0, kkt[:, s], zero), jnp.where(m0, rt[:, s], zero)], 0),
                  jnp.concatenate([bt[:, s], kt[:, s]], 0)) for s in sl]
    r1 = [_dot_nt(jnp.concatenate([jnp.where(m1, rt[:, s], zero), jnp.where(m1, kkt[:, s], zero)], 0),
                  jnp.concatenate([kt[:, s], bt[:, s]], 0)) for s in sl]
    s_old = [s_scr[direction, b, p] for p in pairs]
    proj = [_dot_nt(jnp.concatenate([kkt[:, s], rt[:, s]], 0), st) for s, st in zip(sl, s_old)]
    a_bk = [pick(tl & strict, br & strict, x0, x1) for x0, x1 in zip(r0, r1)]
    a_kk = [pick(tr & strict, bl & strict, x0, x1) for x0, x1 in zip(r0, r1)]
    a_rbk = [jnp.concatenate([pick(bl & incl, tr & incl, x0, x1), pick(br & incl, tl & incl, x0, x1)], 1)
             for x0, x1 in zip(r0, r1)]
    v_sw = [jnp.concatenate([jnp.where(m1, v[:, s], zero), jnp.where(m0, v[:, s], zero)], 0) for s in sl]
    akkv = [_dot(a, x) for a, x in zip(a_kk, v_sw)]
    t_inv = _unit_tri_inverse(a_bk, inv_masks)
    ks_sm = [jnp.concatenate([jnp.where(m0, pj[:c], zero), jnp.where(m1, pj[:c], zero)], 0) for pj in proj]
    u_sm = [-_dot(t, k_ + a) for t, k_, a in zip(t_inv, ks_sm, akkv)]
    y_sm = [_dot(a, jnp.concatenate([u, x], 0)) for a, u, x in zip(a_rbk, u_sm, v_sw)]
    upd = [_dot_tn(jnp.concatenate([u[:c] + u[c:], v[:, s]], 0), jnp.concatenate([bh[:, s], kh[:, s]], 0))
           for u, s in zip(u_sm, sl)]
    for p in pairs:
        y_ref[b, :, sl[p]] = proj[p][c:] + y_sm[p][:c] + y_sm[p][c:]
        s_scr[direction, b, p] = s_old[p] * gl[:, sl[p]] + jnp.where(diag_blk, upd[p], zero)


def _rk_scan_body(rf, rb, kkf, kkb, vf, vb, lwf, lwb, bf, bb, kdf, kdb, cum_ref, yf, yb, s_scr):
    @pl.when(pl.program_id(0) == 0)
    def _():
        s_scr[...] = jnp.zeros(s_scr.shape, F32)

    def per_batch(b, carry):
        _rk_scan_chunk(0, b, rf, kkf, vf, lwf, bf, kdf, cum_ref, yf, s_scr)
        _rk_scan_chunk(1, b, rb, kkb, vb, lwb, bb, kdb, cum_ref, yb, s_scr)
        return carry

    lax.fori_loop(0, rf.shape[0], per_batch, 0)


def _rk_out_body(x_ref, yf_ref, yb_ref, bonus_ref, gate_ref, mod_ref, lxg_ref, lxb_ref, wo_ref, e_ref,
                 et_ref, lng_ref, lnb_ref, o_ref, *, alpha, head):
    x = x_ref[...]
    y = yf_ref[...] + yb_ref[...]
    mean = _segsum(y, e_ref, et_ref) * (1.0 / head)
    cen = y - mean
    var = _segsum(cen * cen, e_ref, et_ref) * (1.0 / head)
    yn = cen * lax.rsqrt(var + RK_GN_EPS) * lxg_ref[...] + lxb_ref[...]
    out = _dot((yn + bonus_ref[...]) * gate_ref[...], wo_ref[...])
    gt = mod_ref[2:3, :]
    o_ref[...] = _layer_norm(alpha * x + gt * out, lng_ref[...], lnb_ref[...])


def _rwkv_layer(geom, x, mods, mu, w_rkv, w0, w1, w2, a0, a1, a2, g1, g2, k_k, k_a, r_k,
                lnx_g, lnx_b, w_o, ln_g, ln_b, alpha, tm=256):
    t, d = x.shape
    head = r_k.shape[-1]
    assert 2 * head == LANES and geom.ctx == tm and geom.seq % tm == 0
    lora_d, lora_a, lora_g = w1.shape[-1], a1.shape[-1], g1.shape[-1]
    e, et = _head_indicator(d, head)

    def cat_dirs(w):
        return jnp.concatenate([w[0], w[1]], axis=-1).astype(BF16)

    def pad_dirs(w):
        z = jnp.zeros_like(w[0])
        return jnp.stack([jnp.concatenate([w[0], z], 0), jnp.concatenate([z, w[1]], 0)]).astype(BF16)

    nb = t // GRID_W
    nt = geom.ctx + geom.seq
    row = lambda i: (i, 0)
    tile = pl.BlockSpec((tm, d), row)
    scan_tile = pl.BlockSpec((None, tm, d), lambda i: (*_scan_tile_index(geom, i, tm), 0))
    scan_tile2 = pl.BlockSpec((2, None, tm, d), lambda i: (0, *_scan_tile_index(geom, i, tm), 0))
    mod_spec = pl.BlockSpec((None, 6, d), lambda i: (geom.mod_row(i, tm), 0, 0))
    per = tm // GRID_W
    feat = pl.pallas_call(
        functools.partial(_rk_feat_body, n_lat_tiles=geom.n_lat // tm,
                          tiles_per_batch=geom.seq // tm, seq=geom.seq),
        grid=(t // tm,),
        in_specs=[tile,
                  pl.BlockSpec((GRID_W, d), lambda i: (jnp.maximum(i * per - 1, 0), 0)),
                  pl.BlockSpec((GRID_W, d), lambda i: (jnp.minimum(i * per + per, nb - 1), 0)),
                  mod_spec, _full((6, d)), _resident((3, d, d)),
                  _full((d, 2 * lora_d)), _full((2, 2 * lora_d, d)), _full((2, d)),
                  _full((d, 2 * lora_a)), _full((2, 2 * lora_a, d)), _full((2, d)),
                  _full((d, lora_g)), _full((lora_g, d)),
                  _full((1, d)), _full((1, d)), _full((1, d)), _full((d, LANES)), _full((LANES, d))],
        out_specs=[scan_tile, scan_tile, scan_tile, tile, tile, scan_tile2, scan_tile2, scan_tile2],
        out_shape=([jax.ShapeDtypeStruct((geom.b, nt, d), F32)] * 3 + [jax.ShapeDtypeStruct((t, d), F32)] * 2
                   + [jax.ShapeDtypeStruct((2, geom.b, nt, d), F32)] * 3),
        compiler_params=_params(("parallel",)),
        name="rwkv_feat",
    )
    r, kkn, v, gate, bonus, lw, bb, kd = feat(
        x, x, x, mods, mu, w_rkv.astype(BF16), cat_dirs(w1), pad_dirs(w2), w0,
        cat_dirs(a1), pad_dirs(a2), a0, g1.astype(BF16), g2.astype(BF16),
        k_k.reshape(1, d), k_a.reshape(1, d), r_k.reshape(1, d), e, et)

    n_steps = nt // CHUNK
    shared = [pl.BlockSpec((geom.b, CHUNK, d), lambda s, z=z: (0, _scan_step_chunk(geom, z, s), 0))
              for z in range(2)]
    per_dir = [pl.BlockSpec((None, geom.b, CHUNK, d), lambda s, z=z: (z, 0, _scan_step_chunk(geom, z, s), 0))
               for z in range(2)]
    yf, yb = pl.pallas_call(
        _rk_scan_body,
        grid=(n_steps,),
        in_specs=shared * 3 + per_dir * 3 + [_full((2, CHUNK, CHUNK))],
        out_specs=shared,
        out_shape=[jax.ShapeDtypeStruct((geom.b, nt, d), F32)] * 2,
        scratch_shapes=[pltpu.VMEM((2, geom.b, d // LANES, LANES, LANES), F32)],
        compiler_params=_params(("arbitrary",)),
        name="rwkv_scan",
    )(r, r, kkn, kkn, v, v, lw, lw, bb, bb, kd, kd, _cum_matrices())

    return pl.pallas_call(
        functools.partial(_rk_out_body, alpha=alpha, head=head),
        grid=(t // tm,),
        in_specs=[tile, scan_tile, scan_tile, tile, tile, mod_spec, _full((1, d)), _full((1, d)),
                  _resident((d, d)), _full((d, LANES)), _full((LANES, d)), _full((1, d)), _full((1, d))],
        out_specs=tile,
        out_shape=jax.ShapeDtypeStruct((t, d), F32),
        compiler_params=_params(("parallel",)),
        name="rwkv_out",
    )(x, yf, yb, bonus, gate, mods, lnx_g.reshape(1, d), lnx_b.reshape(1, d), w_o.astype(BF16), e, et,
      ln_g.reshape(1, d), ln_b.reshape(1, d))


HALO = 8


def _gdn_feat_body(x_ref, xp_ref, xn_ref, mod_ref, wq_ref, wz_ref, wab_ref, cw_ref, alog_ref, dtb_ref,
                   q_out, k_out, v_out, z_out, gb_out, proj_scr,
                   *, n_lat_tiles, tiles_per_batch, head, n_taps):
    i = pl.program_id(0)
    tm, d = x_ref.shape
    sh, sc = mod_ref[0:1, :], mod_ref[1:2, :]
    h = x_ref[...] * (1.0 + sc) + sh
    is_ctx = i >= n_lat_tiles
    j = i % tiles_per_batch
    has_prev = jnp.logical_and(jnp.logical_not(is_ctx), j > 0)
    has_next = jnp.logical_and(jnp.logical_not(is_ctx), j < tiles_per_batch - 1)
    hp = jnp.where(has_prev, xp_ref[...] * (1.0 + sc) + sh, 0.0)
    hn = jnp.where(has_next, xn_ref[...] * (1.0 + sc) + sh, 0.0)
    hb = h.astype(BF16)
    proj_scr[0:HALO, :] = _dot(hp, wq_ref[...])
    proj_scr[HALO:HALO + tm, :] = jnp.dot(hb, wq_ref[...], preferred_element_type=F32)
    proj_scr[HALO + tm:, :] = _dot(hn, wq_ref[...])
    left = n_taps // 2
    conv = jnp.zeros((tm, 3 * d), F32)
    for tap in range(n_taps):
        conv = conv + proj_scr[pl.ds(HALO + tap - left, tm), :] * cw_ref[tap:tap + 1, :]
    qkv = _silu(conv)
    n_heads = d // head
    for hd in range(n_heads):
        qs = qkv[:, hd * head:(hd + 1) * head]
        ks = qkv[:, d + hd * head:d + (hd + 1) * head]
        q_out[:, hd * head:(hd + 1) * head] = (
            qs * lax.rsqrt(jnp.sum(qs * qs, axis=-1, keepdims=True) + GDN_L2_EPS) * head ** -0.5)
        k_out[:, hd * head:(hd + 1) * head] = (
            ks * lax.rsqrt(jnp.sum(ks * ks, axis=-1, keepdims=True) + GDN_L2_EPS))
    v_out[...] = qkv[:, 2 * d:]
    z_out[...] = jnp.dot(hb, wz_ref[...], preferred_element_type=F32)
    ab = _dot3(h, wab_ref[...])
    lane = lax.broadcasted_iota(jnp.int32, ab.shape, 1)
    g = -jnp.exp(alog_ref[...]) * _softplus(ab + dtb_ref[...])
    gb_out[...] = jnp.where(lane < 2 * n_heads, g, _sigmoid(ab))


def _gdn_scan_chunk(direction, b, q_ref, k_ref, v_ref, gb_ref, y_ref, s_scr, n_heads):
    c = CHUNK
    head = q_ref.shape[-1] // n_heads
    strict, incl, top, left = _order_masks(direction)
    inv_masks = _inverse_masks(2 * c)
    same = top == left
    cum_bd = jnp.where(same & incl, 1.0, 0.0).astype(BF16)
    nxt_bd = jnp.where(same & strict, 1.0, 0.0)
    zero = jnp.float32(0.0)
    gb = gb_ref[b]
    row = lax.broadcasted_iota(jnp.int32, (2 * c, 1), 0)
    last_row = c - 1 if direction == 0 else 0

    pairs = range(n_heads // 2)
    hs = [(2 * p, 2 * p + 1) for p in pairs]
    col = lambda idx: gb[:, idx:idx + 1]
    g_col = [jnp.concatenate([col(direction * n_heads + h0), col(direction * n_heads + h1)], 0)
             for h0, h1 in hs]
    beta = [jnp.concatenate([col((2 + direction) * n_heads + h0), col((2 + direction) * n_heads + h1)], 0)
            for h0, h1 in hs]

    def stack(ref):
        return [jnp.concatenate([ref[b, :, h0 * head:(h0 + 1) * head],
                                 ref[b, :, h1 * head:(h1 + 1) * head]], 0) for h0, h1 in hs]

    q, k, v = stack(q_ref), stack(k_ref), stack(v_ref)
    cs = [_dot_rhs2(cum_bd, jnp.broadcast_to(g, (2 * c, LANES)))[:, 0:1] for g in g_col]
    dlog = [_dot_rhs2(cum_bd, g * nxt_bd) for g in g_col]
    decay = [jnp.exp(jnp.where(same & incl, x, -jnp.inf)) for x in dlog]
    cs_l0 = [jnp.sum(jnp.where(row == last_row, x, zero), axis=0, keepdims=True) for x in cs]
    cs_l1 = [jnp.sum(jnp.where(row == last_row + c, x, zero), axis=0, keepdims=True) for x in cs]
    cs_end = [jnp.where(row < c, x0, x1) for x0, x1 in zip(cs_l0, cs_l1)]
    kb = [x * bt for x, bt in zip(k, beta)]
    vb = [x * bt for x, bt in zip(v, beta)]
    kkt = [_dot_nt(x, y) for x, y in zip(kb, k)]
    qkt = [_dot_nt(x, y) for x, y in zip(q, k)]
    lower = [jnp.where(same & strict, x * dc, zero) for x, dc in zip(kkt, decay)]
    a_qk = [jnp.where(same & incl, x * dc, zero) for x, dc in zip(qkt, decay)]
    t_inv = _unit_tri_inverse(lower, inv_masks)
    eg = [jnp.exp(x) for x in cs]
    uw = [_dot(t, jnp.concatenate([x, y * e], axis=1)) for t, x, y, e in zip(t_inv, vb, kb, eg)]
    s_old = [(s_scr[direction, b, h0], s_scr[direction, b, h1]) for h0, h1 in hs]
    ws = [jnp.concatenate([_dot(x[:c, head:], s0), _dot(x[c:, head:], s1)], 0)
          for x, (s0, s1) in zip(uw, s_old)]
    qs = [jnp.concatenate([_dot((x * e)[:c], s0), _dot((x * e)[c:], s1)], 0)
          for x, e, (s0, s1) in zip(q, eg, s_old)]
    v_new = [x[:, :head] - y for x, y in zip(uw, ws)]
    o = [x + _dot(a, y) for x, a, y in zip(qs, a_qk, v_new)]
    k_dec = [x * jnp.exp(ce - y) for x, ce, y in zip(k, cs_end, cs)]
    for p, (h0, h1) in enumerate(hs):
        y_ref[b, :, h0 * head:(h0 + 1) * head] = o[p][:c]
        y_ref[b, :, h1 * head:(h1 + 1) * head] = o[p][c:]
        s_scr[direction, b, h0] = s_old[p][0] * jnp.exp(cs_l0[p]) + _dot_tn(k_dec[p][:c], v_new[p][:c])
        s_scr[direction, b, h1] = s_old[p][1] * jnp.exp(cs_l1[p]) + _dot_tn(k_dec[p][c:], v_new[p][c:])


def _gdn_scan_body(qf, qb, kf, kb, vf, vb, gf, gb, yf, yb, s_scr, *, n_heads):
    @pl.when(pl.program_id(0) == 0)
    def _():
        s_scr[...] = jnp.zeros(s_scr.shape, F32)

    def per_batch(b, carry):
        _gdn_scan_chunk(0, b, qf, kf, vf, gf, yf, s_scr, n_heads)
        _gdn_scan_chunk(1, b, qb, kb, vb, gb, yb, s_scr, n_heads)
        return carry

    lax.fori_loop(0, qf.shape[0], per_batch, 0)


def _gdn_out_body(x_ref, of_ref, ob_ref, z_ref, mod_ref, nw_ref, wo_ref, lng_ref, lnb_ref, out_ref,
                  *, alpha, head):
    x = x_ref[...]
    o = of_ref[...] + ob_ref[...]
    z = z_ref[...]
    d = x.shape[1]
    parts = []
    for hd in range(d // head):
        oh = o[:, hd * head:(hd + 1) * head]
        on = oh * lax.rsqrt(jnp.mean(oh * oh, axis=-1, keepdims=True) + GDN_NORM_EPS) * nw_ref[...]
        parts.append(on * _silu(z[:, hd * head:(hd + 1) * head]))
    y = _dot(jnp.concatenate(parts, axis=-1), wo_ref[...])
    gt = mod_ref[2:3, :]
    out_ref[...] = _layer_norm(alpha * x + gt * y, lng_ref[...], lnb_ref[...])


def _gdn_layer(geom, x, mods, w_in, conv_w, a_log, dt_bias, norm_w, w_o, ln_g, ln_b, alpha, tm=256):
    t, d = x.shape
    n_heads = a_log.shape[-1]
    head = d // n_heads
    n_taps = conv_w.shape[0]
    assert head == LANES and geom.ctx == tm and geom.seq % tm == 0 and n_heads % 2 == 0
    assert 4 * n_heads <= LANES
    w_qkv = w_in[:, :3 * d].astype(BF16)
    w_z = w_in[:, 3 * d:4 * d].astype(BF16)
    w_ab = jnp.zeros((d, LANES), F32).at[:, :4 * n_heads].set(w_in[:, 4 * d:])
    alog = jnp.zeros((1, LANES), F32).at[0, :2 * n_heads].set(a_log.reshape(-1))
    dtb = jnp.zeros((1, LANES), F32).at[0, :2 * n_heads].set(dt_bias.reshape(-1))

    nb = t // HALO
    nt = geom.ctx + geom.seq
    per = tm // HALO
    tile = pl.BlockSpec((tm, d), lambda i: (i, 0))
    scan_tile = pl.BlockSpec((None, tm, d), lambda i: (*_scan_tile_index(geom, i, tm), 0))
    small = pl.BlockSpec((None, tm, LANES), lambda i: (*_scan_tile_index(geom, i, tm), 0))
    mod_spec = pl.BlockSpec((None, 6, d), lambda i: (geom.mod_row(i, tm), 0, 0))
    q, k, v, z, gb = pl.pallas_call(
        functools.partial(_gdn_feat_body, n_lat_tiles=geom.n_lat // tm,
                          tiles_per_batch=geom.seq // tm, head=head, n_taps=n_taps),
        grid=(t // tm,),
        in_specs=[tile,
                  pl.BlockSpec((HALO, d), lambda i: (jnp.maximum(i * per - 1, 0), 0)),
                  pl.BlockSpec((HALO, d), lambda i: (jnp.minimum(i * per + per, nb - 1), 0)),
                  mod_spec, _resident((d, 3 * d)), _resident((d, d)), _full((d, LANES)),
                  _full((n_taps, 3 * d)), _full((1, LANES)), _full((1, LANES))],
        out_specs=[scan_tile, scan_tile, scan_tile, tile, small],
        out_shape=([jax.ShapeDtypeStruct((geom.b, nt, d), F32)] * 3 + [jax.ShapeDtypeStruct((t, d), F32)]
                   + [jax.ShapeDtypeStruct((geom.b, nt, LANES), F32)]),
        scratch_shapes=[pltpu.VMEM((tm + 2 * HALO, 3 * d), F32)],
        compiler_params=_params(("parallel",)),
        name="gdn_feat",
    )(x, x, x, mods, w_qkv, w_z, w_ab, conv_w, alog, dtb)

    n_steps = nt // CHUNK
    shared = [pl.BlockSpec((geom.b, CHUNK, d), lambda s, zz=zz: (0, _scan_step_chunk(geom, zz, s), 0))
              for zz in range(2)]
    shared_small = [pl.BlockSpec((geom.b, CHUNK, LANES), lambda s, zz=zz: (0, _scan_step_chunk(geom, zz, s), 0))
                    for zz in range(2)]
    of, ob = pl.pallas_call(
        functools.partial(_gdn_scan_body, n_heads=n_heads),
        grid=(n_steps,),
        in_specs=shared * 3 + shared_small,
        out_specs=shared,
        out_shape=[jax.ShapeDtypeStruct((geom.b, nt, d), F32)] * 2,
        scratch_shapes=[pltpu.VMEM((2, geom.b, n_heads, head, head), F32)],
        compiler_params=_params(("arbitrary",)),
        name="gdn_scan",
    )(q, q, k, k, v, v, gb, gb)

    return pl.pallas_call(
        functools.partial(_gdn_out_body, alpha=alpha, head=head),
        grid=(t // tm,),
        in_specs=[tile, scan_tile, scan_tile, tile, mod_spec, _full((1, head)), _resident((d, d)),
                  _full((1, d)), _full((1, d))],
        out_specs=tile,
        out_shape=jax.ShapeDtypeStruct((t, d), F32),
        compiler_params=_params(("parallel",)),
        name="gdn_out",
    )(x, of, ob, z, mods, norm_w.reshape(1, head), w_o.astype(BF16), ln_g.reshape(1, d), ln_b.reshape(1, d))


def kernel(x, c, ctx, c_ctx, ada_w, ada_b, ln_g, ln_b, rk_mu, rk_w_rkv, rk_w0, rk_w1, rk_w2, rk_a0, rk_a1, rk_a2, rk_g1, rk_g2, rk_k_k, rk_k_a, rk_r_k, rk_lnx_g, rk_lnx_b, rk_w_o, pool_w, pool_scale, gdn_w_in, gdn_conv_w, gdn_a_log, gdn_dt_bias, gdn_norm_w, gdn_w_o, ffn_w1, ffn_w3, ffn_w2, moe_router_w, moe_router_b, moe_w1, moe_w3, moe_w2):
    batch, seq, d = x.shape
    ctx_len = ctx.shape[1]
    depth = ada_w.shape[0]
    geom = _Geom(batch, seq, ctx_len, d)
    alpha = (2 * depth) ** 0.25
    n_mixers = 3

    rows = -(-(batch + 1) // 8) * 8
    cvec = jnp.zeros((rows, d), F32).at[:batch].set(c).at[batch].set(c_ctx)
    table = _ada_table(cvec, ada_w, ada_b)
    mods_all = table[:, :batch + 1].reshape(depth, batch + 1, 6, d)

    xs = jnp.concatenate([x.reshape(batch * seq, d), ctx.reshape(batch * ctx_len, d)], axis=0)
    for i in range(depth):
        mods = mods_all[i]
        kind, j = i % n_mixers, i // n_mixers
        if kind == 0:
            xs = _rwkv_layer(geom, xs, mods, rk_mu[j], rk_w_rkv[j], rk_w0[j], rk_w1[j], rk_w2[j],
                             rk_a0[j], rk_a1[j], rk_a2[j], rk_g1[j], rk_g2[j], rk_k_k[j], rk_k_a[j],
                             rk_r_k[j], rk_lnx_g[j], rk_lnx_b[j], rk_w_o[j], ln_g[i, 0], ln_b[i, 0], alpha)
        elif kind == 1:
            xs = _pool_layer(geom, xs, mods, pool_w[j], pool_scale[j], ln_g[i, 0], ln_b[i, 0], alpha)
        else:
            xs = _gdn_layer(geom, xs, mods, gdn_w_in[j], gdn_conv_w[j], gdn_a_log[j], gdn_dt_bias[j],
                            gdn_norm_w[j], gdn_w_o[j], ln_g[i, 0], ln_b[i, 0], alpha)
        e = i // 2
        if i % 2 == 0:
            xs = _ffn_layer(geom, xs, mods, ffn_w1[e], ffn_w3[e], ffn_w2[e], ln_g[i, 1], ln_b[i, 1], alpha)
        else:
            xs = _moe_layer(geom, xs, mods, moe_router_w[e], moe_router_b[e], moe_w1[e], moe_w3[e],
                            moe_w2[e], ln_g[i, 1], ln_b[i, 1], alpha)
    return xs[:batch * seq].reshape(batch, seq, d)
```

```python
import functools
import math

import jax
import jax.numpy as jnp
import numpy as np
from jax import lax
from jax.experimental import pallas as pl
from jax.experimental.pallas import tpu as pltpu

F32 = jnp.float32
BF16 = jnp.bfloat16

GRID_W = 64
CHUNK = 64
LANES = 128
POOL_WINDOWS = (2, 4, 8, 16)
LN_EPS = 1e-5
RK_DECAY_SCALE = math.exp(-0.5)
RK_GN_EPS = 64e-5
GDN_NORM_EPS = 1e-6
GDN_L2_EPS = 1e-6
TOP_K = 2
VMEM_LIMIT = 56 * 1024 * 1024


def _sigmoid(x):
    return 1.0 / (1.0 + jnp.exp(-x))


def _silu(x):
    return x * _sigmoid(x)


def _softplus(x):
    return jnp.maximum(x, 0.0) + jnp.log(1.0 + jnp.exp(-jnp.abs(x)))


def _dot(a, b):
    return jnp.dot(a.astype(BF16), b.astype(BF16), preferred_element_type=F32)


def _dot_nt(a, b):
    return lax.dot_general(a.astype(BF16), b.astype(BF16), (((1,), (1,)), ((), ())),
                           preferred_element_type=F32)


def _dot_tn(a, b):
    return lax.dot_general(a.astype(BF16), b.astype(BF16), (((0,), (0,)), ((), ())),
                           preferred_element_type=F32)


def _split(x):
    hi = x.astype(BF16)
    lo = (x - hi.astype(F32)).astype(BF16)
    return hi, lo


def _dot_lhs2(a, b_exact):
    hi, lo = _split(a)
    return (jnp.dot(hi, b_exact, preferred_element_type=F32)
            + jnp.dot(lo, b_exact, preferred_element_type=F32))


def _dot_rhs2(a_exact, b):
    hi, lo = _split(b)
    return (jnp.dot(a_exact, hi, preferred_element_type=F32)
            + jnp.dot(a_exact, lo, preferred_element_type=F32))


def _dot3(a, b):
    ah, al = _split(a)
    bh, bl = _split(b)
    return (jnp.dot(ah, bh, preferred_element_type=F32)
            + jnp.dot(ah, bl, preferred_element_type=F32)
            + jnp.dot(al, bh, preferred_element_type=F32))


def _segsum(x, e_ref, et_ref):
    s = _dot_lhs2(x, e_ref[...])
    return _dot_lhs2(s, et_ref[...])


def _layer_norm(v, g, b):
    mean = jnp.mean(v, axis=-1, keepdims=True)
    c = v - mean
    var = jnp.mean(c * c, axis=-1, keepdims=True)
    return c * lax.rsqrt(var + LN_EPS) * g + b


def _head_indicator(d_model, head):
    n_heads = d_model // head
    e = np.zeros((d_model, LANES), np.float32)
    e[np.arange(d_model), np.arange(d_model) // head] = 1.0
    assert n_heads <= LANES
    return jnp.asarray(e, BF16), jnp.asarray(e.T.copy(), BF16)


def _full(shape):
    n = len(shape)
    return pl.BlockSpec(shape, lambda *_: (0,) * n)


def _resident(shape):
    n = len(shape)
    return pl.BlockSpec(shape, lambda *_: (0,) * n, pipeline_mode=pl.Buffered(1))


def _params(sem):
    return pltpu.CompilerParams(dimension_semantics=sem, vmem_limit_bytes=VMEM_LIMIT)


class _Geom:
    def __init__(self, batch, seq, ctx_len, d_model):
        self.b, self.seq, self.ctx, self.d = batch, seq, ctx_len, d_model
        self.n_lat = batch * seq
        self.t = batch * (seq + ctx_len)

    def mod_row(self, i, tm):
        return jnp.where(i < self.n_lat // tm, i // (self.seq // tm), self.b)


def _ada_body(c_ref, w_ref, b_ref, o_ref):
    o_ref[...] = _dot3(_silu(c_ref[...]), w_ref[...]) + b_ref[...]


def _ada_table(cvec, ada_w, ada_b):
    depth, d, d6 = ada_w.shape
    rows = cvec.shape[0]
    return pl.pallas_call(
        _ada_body,
        grid=(depth, d6 // d),
        in_specs=[_full((rows, d)),
                  pl.BlockSpec((None, d, d), lambda l, j: (l, 0, j)),
                  pl.BlockSpec((None, 1, d), lambda l, j: (l, 0, j))],
        out_specs=pl.BlockSpec((None, rows, d), lambda l, j: (l, 0, j)),
        out_shape=jax.ShapeDtypeStruct((depth, rows, d6), F32),
        compiler_params=_params(("parallel", "parallel")),
        name="ada_table",
    )(cvec, ada_w, ada_b.reshape(depth, 1, d6))


def _ffn_body(x_ref, mod_ref, w1_ref, w3_ref, w2_ref, lng_ref, lnb_ref, o_ref, *, alpha, n_split):
    x = x_ref[...]
    sh, sc, gt = mod_ref[3:4, :], mod_ref[4:5, :], mod_ref[5:6, :]
    h = (x * (1.0 + sc) + sh).astype(BF16)
    ff = w1_ref.shape[1] // n_split
    acc = jnp.zeros(x.shape, F32)
    for s in range(n_split):
        a = jnp.dot(h, w1_ref[:, s * ff:(s + 1) * ff], preferred_element_type=F32)
        g = jnp.dot(h, w3_ref[:, s * ff:(s + 1) * ff], preferred_element_type=F32)
        acc = acc + jnp.dot((_silu(a) * g).astype(BF16), w2_ref[s * ff:(s + 1) * ff, :],
                            preferred_element_type=F32)
    o_ref[...] = _layer_norm(alpha * x + gt * acc, lng_ref[...], lnb_ref[...])


def _ffn_layer(geom, x, mods, w1, w3, w2, ln_g, ln_b, alpha, tm=256):
    t, d = x.shape
    ff = w1.shape[1]
    return pl.pallas_call(
        functools.partial(_ffn_body, alpha=alpha, n_split=2),
        grid=(t // tm,),
        in_specs=[pl.BlockSpec((tm, d), lambda i: (i, 0)),
                  pl.BlockSpec((None, 6, d), lambda i: (geom.mod_row(i, tm), 0, 0)),
                  _resident((d, ff)), _resident((d, ff)), _resident((ff, d)),
                  _full((1, d)), _full((1, d))],
        out_specs=pl.BlockSpec((tm, d), lambda i: (i, 0)),
        out_shape=jax.ShapeDtypeStruct((t, d), F32),
        compiler_params=_params(("parallel",)),
        name="ffn",
    )(x, mods, w1.astype(BF16), w3.astype(BF16), w2.astype(BF16),
      ln_g.reshape(1, d), ln_b.reshape(1, d))


MOE_GROUP = 512
MOE_ROWS = 128


def _moe_body(x_ref, mod_ref, rw_ref, rb_ref, tri_ref, w1_ref, w3_ref, w2_ref, lng_ref, lnb_ref, o_ref,
              h_scr, gate_scr, sel_scr, rank_scr, selt_scr, rankt_scr, acc_scr, *, alpha, n_experts):
    e = pl.program_id(1)
    tm = x_ref.shape[0]
    n_groups = tm // MOE_GROUP

    @pl.when(e == 0)
    def _():
        x = x_ref[...]
        sh, sc = mod_ref[3:4, :], mod_ref[4:5, :]
        h = x * (1.0 + sc) + sh
        h_scr[...] = h.astype(BF16)
        logits = _dot3(h, rw_ref[...]) + rb_ref[...]
        lane = lax.broadcasted_iota(jnp.int32, logits.shape, 1)
        neg = jnp.float32(-jnp.inf)
        logits = jnp.where(lane < n_experts, logits, neg)
        m1 = jnp.max(logits, axis=-1, keepdims=True)
        i1 = jnp.min(jnp.where(logits == m1, lane, LANES), axis=-1, keepdims=True)
        rest = jnp.where(lane == i1, neg, logits)
        m2 = jnp.max(rest, axis=-1, keepdims=True)
        i2 = jnp.min(jnp.where(rest == m2, lane, LANES), axis=-1, keepdims=True)
        e2 = jnp.exp(m2 - m1)
        p1 = 1.0 / (1.0 + e2)
        p2 = e2 / (1.0 + e2)
        gate_scr[...] = jnp.where(lane == i1, p1, 0.0) + jnp.where(lane == i2, p2, 0.0)
        sel = jnp.where(lane == i1, 1.0, 0.0) + jnp.where(lane == i2, 1.0, 0.0)
        sel_scr[...] = sel
        for s in range(n_groups):
            rows = slice(s * MOE_GROUP, (s + 1) * MOE_GROUP)
            rank = jnp.dot(tri_ref[...], sel[rows].astype(BF16), preferred_element_type=F32)
            rank_scr[rows, :] = rank
            rankt_scr[s] = rank.T
            selt_scr[s] = sel[rows].T
        acc_scr[...] = jnp.zeros(acc_scr.shape, F32)

    lane = lax.broadcasted_iota(jnp.int32, (MOE_GROUP, LANES), 1)
    column = lambda ref, rows: jnp.sum(jnp.where(lane == e, ref[rows, :], 0.0), axis=-1, keepdims=True)
    for s in range(n_groups):
        rows = slice(s * MOE_GROUP, (s + 1) * MOE_GROUP)
        gate_col = column(gate_scr, rows)
        sel_col = column(sel_scr, rows) > 0.5
        rank_col = column(rank_scr, rows)
        sel_row = selt_scr[s, pl.ds(e, 1), :] > 0.5
        rank_row = rankt_scr[s, pl.ds(e, 1), :]
        count = jnp.sum(jnp.where(sel_col, 1.0, 0.0)).astype(jnp.int32)
        h = h_scr[rows, :]

        def one_pass(j, carry):
            base = (j * MOE_ROWS).astype(F32)
            slot_col = lax.broadcasted_iota(jnp.int32, (MOE_ROWS, 1), 0).astype(F32) + base
            slot_row = lax.broadcasted_iota(jnp.int32, (1, MOE_ROWS), 1).astype(F32) + base
            take = jnp.where((rank_row == slot_col) & sel_row, 1.0, 0.0).astype(BF16)
            put = jnp.where((rank_col == slot_row) & sel_col, 1.0, 0.0).astype(BF16)
            xg = jnp.dot(take, h, preferred_element_type=F32).astype(BF16)
            a = jnp.dot(xg, w1_ref[...], preferred_element_type=F32)
            g = jnp.dot(xg, w3_ref[...], preferred_element_type=F32)
            y = jnp.dot((_silu(a) * g).astype(BF16), w2_ref[...], preferred_element_type=F32)
            acc_scr[rows, :] += gate_col * _dot_rhs2(put, y)
            return carry

        lax.fori_loop(0, (count + MOE_ROWS - 1) // MOE_ROWS, one_pass, 0)

    @pl.when(e == n_experts - 1)
    def _():
        x = x_ref[...]
        gt = mod_ref[5:6, :]
        o_ref[...] = _layer_norm(alpha * x + gt * acc_scr[...], lng_ref[...], lnb_ref[...])


def _moe_layer(geom, x, mods, router_w, router_b, w1, w3, w2, ln_g, ln_b, alpha, tm=1024):
    t, d = x.shape
    n_e, _, ffe = w1.shape
    assert tm % MOE_GROUP == 0 and n_e <= LANES
    rw = jnp.zeros((d, LANES), F32).at[:, :n_e].set(router_w)
    rb = jnp.zeros((1, LANES), F32).at[0, :n_e].set(router_b)
    idx = np.arange(MOE_GROUP)
    tri = jnp.asarray(idx[None, :] < idx[:, None], BF16)
    n_groups = tm // MOE_GROUP
    return pl.pallas_call(
        functools.partial(_moe_body, alpha=alpha, n_experts=n_e),
        grid=(t // tm, n_e),
        in_specs=[pl.BlockSpec((tm, d), lambda i, e: (i, 0)),
                  pl.BlockSpec((None, 6, d), lambda i, e: (geom.mod_row(i, tm), 0, 0)),
                  _full((d, LANES)), _full((1, LANES)), _full((MOE_GROUP, MOE_GROUP)),
                  pl.BlockSpec((None, d, ffe), lambda i, e: (e, 0, 0)),
                  pl.BlockSpec((None, d, ffe), lambda i, e: (e, 0, 0)),
                  pl.BlockSpec((None, ffe, d), lambda i, e: (e, 0, 0)),
                  _full((1, d)), _full((1, d))],
        out_specs=pl.BlockSpec((tm, d), lambda i, e: (i, 0)),
        out_shape=jax.ShapeDtypeStruct((t, d), F32),
        scratch_shapes=[pltpu.VMEM((tm, d), BF16), pltpu.VMEM((tm, LANES), F32),
                        pltpu.VMEM((tm, LANES), F32), pltpu.VMEM((tm, LANES), F32),
                        pltpu.VMEM((n_groups, LANES, MOE_GROUP), F32),
                        pltpu.VMEM((n_groups, LANES, MOE_GROUP), F32),
                        pltpu.VMEM((tm, d), F32)],
        compiler_params=_params(("parallel", "arbitrary")),
        name="moe",
    )(x, mods, rw, rb, tri, w1.astype(BF16), w3.astype(BF16), w2.astype(BF16),
      ln_g.reshape(1, d), ln_b.reshape(1, d))


def _pool_matrices(tm, seq_len):
    t = np.arange(tm)
    pos = t % seq_len
    mats, inv = [], []
    for win in POOL_WINDOWS:
        lo = np.clip(pos - win // 2, 0, seq_len)
        hi = np.clip(pos + win // 2, 0, seq_len)
        base = t - pos
        j = t[None, :]
        mats.append(((j >= (base + lo)[:, None]) & (j < (base + hi)[:, None])).astype(np.float32))
        inv.append((hi - lo).astype(np.float32))
    return np.stack(mats), np.stack(inv)


def _pool_body(x_ref, mod_ref, pm_ref, cnt_ref, pw_ref, ps_ref, lng_ref, lnb_ref, o_ref, *, alpha):
    x = x_ref[...]
    sh, sc, gt = mod_ref[0:1, :], mod_ref[1:2, :], mod_ref[2:3, :]
    h = x * (1.0 + sc) + sh
    n_g = pm_ref.shape[0]
    gw = x.shape[1] // n_g
    outs = []
    for g in range(n_g):
        hg = h[:, g * gw:(g + 1) * gw]
        total = _dot_rhs2(pm_ref[g], hg)
        pooled = total / cnt_ref[g] - hg
        outs.append(_dot(pooled, pw_ref[g]))
    y = jnp.concatenate(outs, axis=-1) * ps_ref[...]
    o_ref[...] = _layer_norm(alpha * x + gt * y, lng_ref[...], lnb_ref[...])


def _pool_layer(geom, x, mods, pool_w, pool_scale, ln_g, ln_b, alpha, tm=256):
    t, d = x.shape
    n_g = len(POOL_WINDOWS)
    m_lat, c_lat = _pool_matrices(tm, GRID_W)
    m_ctx, c_ctx = _pool_matrices(tm, geom.ctx)
    pm = jnp.asarray(np.stack([m_lat, m_ctx]), BF16)
    cnt = jnp.asarray(np.stack([c_lat, c_ctx])[..., None], F32)
    n_lat_tiles = geom.n_lat // tm
    kind = lambda i: jnp.where(i < n_lat_tiles, 0, 1)
    return pl.pallas_call(
        functools.partial(_pool_body, alpha=alpha),
        grid=(t // tm,),
        in_specs=[pl.BlockSpec((tm, d), lambda i: (i, 0)),
                  pl.BlockSpec((None, 6, d), lambda i: (geom.mod_row(i, tm), 0, 0)),
                  pl.BlockSpec((None, n_g, tm, tm), lambda i: (kind(i), 0, 0, 0)),
                  pl.BlockSpec((None, n_g, tm, 1), lambda i: (kind(i), 0, 0, 0)),
                  _full((n_g, d // n_g, d // n_g)), _full((1, d)), _full((1, d)), _full((1, d))],
        out_specs=pl.BlockSpec((tm, d), lambda i: (i, 0)),
        out_shape=jax.ShapeDtypeStruct((t, d), F32),
        compiler_params=_params(("parallel",)),
        name="pool",
    )(x, mods, pm, cnt, pool_w.astype(BF16), pool_scale.reshape(1, d),
      ln_g.reshape(1, d), ln_b.reshape(1, d))


def _scan_tile_index(geom, i, tm):
    n_lat_tiles = geom.n_lat // tm
    tpb = geom.seq // tm
    is_lat = i < n_lat_tiles
    return jnp.where(is_lat, i // tpb, i - n_lat_tiles), jnp.where(is_lat, geom.ctx // tm + i % tpb, 0)


def _scan_step_chunk(geom, direction, s):
    if direction == 0:
        return s
    nc_ctx = geom.ctx // CHUNK
    nc = (geom.ctx + geom.seq) // CHUNK
    return jnp.where(s < nc_ctx, nc_ctx - 1 - s, nc - 1 + nc_ctx - s)


def _order_masks(direction):
    n = 2 * CHUNK
    ri = lax.broadcasted_iota(jnp.int32, (n, n), 0)
    ci = lax.broadcasted_iota(jnp.int32, (n, n), 1)
    rt, ct = ri & (CHUNK - 1), ci & (CHUNK - 1)
    ahead = rt - ct if direction == 0 else ct - rt
    return ahead > 0, ahead >= 0, ri < CHUNK, ci < CHUNK


INV_BASE = 8


def _inverse_masks(n):
    ri = lax.broadcasted_iota(jnp.int32, (n, n), 0)
    ci = lax.broadcasted_iota(jnp.int32, (n, n), 1)
    eye = (ri == ci).astype(F32)
    same = lambda size: (ri // size) == (ci // size)
    base = same(INV_BASE)
    levels = []
    size = INV_BASE
    while size < CHUNK:
        levels.append(same(2 * size) & ~same(size))
        size *= 2
    return eye, base, levels


def _unit_tri_inverse(mats, masks):
    eye, base, levels = masks
    ps = [jnp.where(base, -a, 0.0) for a in mats]
    ts = [eye + p for p in ps]
    for _ in range(int(math.log2(INV_BASE)) - 1):
        ps = [_dot(p, p) for p in ps]
        ts = [t + _dot(t, p) for t, p in zip(ts, ps)]
    for off in levels:
        cs = [_dot(jnp.where(off, a, 0.0), t) for a, t in zip(mats, ts)]
        ts = [t - _dot(t, c) for t, c in zip(ts, cs)]
    return ts


def _cum_matrices():
    i = np.arange(CHUNK)
    fwd = (i[None, :] <= i[:, None]).astype(np.float32)
    return jnp.asarray(np.stack([fwd, fwd.T]), BF16)


def _rk_feat_body(x_ref, xp_ref, xn_ref, mod_ref, mu_ref, wrkv_ref, w1_ref, w2_ref, w0_ref,
                  a1_ref, a2_ref, a0_ref, g1_ref, g2_ref, kk_ref, ka_ref, rk_ref, e_ref, et_ref,
                  r_out, kkn_out, v_out, gate_out, bonus_out, lw_out, b_out, kd_out,
                  *, n_lat_tiles, tiles_per_batch, seq):
    i = pl.program_id(0)
    tm, d = x_ref.shape
    q = d // 4
    sh, sc = mod_ref[0:1, :], mod_ref[1:2, :]
    h = x_ref[...] * (1.0 + sc) + sh
    hp = xp_ref[...] * (1.0 + sc) + sh
    hn = xn_ref[...] * (1.0 + sc) + sh
    is_ctx = i >= n_lat_tiles
    t = lax.broadcasted_iota(jnp.int32, (tm, 1), 0)
    col = t & (GRID_W - 1)
    pos = (i % tiles_per_batch) * tm + t
    to_end = (tm - 1) - t

    def prev_tok(z):
        return pltpu.roll(z, 1, 0)

    def next_tok(z):
        return pltpu.roll(z, tm - 1, 0)

    h0, h1, h2, h3 = (h[:, k * q:(k + 1) * q] for k in range(4))
    up = jnp.concatenate([hp[:, 2 * q:3 * q], h2[:tm - GRID_W]], axis=0)
    down = jnp.concatenate([h3[GRID_W:], hn[:, 3 * q:]], axis=0)
    edge0 = jnp.where(is_ctx, t, col)
    edge1 = jnp.where(is_ctx, t, (GRID_W - 1) - col)
    edge2 = jnp.where(is_ctx, to_end, jnp.maximum(pos - (GRID_W - 1), 0))
    edge3 = jnp.where(is_ctx, to_end, jnp.maximum((seq - GRID_W) - pos, 0))
    s0 = jnp.where(edge0 == 0, 0.0, prev_tok(h0))
    s1 = jnp.where(edge1 == 0, 0.0, jnp.where(is_ctx, prev_tok(h1), next_tok(h1)))
    s2 = jnp.where(edge2 == 0, 0.0, jnp.where(is_ctx, next_tok(h2), up))
    s3 = jnp.where(edge3 == 0, 0.0, jnp.where(is_ctx, next_tok(h3), down))
    xx = jnp.concatenate([s0, s1, s2, s3], axis=-1) - h
    xr, xw, xk, xv, xa, xg = (h + xx * mu_ref[m:m + 1, :] for m in range(6))

    r = _dot(xr, wrkv_ref[0])
    k = _dot(xk, wrkv_ref[1])
    v = _dot(xv, wrkv_ref[2])
    dl = jnp.tanh(_dot(xw, w1_ref[...]))
    al = _dot(xa, a1_ref[...])
    kkr = k * kk_ref[...]
    ss = _segsum(kkr * kkr, e_ref, et_ref)
    kkn = kkr / jnp.maximum(jnp.sqrt(ss), 1e-12)
    kd_sum = jnp.zeros_like(k)
    for z in range(2):
        lw = -RK_DECAY_SCALE * _sigmoid(w0_ref[z:z + 1, :] + _dot(dl, w2_ref[z]))
        a = _sigmoid(a0_ref[z:z + 1, :] + _dot(al, a2_ref[z]))
        kd = k * (1.0 + (a - 1.0) * ka_ref[...])
        lw_out[z] = lw
        b_out[z] = kkn * a
        kd_out[z] = kd
        kd_sum = kd_sum + kd
    coef = _segsum(r * kd_sum * rk_ref[...], e_ref, et_ref)
    r_out[...] = r
    kkn_out[...] = kkn
    v_out[...] = v
    bonus_out[...] = coef * v
    gate_out[...] = _dot(_sigmoid(_dot(xg, g1_ref[...])), g2_ref[...])


def _rk_scan_chunks(streams, cum_ref, s_scr):
    c = CHUNK
    zero = jnp.float32(0.0)
    inv_masks = _inverse_masks(2 * c)
    lane = lax.broadcasted_iota(jnp.int32, (1, LANES), 1)
    m0, m1 = lane < c, lane >= c

    def pick(m_first, m_second, first, second):
        return jnp.where(m_first, first, jnp.where(m_second, second, zero))

    chains = []
    for direction, b, (r_ref, kk_ref, v_ref, lw_ref, b_ref, kd_ref, y_ref) in streams:
        lw = lw_ref[b]
        cs = _dot_rhs2(cum_ref[direction], lw)
        cs_last = cs[c - 1:c, :] if direction == 0 else cs[0:1, :]
        eg, en, el = jnp.exp(cs), jnp.exp(-cs), jnp.exp(cs_last - cs)
        rt = r_ref[b] * eg
        kkt = kk_ref[b] * jnp.exp(cs - lw)
        bb, kd, v = b_ref[b], kd_ref[b], v_ref[b]
        bt, kt, bh, kh = bb * en, kd * en, bb * el, kd * el
        gl = jnp.exp(cs_last)
        strict, incl, top, left = _order_masks(direction)
        tl, br = top & left, (~top) & (~left)
        tr, bl = top & (~left), (~top) & left
        masks = dict(bk=(tl & strict, br & strict), kk=(tr & strict, bl & strict),
                     rb=(bl & incl, tr & incl), rk=(br & incl, tl & incl), diag=top == left)
        for p in range(r_ref.shape[-1] // LANES):
            s = slice(p * LANES, (p + 1) * LANES)
            chains.append(dict(rt=rt[:, s], kkt=kkt[:, s], bt=bt[:, s], kt=kt[:, s], bh=bh[:, s],
                               kh=kh[:, s], v=v[:, s], gl=gl[:, s], m=masks, y_ref=y_ref,
                               where=(direction, b, p), lanes=s))

    r0 = [_dot_nt(jnp.concatenate([jnp.where(m0, ch["kkt"], zero), jnp.where(m0, ch["rt"], zero)], 0),
                  jnp.concatenate([ch["bt"], ch["kt"]], 0)) for ch in chains]
    r1 = [_dot_nt(jnp.concatenate([jnp.where(m1, ch["rt"], zero), jnp.where(m1, ch["kkt"], zero)], 0),
                  jnp.concatenate([ch["kt"], ch["bt"]], 0)) for ch in chains]
    s_old = [s_scr[ch["where"]] for ch in chains]
    proj = [_dot_nt(jnp.concatenate([ch["kkt"], ch["rt"]], 0), st) for ch, st in zip(chains, s_old)]
    a_bk = [pick(*ch["m"]["bk"], x0, x1) for ch, x0, x1 in zip(chains, r0, r1)]
    a_kk = [pick(*ch["m"]["kk"], x0, x1) for ch, x0, x1 in zip(chains, r0, r1)]
    a_rbk = [jnp.concatenate([pick(*ch["m"]["rb"], x0, x1), pick(*ch["m"]["rk"], x0, x1)], 1)
             for ch, x0, x1 in zip(chains, r0, r1)]
    v_sw = [jnp.concatenate([jnp.where(m1, ch["v"], zero), jnp.where(m0, ch["v"], zero)], 0) for ch in chains]
    akkv = [_dot(a, x) for a, x in zip(a_kk, v_sw)]
    t_inv = _unit_tri_inverse(a_bk, inv_masks)
    ks_sm = [jnp.concatenate([jnp.where(m0, pj[:c], zero), jnp.where(m1, pj[:c], zero)], 0) for pj in proj]
    u_sm = [-_dot(t, k_ + a) for t, k_, a in zip(t_inv, ks_sm, akkv)]
    y_sm = [_dot(a, jnp.concatenate([u, x], 0)) for a, u, x in zip(a_rbk, u_sm, v_sw)]
    upd = [_dot_tn(jnp.concatenate([u[:c] + u[c:], ch["v"]], 0), jnp.concatenate([ch["bh"], ch["kh"]], 0))
           for ch, u in zip(chains, u_sm)]
    for i, ch in enumerate(chains):
        _, b, _ = ch["where"]
        ch["y_ref"][b, :, ch["lanes"]] = proj[i][c:] + y_sm[i][:c] + y_sm[i][c:]
        s_scr[ch["where"]] = s_old[i] * ch["gl"] + jnp.where(ch["m"]["diag"], upd[i], zero)


def _rk_scan_body(rf, rb, kkf, kkb, vf, vb, lwf, lwb, bf, bb, kdf, kdb, cum_ref, yf, yb, s_scr,
                  *, batches_per_trip):
    @pl.when(pl.program_id(0) == 0)
    def _():
        s_scr[...] = jnp.zeros(s_scr.shape, F32)

    fwd = (rf, kkf, vf, lwf, bf, kdf, yf)
    bwd = (rb, kkb, vb, lwb, bb, kdb, yb)

    def trip(i, carry):
        streams = []
        for k in range(batches_per_trip):
            b = i * batches_per_trip + k
            streams += [(0, b, fwd), (1, b, bwd)]
        _rk_scan_chunks(streams, cum_ref, s_scr)
        return carry

    lax.fori_loop(0, rf.shape[0] // batches_per_trip, trip, 0)


def _rk_out_body(x_ref, yf_ref, yb_ref, bonus_ref, gate_ref, mod_ref, lxg_ref, lxb_ref, wo_ref, e_ref,
                 et_ref, lng_ref, lnb_ref, o_ref, *, alpha, head):
    x = x_ref[...]
    y = yf_ref[...] + yb_ref[...]
    mean = _segsum(y, e_ref, et_ref) * (1.0 / head)
    cen = y - mean
    var = _segsum(cen * cen, e_ref, et_ref) * (1.0 / head)
    yn = cen * lax.rsqrt(var + RK_GN_EPS) * lxg_ref[...] + lxb_ref[...]
    out = _dot((yn + bonus_ref[...]) * gate_ref[...], wo_ref[...])
    gt = mod_ref[2:3, :]
    o_ref[...] = _layer_norm(alpha * x + gt * out, lng_ref[...], lnb_ref[...])


def _rwkv_layer(geom, x, mods, mu, w_rkv, w0, w1, w2, a0, a1, a2, g1, g2, k_k, k_a, r_k,
                lnx_g, lnx_b, w_o, ln_g, ln_b, alpha, tm=256):
    t, d = x.shape
    head = r_k.shape[-1]
    assert 2 * head == LANES and geom.ctx == tm and geom.seq % tm == 0
    lora_d, lora_a, lora_g = w1.shape[-1], a1.shape[-1], g1.shape[-1]
    e, et = _head_indicator(d, head)

    def cat_dirs(w):
        return jnp.concatenate([w[0], w[1]], axis=-1).astype(BF16)

    def pad_dirs(w):
        z = jnp.zeros_like(w[0])
        return jnp.stack([jnp.concatenate([w[0], z], 0), jnp.concatenate([z, w[1]], 0)]).astype(BF16)

    nb = t // GRID_W
    nt = geom.ctx + geom.seq
    row = lambda i: (i, 0)
    tile = pl.BlockSpec((tm, d), row)
    scan_tile = pl.BlockSpec((None, tm, d), lambda i: (*_scan_tile_index(geom, i, tm), 0))
    scan_tile2 = pl.BlockSpec((2, None, tm, d), lambda i: (0, *_scan_tile_index(geom, i, tm), 0))
    mod_spec = pl.BlockSpec((None, 6, d), lambda i: (geom.mod_row(i, tm), 0, 0))
    per = tm // GRID_W
    feat = pl.pallas_call(
        functools.partial(_rk_feat_body, n_lat_tiles=geom.n_lat // tm,
                          tiles_per_batch=geom.seq // tm, seq=geom.seq),
        grid=(t // tm,),
        in_specs=[tile,
                  pl.BlockSpec((GRID_W, d), lambda i: (jnp.maximum(i * per - 1, 0), 0)),
                  pl.BlockSpec((GRID_W, d), lambda i: (jnp.minimum(i * per + per, nb - 1), 0)),
                  mod_spec, _full((6, d)), _resident((3, d, d)),
                  _full((d, 2 * lora_d)), _full((2, 2 * lora_d, d)), _full((2, d)),
                  _full((d, 2 * lora_a)), _full((2, 2 * lora_a, d)), _full((2, d)),
                  _full((d, lora_g)), _full((lora_g, d)),
                  _full((1, d)), _full((1, d)), _full((1, d)), _full((d, LANES)), _full((LANES, d))],
        out_specs=[scan_tile, scan_tile, scan_tile, tile, tile, scan_tile2, scan_tile2, scan_tile2],
        out_shape=([jax.ShapeDtypeStruct((geom.b, nt, d), F32)] * 3 + [jax.ShapeDtypeStruct((t, d), F32)] * 2
                   + [jax.ShapeDtypeStruct((2, geom.b, nt, d), F32)] * 3),
        compiler_params=_params(("parallel",)),
        name="rwkv_feat",
    )
    r, kkn, v, gate, bonus, lw, bb, kd = feat(
        x, x, x, mods, mu, w_rkv.astype(BF16), cat_dirs(w1), pad_dirs(w2), w0,
        cat_dirs(a1), pad_dirs(a2), a0, g1.astype(BF16), g2.astype(BF16),
        k_k.reshape(1, d), k_a.reshape(1, d), r_k.reshape(1, d), e, et)

    n_steps = nt // CHUNK
    shared = [pl.BlockSpec((geom.b, CHUNK, d), lambda s, z=z: (0, _scan_step_chunk(geom, z, s), 0))
              for z in range(2)]
    per_dir = [pl.BlockSpec((None, geom.b, CHUNK, d), lambda s, z=z: (z, 0, _scan_step_chunk(geom, z, s), 0))
               for z in range(2)]
    yf, yb = pl.pallas_call(
        functools.partial(_rk_scan_body, batches_per_trip=1),
        grid=(n_steps,),
        in_specs=shared * 3 + per_dir * 3 + [_full((2, CHUNK, CHUNK))],
        out_specs=shared,
        out_shape=[jax.ShapeDtypeStruct((geom.b, nt, d), F32)] * 2,
        scratch_shapes=[pltpu.VMEM((2, geom.b, d // LANES, LANES, LANES), F32)],
        compiler_params=_params(("arbitrary",)),
        name="rwkv_scan",
    )(r, r, kkn, kkn, v, v, lw, lw, bb, bb, kd, kd, _cum_matrices())

    return pl.pallas_call(
        functools.partial(_rk_out_body, alpha=alpha, head=head),
        grid=(t // tm,),
        in_specs=[tile, scan_tile, scan_tile, tile, tile, mod_spec, _full((1, d)), _full((1, d)),
                  _resident((d, d)), _full((d, LANES)), _full((LANES, d)), _full((1, d)), _full((1, d))],
        out_specs=tile,
        out_shape=jax.ShapeDtypeStruct((t, d), F32),
        compiler_params=_params(("parallel",)),
        name="rwkv_out",
    )(x, yf, yb, bonus, gate, mods, lnx_g.reshape(1, d), lnx_b.reshape(1, d), w_o.astype(BF16), e, et,
      ln_g.reshape(1, d), ln_b.reshape(1, d))


HALO = 8


def _gdn_feat_body(x_ref, xp_ref, xn_ref, mod_ref, wq_ref, wz_ref, wab_ref, cw_ref, alog_ref, dtb_ref,
                   q_out, k_out, v_out, z_out, gb_out, proj_scr,
                   *, n_lat_tiles, tiles_per_batch, head, n_taps):
    i = pl.program_id(0)
    tm, d = x_ref.shape
    sh, sc = mod_ref[0:1, :], mod_ref[1:2, :]
    h = x_ref[...] * (1.0 + sc) + sh
    is_ctx = i >= n_lat_tiles
    j = i % tiles_per_batch
    has_prev = jnp.logical_and(jnp.logical_not(is_ctx), j > 0)
    has_next = jnp.logical_and(jnp.logical_not(is_ctx), j < tiles_per_batch - 1)
    hp = jnp.where(has_prev, xp_ref[...] * (1.0 + sc) + sh, 0.0)
    hn = jnp.where(has_next, xn_ref[...] * (1.0 + sc) + sh, 0.0)
    hb = h.astype(BF16)
    proj_scr[0:HALO, :] = _dot(hp, wq_ref[...])
    proj_scr[HALO:HALO + tm, :] = jnp.dot(hb, wq_ref[...], preferred_element_type=F32)
    proj_scr[HALO + tm:, :] = _dot(hn, wq_ref[...])
    left = n_taps // 2
    conv = jnp.zeros((tm, 3 * d), F32)
    for tap in range(n_taps):
        conv = conv + proj_scr[pl.ds(HALO + tap - left, tm), :] * cw_ref[tap:tap + 1, :]
    qkv = _silu(conv)
    n_heads = d // head
    for hd in range(n_heads):
        qs = qkv[:, hd * head:(hd + 1) * head]
        ks = qkv[:, d + hd * head:d + (hd + 1) * head]
        q_out[:, hd * head:(hd + 1) * head] = (
            qs * lax.rsqrt(jnp.sum(qs * qs, axis=-1, keepdims=True) + GDN_L2_EPS) * head ** -0.5)
        k_out[:, hd * head:(hd + 1) * head] = (
            ks * lax.rsqrt(jnp.sum(ks * ks, axis=-1, keepdims=True) + GDN_L2_EPS))
    v_out[...] = qkv[:, 2 * d:]
    z_out[...] = jnp.dot(hb, wz_ref[...], preferred_element_type=F32)
    ab = _dot3(h, wab_ref[...])
    lane = lax.broadcasted_iota(jnp.int32, ab.shape, 1)
    g = -jnp.exp(alog_ref[...]) * _softplus(ab + dtb_ref[...])
    gb_out[...] = jnp.where(lane < 2 * n_heads, g, _sigmoid(ab))


def _gdn_scan_chunks(streams, s_scr, n_heads):
    c = CHUNK
    zero = jnp.float32(0.0)
    inv_masks = _inverse_masks(2 * c)
    row = lax.broadcasted_iota(jnp.int32, (2 * c, 1), 0)
    top_rows = row < c

    chains = []
    for direction, b, (q_ref, k_ref, v_ref, gb_ref, y_ref) in streams:
        head = q_ref.shape[-1] // n_heads
        strict, incl, top, left = _order_masks(direction)
        same = top == left
        masks = dict(strict=same & strict, incl=same & incl,
                     cum=jnp.where(same & incl, 1.0, 0.0).astype(BF16),
                     nxt=jnp.where(same & strict, 1.0, 0.0))
        first, last = (0, c - 1) if direction == 0 else (c - 1, 0)
        gb = gb_ref[b]
        col = lambda idx: gb[:, idx:idx + 1]
        for p in range(n_heads // 2):
            h0, h1 = 2 * p, 2 * p + 1
            stack = lambda ref: jnp.concatenate([ref[b, :, h0 * head:(h0 + 1) * head],
                                                 ref[b, :, h1 * head:(h1 + 1) * head]], 0)
            chains.append(dict(
                g=jnp.concatenate([col(direction * n_heads + h0), col(direction * n_heads + h1)], 0),
                beta=jnp.concatenate([col((2 + direction) * n_heads + h0),
                                      col((2 + direction) * n_heads + h1)], 0),
                q=stack(q_ref), k=stack(k_ref), v=stack(v_ref),
                m=masks, first=first, last=last, y_ref=y_ref, b=b, head=head,
                where=((direction, b, h0), (direction, b, h1)),
                lanes=(slice(h0 * head, (h0 + 1) * head), slice(h1 * head, (h1 + 1) * head))))

    dlog = [_dot_rhs2(ch["m"]["cum"], ch["g"] * ch["m"]["nxt"]) for ch in chains]
    cs, cs_l0, cs_l1 = [], [], []
    for ch, dl in zip(chains, dlog):
        f, l, g = ch["first"], ch["last"], ch["g"]
        x = jnp.where(top_rows, dl[:, f:f + 1] + g[f:f + 1], dl[:, c + f:c + f + 1] + g[c + f:c + f + 1])
        cs.append(x)
        cs_l0.append(x[l:l + 1])
        cs_l1.append(x[c + l:c + l + 1])
    cs_end = [jnp.where(top_rows, x0, x1) for x0, x1 in zip(cs_l0, cs_l1)]
    decay = [jnp.exp(jnp.where(ch["m"]["incl"], x, -jnp.inf)) for ch, x in zip(chains, dlog)]
    kb = [ch["k"] * ch["beta"] for ch in chains]
    vb = [ch["v"] * ch["beta"] for ch in chains]
    kkt = [_dot_nt(x, ch["k"]) for x, ch in zip(kb, chains)]
    qkt = [_dot_nt(ch["q"], ch["k"]) for ch in chains]
    lower = [jnp.where(ch["m"]["strict"], x * dc, zero) for ch, x, dc in zip(chains, kkt, decay)]
    a_qk = [jnp.where(ch["m"]["incl"], x * dc, zero) for ch, x, dc in zip(chains, qkt, decay)]
    t_inv = _unit_tri_inverse(lower, inv_masks)
    eg = [jnp.exp(x) for x in cs]
    uw = [_dot(t, jnp.concatenate([x, y * e], axis=1)) for t, x, y, e in zip(t_inv, vb, kb, eg)]
    s_old = [(s_scr[ch["where"][0]], s_scr[ch["where"][1]]) for ch in chains]
    ws = [jnp.concatenate([_dot(x[:c, ch["head"]:], s0), _dot(x[c:, ch["head"]:], s1)], 0)
          for ch, x, (s0, s1) in zip(chains, uw, s_old)]
    qs = [jnp.concatenate([_dot((ch["q"] * e)[:c], s0), _dot((ch["q"] * e)[c:], s1)], 0)
          for ch, e, (s0, s1) in zip(chains, eg, s_old)]
    v_new = [x[:, :ch["head"]] - y for ch, x, y in zip(chains, uw, ws)]
    o = [x + _dot(a, y) for x, a, y in zip(qs, a_qk, v_new)]
    k_dec = [ch["k"] * jnp.exp(ce - y) for ch, ce, y in zip(chains, cs_end, cs)]
    for i, ch in enumerate(chains):
        ch["y_ref"][ch["b"], :, ch["lanes"][0]] = o[i][:c]
        ch["y_ref"][ch["b"], :, ch["lanes"][1]] = o[i][c:]
        s_scr[ch["where"][0]] = s_old[i][0] * jnp.exp(cs_l0[i]) + _dot_tn(k_dec[i][:c], v_new[i][:c])
        s_scr[ch["where"][1]] = s_old[i][1] * jnp.exp(cs_l1[i]) + _dot_tn(k_dec[i][c:], v_new[i][c:])


def _gdn_scan_body(qf, qb, kf, kb, vf, vb, gf, gb, yf, yb, s_scr, *, n_heads, batches_per_trip):
    @pl.when(pl.program_id(0) == 0)
    def _():
        s_scr[...] = jnp.zeros(s_scr.shape, F32)

    fwd = (qf, kf, vf, gf, yf)
    bwd = (qb, kb, vb, gb, yb)

    def trip(i, carry):
        streams = []
        for j in range(batches_per_trip):
            b = i * batches_per_trip + j
            streams += [(0, b, fwd), (1, b, bwd)]
        _gdn_scan_chunks(streams, s_scr, n_heads)
        return carry

    lax.fori_loop(0, qf.shape[0] // batches_per_trip, trip, 0)


def _gdn_out_body(x_ref, of_ref, ob_ref, z_ref, mod_ref, nw_ref, wo_ref, lng_ref, lnb_ref, out_ref,
                  *, alpha, head):
    x = x_ref[...]
    o = of_ref[...] + ob_ref[...]
    z = z_ref[...]
    d = x.shape[1]
    parts = []
    for hd in range(d // head):
        oh = o[:, hd * head:(hd + 1) * head]
        on = oh * lax.rsqrt(jnp.mean(oh * oh, axis=-1, keepdims=True) + GDN_NORM_EPS) * nw_ref[...]
        parts.append(on * _silu(z[:, hd * head:(hd + 1) * head]))
    y = _dot(jnp.concatenate(parts, axis=-1), wo_ref[...])
    gt = mod_ref[2:3, :]
    out_ref[...] = _layer_norm(alpha * x + gt * y, lng_ref[...], lnb_ref[...])


def _gdn_layer(geom, x, mods, w_in, conv_w, a_log, dt_bias, norm_w, w_o, ln_g, ln_b, alpha, tm=256):
    t, d = x.shape
    n_heads = a_log.shape[-1]
    head = d // n_heads
    n_taps = conv_w.shape[0]
    assert head == LANES and geom.ctx == tm and geom.seq % tm == 0 and n_heads % 2 == 0
    assert 4 * n_heads <= LANES
    w_qkv = w_in[:, :3 * d].astype(BF16)
    w_z = w_in[:, 3 * d:4 * d].astype(BF16)
    w_ab = jnp.zeros((d, LANES), F32).at[:, :4 * n_heads].set(w_in[:, 4 * d:])
    alog = jnp.zeros((1, LANES), F32).at[0, :2 * n_heads].set(a_log.reshape(-1))
    dtb = jnp.zeros((1, LANES), F32).at[0, :2 * n_heads].set(dt_bias.reshape(-1))

    nb = t // HALO
    nt = geom.ctx + geom.seq
    per = tm // HALO
    tile = pl.BlockSpec((tm, d), lambda i: (i, 0))
    scan_tile = pl.BlockSpec((None, tm, d), lambda i: (*_scan_tile_index(geom, i, tm), 0))
    small = pl.BlockSpec((None, tm, LANES), lambda i: (*_scan_tile_index(geom, i, tm), 0))
    mod_spec = pl.BlockSpec((None, 6, d), lambda i: (geom.mod_row(i, tm), 0, 0))
    q, k, v, z, gb = pl.pallas_call(
        functools.partial(_gdn_feat_body, n_lat_tiles=geom.n_lat // tm,
                          tiles_per_batch=geom.seq // tm, head=head, n_taps=n_taps),
        grid=(t // tm,),
        in_specs=[tile,
                  pl.BlockSpec((HALO, d), lambda i: (jnp.maximum(i * per - 1, 0), 0)),
                  pl.BlockSpec((HALO, d), lambda i: (jnp.minimum(i * per + per, nb - 1), 0)),
                  mod_spec, _resident((d, 3 * d)), _resident((d, d)), _full((d, LANES)),
                  _full((n_taps, 3 * d)), _full((1, LANES)), _full((1, LANES))],
        out_specs=[scan_tile, scan_tile, scan_tile, tile, small],
        out_shape=([jax.ShapeDtypeStruct((geom.b, nt, d), F32)] * 3 + [jax.ShapeDtypeStruct((t, d), F32)]
                   + [jax.ShapeDtypeStruct((geom.b, nt, LANES), F32)]),
        scratch_shapes=[pltpu.VMEM((tm + 2 * HALO, 3 * d), F32)],
        compiler_params=_params(("parallel",)),
        name="gdn_feat",
    )(x, x, x, mods, w_qkv, w_z, w_ab, conv_w, alog, dtb)

    n_steps = nt // CHUNK
    shared = [pl.BlockSpec((geom.b, CHUNK, d), lambda s, zz=zz: (0, _scan_step_chunk(geom, zz, s), 0))
              for zz in range(2)]
    shared_small = [pl.BlockSpec((geom.b, CHUNK, LANES), lambda s, zz=zz: (0, _scan_step_chunk(geom, zz, s), 0))
                    for zz in range(2)]
    of, ob = pl.pallas_call(
        functools.partial(_gdn_scan_body, n_heads=n_heads, batches_per_trip=2),
        grid=(n_steps,),
        in_specs=shared * 3 + shared_small,
        out_specs=shared,
        out_shape=[jax.ShapeDtypeStruct((geom.b, nt, d), F32)] * 2,
        scratch_shapes=[pltpu.VMEM((2, geom.b, n_heads, head, head), F32)],
        compiler_params=_params(("arbitrary",)),
        name="gdn_scan",
    )(q, q, k, k, v, v, gb, gb)

    return pl.pallas_call(
        functools.partial(_gdn_out_body, alpha=alpha, head=head),
        grid=(t // tm,),
        in_specs=[tile, scan_tile, scan_tile, tile, mod_spec, _full((1, head)), _resident((d, d)),
                  _full((1, d)), _full((1, d))],
        out_specs=tile,
        out_shape=jax.ShapeDtypeStruct((t, d), F32),
        compiler_params=_params(("parallel",)),
        name="gdn_out",
    )(x, of, ob, z, mods, norm_w.reshape(1, head), w_o.astype(BF16), ln_g.reshape(1, d), ln_b.reshape(1, d))


def kernel(x, c, ctx, c_ctx, ada_w, ada_b, ln_g, ln_b, rk_mu, rk_w_rkv, rk_w0, rk_w1, rk_w2, rk_a0, rk_a1, rk_a2, rk_g1, rk_g2, rk_k_k, rk_k_a, rk_r_k, rk_lnx_g, rk_lnx_b, rk_w_o, pool_w, pool_scale, gdn_w_in, gdn_conv_w, gdn_a_log, gdn_dt_bias, gdn_norm_w, gdn_w_o, ffn_w1, ffn_w3, ffn_w2, moe_router_w, moe_router_b, moe_w1, moe_w3, moe_w2):
    batch, seq, d = x.shape
    ctx_len = ctx.shape[1]
    depth = ada_w.shape[0]
    geom = _Geom(batch, seq, ctx_len, d)
    alpha = (2 * depth) ** 0.25
    n_mixers = 3

    rows = -(-(batch + 1) // 8) * 8
    cvec = jnp.zeros((rows, d), F32).at[:batch].set(c).at[batch].set(c_ctx)
    table = _ada_table(cvec, ada_w, ada_b)
    mods_all = table[:, :batch + 1].reshape(depth, batch + 1, 6, d)

    xs = jnp.concatenate([x.reshape(batch * seq, d), ctx.reshape(batch * ctx_len, d)], axis=0)
    for i in range(depth):
        mods = mods_all[i]
        kind, j = i % n_mixers, i // n_mixers
        if kind == 0:
            xs = _rwkv_layer(geom, xs, mods, rk_mu[j], rk_w_rkv[j], rk_w0[j], rk_w1[j], rk_w2[j],
                             rk_a0[j], rk_a1[j], rk_a2[j], rk_g1[j], rk_g2[j], rk_k_k[j], rk_k_a[j],
                             rk_r_k[j], rk_lnx_g[j], rk_lnx_b[j], rk_w_o[j], ln_g[i, 0], ln_b[i, 0], alpha)
        elif kind == 1:
            xs = _pool_layer(geom, xs, mods, pool_w[j], pool_scale[j], ln_g[i, 0], ln_b[i, 0], alpha)
        else:
            xs = _gdn_layer(geom, xs, mods, gdn_w_in[j], gdn_conv_w[j], gdn_a_log[j], gdn_dt_bias[j],
                            gdn_norm_w[j], gdn_w_o[j], ln_g[i, 0], ln_b[i, 0], alpha)
        e = i // 2
        if i % 2 == 0:
            xs = _ffn_layer(geom, xs, mods, ffn_w1[e], ffn_w3[e], ffn_w2[e], ln_g[i, 1], ln_b[i, 1], alpha)
        else:
            xs = _moe_layer(geom, xs, mods, moe_router_w[e], moe_router_b[e], moe_w1[e], moe_w3[e],
                            moe_w2[e], ln_g[i, 1], ln_b[i, 1], alpha)
    return xs[:batch * seq].reshape(batch, seq, d)
```

```python
import functools
import math

import jax
import jax.numpy as jnp
import numpy as np
from jax import lax
from jax.experimental import pallas as pl
from jax.experimental.pallas import tpu as pltpu

F32 = jnp.float32
BF16 = jnp.bfloat16

GRID_W = 64
CHUNK = 64
LANES = 128
POOL_WINDOWS = (2, 4, 8, 16)
LN_EPS = 1e-5
RK_DECAY_SCALE = math.exp(-0.5)
RK_GN_EPS = 64e-5
GDN_NORM_EPS = 1e-6
GDN_L2_EPS = 1e-6
TOP_K = 2
VMEM_LIMIT = 56 * 1024 * 1024


def _sigmoid(x):
    return 1.0 / (1.0 + jnp.exp(-x))


def _silu(x):
    return x * _sigmoid(x)


def _softplus(x):
    return jnp.maximum(x, 0.0) + jnp.log(1.0 + jnp.exp(-jnp.abs(x)))


def _dot(a, b):
    return jnp.dot(a.astype(BF16), b.astype(BF16), preferred_element_type=F32)


def _dot_nt(a, b):
    return lax.dot_general(a.astype(BF16), b.astype(BF16), (((1,), (1,)), ((), ())),
                           preferred_element_type=F32)


def _dot_tn(a, b):
    return lax.dot_general(a.astype(BF16), b.astype(BF16), (((0,), (0,)), ((), ())),
                           preferred_element_type=F32)


def _split(x):
    hi = x.astype(BF16)
    lo = (x - hi.astype(F32)).astype(BF16)
    return hi, lo


def _dot_lhs2(a, b_exact):
    hi, lo = _split(a)
    return (jnp.dot(hi, b_exact, preferred_element_type=F32)
            + jnp.dot(lo, b_exact, preferred_element_type=F32))


def _dot_rhs2(a_exact, b):
    hi, lo = _split(b)
    return (jnp.dot(a_exact, hi, preferred_element_type=F32)
            + jnp.dot(a_exact, lo, preferred_element_type=F32))


def _dot3(a, b):
    ah, al = _split(a)
    bh, bl = _split(b)
    return (jnp.dot(ah, bh, preferred_element_type=F32)
            + jnp.dot(ah, bl, preferred_element_type=F32)
            + jnp.dot(al, bh, preferred_element_type=F32))


def _segsum(x, e_ref, et_ref):
    s = _dot_lhs2(x, e_ref[...])
    return _dot_lhs2(s, et_ref[...])


def _layer_norm(v, g, b):
    mean = jnp.mean(v, axis=-1, keepdims=True)
    c = v - mean
    var = jnp.mean(c * c, axis=-1, keepdims=True)
    return c * lax.rsqrt(var + LN_EPS) * g + b


def _head_indicator(d_model, head):
    n_heads = d_model // head
    e = np.zeros((d_model, LANES), np.float32)
    e[np.arange(d_model), np.arange(d_model) // head] = 1.0
    assert n_heads <= LANES
    return jnp.asarray(e, BF16), jnp.asarray(e.T.copy(), BF16)


def _full(shape):
    n = len(shape)
    return pl.BlockSpec(shape, lambda *_: (0,) * n)


def _resident(shape):
    n = len(shape)
    return pl.BlockSpec(shape, lambda *_: (0,) * n, pipeline_mode=pl.Buffered(1))


def _params(sem):
    return pltpu.CompilerParams(dimension_semantics=sem, vmem_limit_bytes=VMEM_LIMIT)


class _Geom:
    def __init__(self, batch, seq, ctx_len, d_model):
        self.b, self.seq, self.ctx, self.d = batch, seq, ctx_len, d_model
        self.n_lat = batch * seq
        self.t = batch * (seq + ctx_len)

    def mod_row(self, i, tm):
        return jnp.where(i < self.n_lat // tm, i // (self.seq // tm), self.b)


def _ada_body(c_ref, w_ref, b_ref, o_ref):
    o_ref[...] = _dot3(_silu(c_ref[...]), w_ref[...]) + b_ref[...]


def _ada_table(cvec, ada_w, ada_b):
    depth, d, d6 = ada_w.shape
    rows = cvec.shape[0]
    return pl.pallas_call(
        _ada_body,
        grid=(depth, d6 // d),
        in_specs=[_full((rows, d)),
                  pl.BlockSpec((None, d, d), lambda l, j: (l, 0, j)),
                  pl.BlockSpec((None, 1, d), lambda l, j: (l, 0, j))],
        out_specs=pl.BlockSpec((None, rows, d), lambda l, j: (l, 0, j)),
        out_shape=jax.ShapeDtypeStruct((depth, rows, d6), F32),
        compiler_params=_params(("parallel", "parallel")),
        name="ada_table",
    )(cvec, ada_w, ada_b.reshape(depth, 1, d6))


def _ffn_body(x_ref, mod_ref, w1_ref, w3_ref, w2_ref, lng_ref, lnb_ref, o_ref, *, alpha, n_split):
    x = x_ref[...]
    sh, sc, gt = mod_ref[3:4, :], mod_ref[4:5, :], mod_ref[5:6, :]
    h = (x * (1.0 + sc) + sh).astype(BF16)
    ff = w1_ref.shape[1] // n_split
    acc = jnp.zeros(x.shape, F32)
    for s in range(n_split):
        a = jnp.dot(h, w1_ref[:, s * ff:(s + 1) * ff], preferred_element_type=F32)
        g = jnp.dot(h, w3_ref[:, s * ff:(s + 1) * ff], preferred_element_type=F32)
        acc = acc + jnp.dot((_silu(a) * g).astype(BF16), w2_ref[s * ff:(s + 1) * ff, :],
                            preferred_element_type=F32)
    o_ref[...] = _layer_norm(alpha * x + gt * acc, lng_ref[...], lnb_ref[...])


def _ffn_layer(geom, x, mods, w1, w3, w2, ln_g, ln_b, alpha, tm=512):
    t, d = x.shape
    ff = w1.shape[1]
    return pl.pallas_call(
        functools.partial(_ffn_body, alpha=alpha, n_split=2),
        grid=(t // tm,),
        in_specs=[pl.BlockSpec((tm, d), lambda i: (i, 0)),
                  pl.BlockSpec((None, 6, d), lambda i: (geom.mod_row(i, tm), 0, 0)),
                  _resident((d, ff)), _resident((d, ff)), _resident((ff, d)),
                  _full((1, d)), _full((1, d))],
        out_specs=pl.BlockSpec((tm, d), lambda i: (i, 0)),
        out_shape=jax.ShapeDtypeStruct((t, d), F32),
        compiler_params=_params(("parallel",)),
        name="ffn",
    )(x, mods, w1.astype(BF16), w3.astype(BF16), w2.astype(BF16),
      ln_g.reshape(1, d), ln_b.reshape(1, d))


MOE_GROUP = 512
MOE_ROWS = 160


def _moe_body(x_ref, mod_ref, rw_ref, rb_ref, tri_ref, w1_ref, w3_ref, w2_ref, lng_ref, lnb_ref, o_ref,
              h_scr, gate_scr, sel_scr, rank_scr, selt_scr, rankt_scr, acc_scr, *, alpha, n_experts):
    e = pl.program_id(1)
    tm = x_ref.shape[0]
    n_groups = tm // MOE_GROUP

    @pl.when(e == 0)
    def _():
        x = x_ref[...]
        sh, sc = mod_ref[3:4, :], mod_ref[4:5, :]
        h = x * (1.0 + sc) + sh
        h_scr[...] = h.astype(BF16)
        logits = _dot3(h, rw_ref[...]) + rb_ref[...]
        lane = lax.broadcasted_iota(jnp.int32, logits.shape, 1)
        neg = jnp.float32(-jnp.inf)
        logits = jnp.where(lane < n_experts, logits, neg)
        m1 = jnp.max(logits, axis=-1, keepdims=True)
        i1 = jnp.min(jnp.where(logits == m1, lane, LANES), axis=-1, keepdims=True)
        rest = jnp.where(lane == i1, neg, logits)
        m2 = jnp.max(rest, axis=-1, keepdims=True)
        i2 = jnp.min(jnp.where(rest == m2, lane, LANES), axis=-1, keepdims=True)
        e2 = jnp.exp(m2 - m1)
        p1 = 1.0 / (1.0 + e2)
        p2 = e2 / (1.0 + e2)
        gate_scr[...] = jnp.where(lane == i1, p1, 0.0) + jnp.where(lane == i2, p2, 0.0)
        sel = jnp.where(lane == i1, 1.0, 0.0) + jnp.where(lane == i2, 1.0, 0.0)
        sel_scr[...] = sel
        for s in range(n_groups):
            rows = slice(s * MOE_GROUP, (s + 1) * MOE_GROUP)
            rank = jnp.dot(tri_ref[...], sel[rows].astype(BF16), preferred_element_type=F32)
            rank_scr[rows, :] = rank
            rankt_scr[s] = rank.T
            selt_scr[s] = sel[rows].T
        acc_scr[...] = jnp.zeros(acc_scr.shape, F32)

    lane = lax.broadcasted_iota(jnp.int32, (MOE_GROUP, LANES), 1)
    column = lambda ref, rows: jnp.sum(jnp.where(lane == e, ref[rows, :], 0.0), axis=-1, keepdims=True)
    groups = []
    n_pass = jnp.int32(0)
    for s in range(n_groups):
        rows = slice(s * MOE_GROUP, (s + 1) * MOE_GROUP)
        sel_col = column(sel_scr, rows) > 0.5
        groups.append(dict(rows=rows, gate_col=column(gate_scr, rows), sel_col=sel_col,
                           rank_col=column(rank_scr, rows),
                           sel_row=selt_scr[s, pl.ds(e, 1), :] > 0.5,
                           rank_row=rankt_scr[s, pl.ds(e, 1), :]))
        count = jnp.sum(jnp.where(sel_col, 1.0, 0.0)).astype(jnp.int32)
        n_pass = jnp.maximum(n_pass, (count + MOE_ROWS - 1) // MOE_ROWS)

    def one_pass(j, carry):
        base = (j * MOE_ROWS).astype(F32)
        slot_col = lax.broadcasted_iota(jnp.int32, (MOE_ROWS, 1), 0).astype(F32) + base
        slot_row = lax.broadcasted_iota(jnp.int32, (1, MOE_ROWS), 1).astype(F32) + base
        xg = []
        for gr in groups:
            take = jnp.where((gr["rank_row"] == slot_col) & gr["sel_row"], 1.0, 0.0).astype(BF16)
            xg.append(jnp.dot(take, h_scr[gr["rows"], :], preferred_element_type=F32).astype(BF16))
        xg = jnp.concatenate(xg, axis=0)
        a = jnp.dot(xg, w1_ref[...], preferred_element_type=F32)
        g = jnp.dot(xg, w3_ref[...], preferred_element_type=F32)
        y = jnp.dot((_silu(a) * g).astype(BF16), w2_ref[...], preferred_element_type=F32)
        for k, gr in enumerate(groups):
            put = jnp.where((gr["rank_col"] == slot_row) & gr["sel_col"], 1.0, 0.0).astype(BF16)
            acc_scr[gr["rows"], :] += gr["gate_col"] * _dot_rhs2(put, y[k * MOE_ROWS:(k + 1) * MOE_ROWS])
        return carry

    lax.fori_loop(0, n_pass, one_pass, 0)

    @pl.when(e == n_experts - 1)
    def _():
        x = x_ref[...]
        gt = mod_ref[5:6, :]
        o_ref[...] = _layer_norm(alpha * x + gt * acc_scr[...], lng_ref[...], lnb_ref[...])


def _moe_layer(geom, x, mods, router_w, router_b, w1, w3, w2, ln_g, ln_b, alpha, tm=1024):
    t, d = x.shape
    n_e, _, ffe = w1.shape
    assert tm % MOE_GROUP == 0 and n_e <= LANES
    rw = jnp.zeros((d, LANES), F32).at[:, :n_e].set(router_w)
    rb = jnp.zeros((1, LANES), F32).at[0, :n_e].set(router_b)
    idx = np.arange(MOE_GROUP)
    tri = jnp.asarray(idx[None, :] < idx[:, None], BF16)
    n_groups = tm // MOE_GROUP
    return pl.pallas_call(
        functools.partial(_moe_body, alpha=alpha, n_experts=n_e),
        grid=(t // tm, n_e),
        in_specs=[pl.BlockSpec((tm, d), lambda i, e: (i, 0)),
                  pl.BlockSpec((None, 6, d), lambda i, e: (geom.mod_row(i, tm), 0, 0)),
                  _full((d, LANES)), _full((1, LANES)), _full((MOE_GROUP, MOE_GROUP)),
                  pl.BlockSpec((None, d, ffe), lambda i, e: (e, 0, 0)),
                  pl.BlockSpec((None, d, ffe), lambda i, e: (e, 0, 0)),
                  pl.BlockSpec((None, ffe, d), lambda i, e: (e, 0, 0)),
                  _full((1, d)), _full((1, d))],
        out_specs=pl.BlockSpec((tm, d), lambda i, e: (i, 0)),
        out_shape=jax.ShapeDtypeStruct((t, d), F32),
        scratch_shapes=[pltpu.VMEM((tm, d), BF16), pltpu.VMEM((tm, LANES), F32),
                        pltpu.VMEM((tm, LANES), F32), pltpu.VMEM((tm, LANES), F32),
                        pltpu.VMEM((n_groups, LANES, MOE_GROUP), F32),
                        pltpu.VMEM((n_groups, LANES, MOE_GROUP), F32),
                        pltpu.VMEM((tm, d), F32)],
        compiler_params=_params(("parallel", "arbitrary")),
        name="moe",
    )(x, mods, rw, rb, tri, w1.astype(BF16), w3.astype(BF16), w2.astype(BF16),
      ln_g.reshape(1, d), ln_b.reshape(1, d))


def _pool_matrices(tm, seq_len):
    t = np.arange(tm)
    pos = t % seq_len
    mats, inv = [], []
    for win in POOL_WINDOWS:
        lo = np.clip(pos - win // 2, 0, seq_len)
        hi = np.clip(pos + win // 2, 0, seq_len)
        base = t - pos
        j = t[None, :]
        mats.append(((j >= (base + lo)[:, None]) & (j < (base + hi)[:, None])).astype(np.float32))
        inv.append((hi - lo).astype(np.float32))
    return np.stack(mats), np.stack(inv)


def _pool_body(x_ref, mod_ref, pm_ref, cnt_ref, pw_ref, ps_ref, lng_ref, lnb_ref, o_ref, *, alpha):
    x = x_ref[...]
    sh, sc, gt = mod_ref[0:1, :], mod_ref[1:2, :], mod_ref[2:3, :]
    h = x * (1.0 + sc) + sh
    n_g = pm_ref.shape[0]
    gw = x.shape[1] // n_g
    outs = []
    for g in range(n_g):
        hg = h[:, g * gw:(g + 1) * gw]
        total = _dot_rhs2(pm_ref[g], hg)
        pooled = total / cnt_ref[g] - hg
        outs.append(_dot(pooled, pw_ref[g]))
    y = jnp.concatenate(outs, axis=-1) * ps_ref[...]
    o_ref[...] = _layer_norm(alpha * x + gt * y, lng_ref[...], lnb_ref[...])


def _pool_layer(geom, x, mods, pool_w, pool_scale, ln_g, ln_b, alpha, tm=256):
    t, d = x.shape
    n_g = len(POOL_WINDOWS)
    m_lat, c_lat = _pool_matrices(tm, GRID_W)
    m_ctx, c_ctx = _pool_matrices(tm, geom.ctx)
    pm = jnp.asarray(np.stack([m_lat, m_ctx]), BF16)
    cnt = jnp.asarray(np.stack([c_lat, c_ctx])[..., None], F32)
    n_lat_tiles = geom.n_lat // tm
    kind = lambda i: jnp.where(i < n_lat_tiles, 0, 1)
    return pl.pallas_call(
        functools.partial(_pool_body, alpha=alpha),
        grid=(t // tm,),
        in_specs=[pl.BlockSpec((tm, d), lambda i: (i, 0)),
                  pl.BlockSpec((None, 6, d), lambda i: (geom.mod_row(i, tm), 0, 0)),
                  pl.BlockSpec((None, n_g, tm, tm), lambda i: (kind(i), 0, 0, 0)),
                  pl.BlockSpec((None, n_g, tm, 1), lambda i: (kind(i), 0, 0, 0)),
                  _full((n_g, d // n_g, d // n_g)), _full((1, d)), _full((1, d)), _full((1, d))],
        out_specs=pl.BlockSpec((tm, d), lambda i: (i, 0)),
        out_shape=jax.ShapeDtypeStruct((t, d), F32),
        compiler_params=_params(("parallel",)),
        name="pool",
    )(x, mods, pm, cnt, pool_w.astype(BF16), pool_scale.reshape(1, d),
      ln_g.reshape(1, d), ln_b.reshape(1, d))


def _scan_tile_index(geom, i, tm):
    n_lat_tiles = geom.n_lat // tm
    tpb = geom.seq // tm
    is_lat = i < n_lat_tiles
    return jnp.where(is_lat, i // tpb, i - n_lat_tiles), jnp.where(is_lat, geom.ctx // tm + i % tpb, 0)


def _scan_step_chunk(geom, direction, s):
    if direction == 0:
        return s
    nc_ctx = geom.ctx // CHUNK
    nc = (geom.ctx + geom.seq) // CHUNK
    return jnp.where(s < nc_ctx, nc_ctx - 1 - s, nc - 1 + nc_ctx - s)


def _order_masks(direction):
    n = 2 * CHUNK
    ri = lax.broadcasted_iota(jnp.int32, (n, n), 0)
    ci = lax.broadcasted_iota(jnp.int32, (n, n), 1)
    rt, ct = ri & (CHUNK - 1), ci & (CHUNK - 1)
    ahead = rt - ct if direction == 0 else ct - rt
    return ahead > 0, ahead >= 0, ri < CHUNK, ci < CHUNK


INV_BASE = 8


def _inverse_masks(n):
    ri = lax.broadcasted_iota(jnp.int32, (n, n), 0)
    ci = lax.broadcasted_iota(jnp.int32, (n, n), 1)
    eye = (ri == ci).astype(F32)
    same = lambda size: (ri // size) == (ci // size)
    base = same(INV_BASE)
    levels = []
    size = INV_BASE
    while size < CHUNK:
        levels.append(same(2 * size) & ~same(size))
        size *= 2
    return eye, base, levels


def _unit_tri_inverse(mats, masks):
    eye, base, levels = masks
    ps = [jnp.where(base, -a, 0.0) for a in mats]
    ts = [eye + p for p in ps]
    for _ in range(int(math.log2(INV_BASE)) - 1):
        ps = [_dot(p, p) for p in ps]
        ts = [t + _dot(t, p) for t, p in zip(ts, ps)]
    for off in levels:
        cs = [_dot(jnp.where(off, a, 0.0), t) for a, t in zip(mats, ts)]
        ts = [t - _dot(t, c) for t, c in zip(ts, cs)]
    return ts


def _cum_matrices():
    i = np.arange(CHUNK)
    fwd = (i[None, :] <= i[:, None]).astype(np.float32)
    return jnp.asarray(np.stack([fwd, fwd.T]), BF16)


def _rk_feat_body(x_ref, xp_ref, xn_ref, mod_ref, mu_ref, wrkv_ref, w1_ref, w2_ref, w0_ref,
                  a1_ref, a2_ref, a0_ref, g1_ref, g2_ref, kk_ref, ka_ref, rk_ref, e_ref, et_ref,
                  r_out, kkn_out, v_out, gate_out, bonus_out, lw_out, b_out, kd_out,
                  *, n_lat_tiles, tiles_per_batch, seq):
    i = pl.program_id(0)
    tm, d = x_ref.shape
    q = d // 4
    sh, sc = mod_ref[0:1, :], mod_ref[1:2, :]
    h = x_ref[...] * (1.0 + sc) + sh
    hp = xp_ref[...] * (1.0 + sc) + sh
    hn = xn_ref[...] * (1.0 + sc) + sh
    is_ctx = i >= n_lat_tiles
    t = lax.broadcasted_iota(jnp.int32, (tm, 1), 0)
    col = t & (GRID_W - 1)
    pos = (i % tiles_per_batch) * tm + t
    to_end = (tm - 1) - t

    def prev_tok(z):
        return pltpu.roll(z, 1, 0)

    def next_tok(z):
        return pltpu.roll(z, tm - 1, 0)

    h0, h1, h2, h3 = (h[:, k * q:(k + 1) * q] for k in range(4))
    up = jnp.concatenate([hp[:, 2 * q:3 * q], h2[:tm - GRID_W]], axis=0)
    down = jnp.concatenate([h3[GRID_W:], hn[:, 3 * q:]], axis=0)
    edge0 = jnp.where(is_ctx, t, col)
    edge1 = jnp.where(is_ctx, t, (GRID_W - 1) - col)
    edge2 = jnp.where(is_ctx, to_end, jnp.maximum(pos - (GRID_W - 1), 0))
    edge3 = jnp.where(is_ctx, to_end, jnp.maximum((seq - GRID_W) - pos, 0))
    s0 = jnp.where(edge0 == 0, 0.0, prev_tok(h0))
    s1 = jnp.where(edge1 == 0, 0.0, jnp.where(is_ctx, prev_tok(h1), next_tok(h1)))
    s2 = jnp.where(edge2 == 0, 0.0, jnp.where(is_ctx, next_tok(h2), up))
    s3 = jnp.where(edge3 == 0, 0.0, jnp.where(is_ctx, next_tok(h3), down))
    xx = jnp.concatenate([s0, s1, s2, s3], axis=-1) - h
    xr, xw, xk, xv, xa, xg = (h + xx * mu_ref[m:m + 1, :] for m in range(6))

    r = _dot(xr, wrkv_ref[0])
    k = _dot(xk, wrkv_ref[1])
    v = _dot(xv, wrkv_ref[2])
    dl = jnp.tanh(_dot(xw, w1_ref[...]))
    al = _dot(xa, a1_ref[...])
    kkr = k * kk_ref[...]
    ss = _segsum(kkr * kkr, e_ref, et_ref)
    kkn = kkr / jnp.maximum(jnp.sqrt(ss), 1e-12)
    kd_sum = jnp.zeros_like(k)
    for z in range(2):
        lw = -RK_DECAY_SCALE * _sigmoid(w0_ref[z:z + 1, :] + _dot(dl, w2_ref[z]))
        a = _sigmoid(a0_ref[z:z + 1, :] + _dot(al, a2_ref[z]))
        kd = k * (1.0 + (a - 1.0) * ka_ref[...])
        lw_out[z] = lw
        b_out[z] = kkn * a
        kd_out[z] = kd
        kd_sum = kd_sum + kd
    coef = _segsum(r * kd_sum * rk_ref[...], e_ref, et_ref)
    r_out[...] = r
    kkn_out[...] = kkn
    v_out[...] = v
    bonus_out[...] = coef * v
    gate_out[...] = _dot(_sigmoid(_dot(xg, g1_ref[...])), g2_ref[...])


def _rk_scan_chunks(streams, cum_ref, s_scr):
    c = CHUNK
    zero = jnp.float32(0.0)
    inv_masks = _inverse_masks(2 * c)
    lane = lax.broadcasted_iota(jnp.int32, (1, LANES), 1)
    m0, m1 = lane < c, lane >= c

    def pick(m_first, m_second, first, second):
        return jnp.where(m_first, first, jnp.where(m_second, second, zero))

    chains = []
    for direction, b, (r_ref, kk_ref, v_ref, lw_ref, b_ref, kd_ref, y_ref) in streams:
        lw = lw_ref[b]
        cs = _dot_rhs2(cum_ref[direction], lw)
        cs_last = cs[c - 1:c, :] if direction == 0 else cs[0:1, :]
        eg, en, el = jnp.exp(cs), jnp.exp(-cs), jnp.exp(cs_last - cs)
        rt = r_ref[b] * eg
        kkt = kk_ref[b] * jnp.exp(cs - lw)
        bb, kd, v = b_ref[b], kd_ref[b], v_ref[b]
        bt, kt, bh, kh = bb * en, kd * en, bb * el, kd * el
        gl = jnp.exp(cs_last)
        strict, incl, top, left = _order_masks(direction)
        tl, br = top & left, (~top) & (~left)
        tr, bl = top & (~left), (~top) & left
        masks = dict(bk=(tl & strict, br & strict), kk=(tr & strict, bl & strict),
                     rb=(bl & incl, tr & incl), rk=(br & incl, tl & incl), diag=top == left)
        for p in range(r_ref.shape[-1] // LANES):
            s = slice(p * LANES, (p + 1) * LANES)
            chains.append(dict(rt=rt[:, s], kkt=kkt[:, s], bt=bt[:, s], kt=kt[:, s], bh=bh[:, s],
                               kh=kh[:, s], v=v[:, s], gl=gl[:, s], m=masks, y_ref=y_ref,
                               where=(direction, b, p), lanes=s))

    r0 = [_dot_nt(jnp.concatenate([jnp.where(m0, ch["kkt"], zero), jnp.where(m0, ch["rt"], zero)], 0),
                  jnp.concatenate([ch["bt"], ch["kt"]], 0)) for ch in chains]
    r1 = [_dot_nt(jnp.concatenate([jnp.where(m1, ch["rt"], zero), jnp.where(m1, ch["kkt"], zero)], 0),
                  jnp.concatenate([ch["kt"], ch["bt"]], 0)) for ch in chains]
    s_old = [s_scr[ch["where"]] for ch in chains]
    proj = [_dot_nt(jnp.concatenate([ch["kkt"], ch["rt"]], 0), st) for ch, st in zip(chains, s_old)]
    a_bk = [pick(*ch["m"]["bk"], x0, x1) for ch, x0, x1 in zip(chains, r0, r1)]
    a_kk = [pick(*ch["m"]["kk"], x0, x1) for ch, x0, x1 in zip(chains, r0, r1)]
    a_rbk = [jnp.concatenate([pick(*ch["m"]["rb"], x0, x1), pick(*ch["m"]["rk"], x0, x1)], 1)
             for ch, x0, x1 in zip(chains, r0, r1)]
    v_sw = [jnp.concatenate([jnp.where(m1, ch["v"], zero), jnp.where(m0, ch["v"], zero)], 0) for ch in chains]
    akkv = [_dot(a, x) for a, x in zip(a_kk, v_sw)]
    t_inv = _unit_tri_inverse(a_bk, inv_masks)
    ks_sm = [jnp.concatenate([jnp.where(m0, pj[:c], zero), jnp.where(m1, pj[:c], zero)], 0) for pj in proj]
    u_sm = [-_dot(t, k_ + a) for t, k_, a in zip(t_inv, ks_sm, akkv)]
    y_sm = [_dot(a, jnp.concatenate([u, x], 0)) for a, u, x in zip(a_rbk, u_sm, v_sw)]
    upd = [_dot_tn(jnp.concatenate([u[:c] + u[c:], ch["v"]], 0), jnp.concatenate([ch["bh"], ch["kh"]], 0))
           for ch, u in zip(chains, u_sm)]
    for i, ch in enumerate(chains):
        _, b, _ = ch["where"]
        ch["y_ref"][b, :, ch["lanes"]] = proj[i][c:] + y_sm[i][:c] + y_sm[i][c:]
        s_scr[ch["where"]] = s_old[i] * ch["gl"] + jnp.where(ch["m"]["diag"], upd[i], zero)


def _rk_scan_body(rf, rb, kkf, kkb, vf, vb, lwf, lwb, bf, bb, kdf, kdb, cum_ref, yf, yb, s_scr,
                  *, batches_per_trip):
    @pl.when(pl.program_id(0) == 0)
    def _():
        s_scr[...] = jnp.zeros(s_scr.shape, F32)

    fwd = (rf, kkf, vf, lwf, bf, kdf, yf)
    bwd = (rb, kkb, vb, lwb, bb, kdb, yb)

    def trip(i, carry):
        streams = []
        for k in range(batches_per_trip):
            b = i * batches_per_trip + k
            streams += [(0, b, fwd), (1, b, bwd)]
        _rk_scan_chunks(streams, cum_ref, s_scr)
        return carry

    lax.fori_loop(0, rf.shape[0] // batches_per_trip, trip, 0)


def _rk_out_body(x_ref, yf_ref, yb_ref, bonus_ref, gate_ref, mod_ref, lxg_ref, lxb_ref, wo_ref, e_ref,
                 et_ref, lng_ref, lnb_ref, o_ref, *, alpha, head):
    x = x_ref[...]
    y = yf_ref[...] + yb_ref[...]
    mean = _segsum(y, e_ref, et_ref) * (1.0 / head)
    cen = y - mean
    var = _segsum(cen * cen, e_ref, et_ref) * (1.0 / head)
    yn = cen * lax.rsqrt(var + RK_GN_EPS) * lxg_ref[...] + lxb_ref[...]
    out = _dot((yn + bonus_ref[...]) * gate_ref[...], wo_ref[...])
    gt = mod_ref[2:3, :]
    o_ref[...] = _layer_norm(alpha * x + gt * out, lng_ref[...], lnb_ref[...])


def _rwkv_layer(geom, x, mods, mu, w_rkv, w0, w1, w2, a0, a1, a2, g1, g2, k_k, k_a, r_k,
                lnx_g, lnx_b, w_o, ln_g, ln_b, alpha, want_ctx=True, tm=256):
    t, d = x.shape
    t_out = t if want_ctx else geom.n_lat
    head = r_k.shape[-1]
    assert 2 * head == LANES and geom.ctx == tm and geom.seq % tm == 0
    lora_d, lora_a, lora_g = w1.shape[-1], a1.shape[-1], g1.shape[-1]
    e, et = _head_indicator(d, head)

    def cat_dirs(w):
        return jnp.concatenate([w[0], w[1]], axis=-1).astype(BF16)

    def pad_dirs(w):
        z = jnp.zeros_like(w[0])
        return jnp.stack([jnp.concatenate([w[0], z], 0), jnp.concatenate([z, w[1]], 0)]).astype(BF16)

    nb = t // GRID_W
    nt = geom.ctx + geom.seq
    row = lambda i: (i, 0)
    tile = pl.BlockSpec((tm, d), row)
    scan_tile = pl.BlockSpec((None, tm, d), lambda i: (*_scan_tile_index(geom, i, tm), 0))
    scan_tile2 = pl.BlockSpec((2, None, tm, d), lambda i: (0, *_scan_tile_index(geom, i, tm), 0))
    mod_spec = pl.BlockSpec((None, 6, d), lambda i: (geom.mod_row(i, tm), 0, 0))
    per = tm // GRID_W
    feat = pl.pallas_call(
        functools.partial(_rk_feat_body, n_lat_tiles=geom.n_lat // tm,
                          tiles_per_batch=geom.seq // tm, seq=geom.seq),
        grid=(t // tm,),
        in_specs=[tile,
                  pl.BlockSpec((GRID_W, d), lambda i: (jnp.maximum(i * per - 1, 0), 0)),
                  pl.BlockSpec((GRID_W, d), lambda i: (jnp.minimum(i * per + per, nb - 1), 0)),
                  mod_spec, _full((6, d)), _resident((3, d, d)),
                  _full((d, 2 * lora_d)), _full((2, 2 * lora_d, d)), _full((2, d)),
                  _full((d, 2 * lora_a)), _full((2, 2 * lora_a, d)), _full((2, d)),
                  _full((d, lora_g)), _full((lora_g, d)),
                  _full((1, d)), _full((1, d)), _full((1, d)), _full((d, LANES)), _full((LANES, d))],
        out_specs=[scan_tile, scan_tile, scan_tile, tile, tile, scan_tile2, scan_tile2, scan_tile2],
        out_shape=([jax.ShapeDtypeStruct((geom.b, nt, d), F32)] * 3 + [jax.ShapeDtypeStruct((t, d), F32)] * 2
                   + [jax.ShapeDtypeStruct((2, geom.b, nt, d), F32)] * 3),
        compiler_params=_params(("parallel",)),
        name="rwkv_feat",
    )
    r, kkn, v, gate, bonus, lw, bb, kd = feat(
        x, x, x, mods, mu, w_rkv.astype(BF16), cat_dirs(w1), pad_dirs(w2), w0,
        cat_dirs(a1), pad_dirs(a2), a0, g1.astype(BF16), g2.astype(BF16),
        k_k.reshape(1, d), k_a.reshape(1, d), r_k.reshape(1, d), e, et)

    n_steps = nt // CHUNK
    shared = [pl.BlockSpec((geom.b, CHUNK, d), lambda s, z=z: (0, _scan_step_chunk(geom, z, s), 0))
              for z in range(2)]
    per_dir = [pl.BlockSpec((None, geom.b, CHUNK, d), lambda s, z=z: (z, 0, _scan_step_chunk(geom, z, s), 0))
               for z in range(2)]
    yf, yb = pl.pallas_call(
        functools.partial(_rk_scan_body, batches_per_trip=1),
        grid=(n_steps,),
        in_specs=shared * 3 + per_dir * 3 + [_full((2, CHUNK, CHUNK))],
        out_specs=shared,
        out_shape=[jax.ShapeDtypeStruct((geom.b, nt, d), F32)] * 2,
        scratch_shapes=[pltpu.VMEM((2, geom.b, d // LANES, LANES, LANES), F32)],
        compiler_params=_params(("arbitrary",)),
        name="rwkv_scan",
    )(r, r, kkn, kkn, v, v, lw, lw, bb, bb, kd, kd, _cum_matrices())

    return pl.pallas_call(
        functools.partial(_rk_out_body, alpha=alpha, head=head),
        grid=(t_out // tm,),
        in_specs=[tile, scan_tile, scan_tile, tile, tile, mod_spec, _full((1, d)), _full((1, d)),
                  _resident((d, d)), _full((d, LANES)), _full((LANES, d)), _full((1, d)), _full((1, d))],
        out_specs=tile,
        out_shape=jax.ShapeDtypeStruct((t_out, d), F32),
        compiler_params=_params(("parallel",)),
        name="rwkv_out",
    )(x, yf, yb, bonus, gate, mods, lnx_g.reshape(1, d), lnx_b.reshape(1, d), w_o.astype(BF16), e, et,
      ln_g.reshape(1, d), ln_b.reshape(1, d))


HALO = 8


def _gdn_feat_body(x_ref, xp_ref, xn_ref, mod_ref, wq_ref, wz_ref, wab_ref, cw_ref, alog_ref, dtb_ref,
                   q_out, k_out, v_out, z_out, gb_out, proj_scr,
                   *, n_lat_tiles, tiles_per_batch, head, n_taps):
    i = pl.program_id(0)
    tm, d = x_ref.shape
    sh, sc = mod_ref[0:1, :], mod_ref[1:2, :]
    h = x_ref[...] * (1.0 + sc) + sh
    is_ctx = i >= n_lat_tiles
    j = i % tiles_per_batch
    has_prev = jnp.logical_and(jnp.logical_not(is_ctx), j > 0)
    has_next = jnp.logical_and(jnp.logical_not(is_ctx), j < tiles_per_batch - 1)
    hp = jnp.where(has_prev, xp_ref[...] * (1.0 + sc) + sh, 0.0)
    hn = jnp.where(has_next, xn_ref[...] * (1.0 + sc) + sh, 0.0)
    hb = h.astype(BF16)
    proj_scr[...] = _dot(jnp.concatenate([hp, h, hn], axis=0), wq_ref[...])
    left = n_taps // 2
    conv = jnp.zeros((tm, 3 * d), F32)
    for tap in range(n_taps):
        conv = conv + proj_scr[pl.ds(HALO + tap - left, tm), :] * cw_ref[tap:tap + 1, :]
    qkv = _silu(conv)
    n_heads = d // head
    for hd in range(n_heads):
        qs = qkv[:, hd * head:(hd + 1) * head]
        ks = qkv[:, d + hd * head:d + (hd + 1) * head]
        q_out[:, hd * head:(hd + 1) * head] = (
            qs * lax.rsqrt(jnp.sum(qs * qs, axis=-1, keepdims=True) + GDN_L2_EPS) * head ** -0.5)
        k_out[:, hd * head:(hd + 1) * head] = (
            ks * lax.rsqrt(jnp.sum(ks * ks, axis=-1, keepdims=True) + GDN_L2_EPS))
    v_out[...] = qkv[:, 2 * d:]
    z_out[...] = jnp.dot(hb, wz_ref[...], preferred_element_type=F32)
    ab = _dot3(h, wab_ref[...])
    lane = lax.broadcasted_iota(jnp.int32, ab.shape, 1)
    g = -jnp.exp(alog_ref[...]) * _softplus(ab + dtb_ref[...])
    gb_out[...] = jnp.where(lane < 2 * n_heads, g, _sigmoid(ab))


def _gdn_scan_chunks(streams, s_scr, n_heads):
    c = CHUNK
    zero = jnp.float32(0.0)
    inv_masks = _inverse_masks(2 * c)
    row = lax.broadcasted_iota(jnp.int32, (2 * c, 1), 0)
    top_rows = row < c

    chains = []
    for direction, b, (q_ref, k_ref, v_ref, gb_ref, y_ref) in streams:
        head = q_ref.shape[-1] // n_heads
        strict, incl, top, left = _order_masks(direction)
        same = top == left
        masks = dict(strict=same & strict, incl=same & incl,
                     cum=jnp.where(same & incl, 1.0, 0.0).astype(BF16),
                     nxt=jnp.where(same & strict, 1.0, 0.0))
        first, last = (0, c - 1) if direction == 0 else (c - 1, 0)
        gb = gb_ref[b]
        col = lambda idx: gb[:, idx:idx + 1]
        for p in range(n_heads // 2):
            h0, h1 = 2 * p, 2 * p + 1
            stack = lambda ref: jnp.concatenate([ref[b, :, h0 * head:(h0 + 1) * head],
                                                 ref[b, :, h1 * head:(h1 + 1) * head]], 0)
            chains.append(dict(
                g=jnp.concatenate([col(direction * n_heads + h0), col(direction * n_heads + h1)], 0),
                beta=jnp.concatenate([col((2 + direction) * n_heads + h0),
                                      col((2 + direction) * n_heads + h1)], 0),
                q=stack(q_ref), k=stack(k_ref), v=stack(v_ref),
                m=masks, first=first, last=last, y_ref=y_ref, b=b, head=head,
                where=((direction, b, h0), (direction, b, h1)),
                lanes=(slice(h0 * head, (h0 + 1) * head), slice(h1 * head, (h1 + 1) * head))))

    dlog = [_dot_rhs2(ch["m"]["cum"], ch["g"] * ch["m"]["nxt"]) for ch in chains]
    cs, cs_l0, cs_l1 = [], [], []
    for ch, dl in zip(chains, dlog):
        f, l, g = ch["first"], ch["last"], ch["g"]
        x = jnp.where(top_rows, dl[:, f:f + 1] + g[f:f + 1], dl[:, c + f:c + f + 1] + g[c + f:c + f + 1])
        cs.append(x)
        cs_l0.append(x[l:l + 1])
        cs_l1.append(x[c + l:c + l + 1])
    cs_end = [jnp.where(top_rows, x0, x1) for x0, x1 in zip(cs_l0, cs_l1)]
    decay = [jnp.exp(jnp.where(ch["m"]["incl"], x, -jnp.inf)) for ch, x in zip(chains, dlog)]
    kb = [ch["k"] * ch["beta"] for ch in chains]
    vb = [ch["v"] * ch["beta"] for ch in chains]
    kkt = [_dot_nt(x, ch["k"]) for x, ch in zip(kb, chains)]
    qkt = [_dot_nt(ch["q"], ch["k"]) for ch in chains]
    lower = [jnp.where(ch["m"]["strict"], x * dc, zero) for ch, x, dc in zip(chains, kkt, decay)]
    a_qk = [jnp.where(ch["m"]["incl"], x * dc, zero) for ch, x, dc in zip(chains, qkt, decay)]
    t_inv = _unit_tri_inverse(lower, inv_masks)
    eg = [jnp.exp(x) for x in cs]
    uw = [_dot(t, jnp.concatenate([x, y * e], axis=1)) for t, x, y, e in zip(t_inv, vb, kb, eg)]
    s_old = [(s_scr[ch["where"][0]], s_scr[ch["where"][1]]) for ch in chains]
    ws = [jnp.concatenate([_dot(x[:c, ch["head"]:], s0), _dot(x[c:, ch["head"]:], s1)], 0)
          for ch, x, (s0, s1) in zip(chains, uw, s_old)]
    qs = [jnp.concatenate([_dot((ch["q"] * e)[:c], s0), _dot((ch["q"] * e)[c:], s1)], 0)
          for ch, e, (s0, s1) in zip(chains, eg, s_old)]
    v_new = [x[:, :ch["head"]] - y for ch, x, y in zip(chains, uw, ws)]
    o = [x + _dot(a, y) for x, a, y in zip(qs, a_qk, v_new)]
    k_dec = [ch["k"] * jnp.exp(ce - y) for ch, ce, y in zip(chains, cs_end, cs)]
    for i, ch in enumerate(chains):
        ch["y_ref"][ch["b"], :, ch["lanes"][0]] = o[i][:c]
        ch["y_ref"][ch["b"], :, ch["lanes"][1]] = o[i][c:]
        s_scr[ch["where"][0]] = s_old[i][0] * jnp.exp(cs_l0[i]) + _dot_tn(k_dec[i][:c], v_new[i][:c])
        s_scr[ch["where"][1]] = s_old[i][1] * jnp.exp(cs_l1[i]) + _dot_tn(k_dec[i][c:], v_new[i][c:])


def _gdn_scan_body(qf, qb, kf, kb, vf, vb, gf, gb, yf, yb, s_scr, *, n_heads, batches_per_trip):
    @pl.when(pl.program_id(0) == 0)
    def _():
        s_scr[...] = jnp.zeros(s_scr.shape, F32)

    fwd = (qf, kf, vf, gf, yf)
    bwd = (qb, kb, vb, gb, yb)

    def trip(i, carry):
        streams = []
        for j in range(batches_per_trip):
            b = i * batches_per_trip + j
            streams += [(0, b, fwd), (1, b, bwd)]
        _gdn_scan_chunks(streams, s_scr, n_heads)
        return carry

    lax.fori_loop(0, qf.shape[0] // batches_per_trip, trip, 0)


def _gdn_out_body(x_ref, of_ref, ob_ref, z_ref, mod_ref, nw_ref, wo_ref, lng_ref, lnb_ref, out_ref,
                  *, alpha, head):
    x = x_ref[...]
    o = of_ref[...] + ob_ref[...]
    z = z_ref[...]
    d = x.shape[1]
    parts = []
    for hd in range(d // head):
        oh = o[:, hd * head:(hd + 1) * head]
        on = oh * lax.rsqrt(jnp.mean(oh * oh, axis=-1, keepdims=True) + GDN_NORM_EPS) * nw_ref[...]
        parts.append(on * _silu(z[:, hd * head:(hd + 1) * head]))
    y = _dot(jnp.concatenate(parts, axis=-1), wo_ref[...])
    gt = mod_ref[2:3, :]
    out_ref[...] = _layer_norm(alpha * x + gt * y, lng_ref[...], lnb_ref[...])


def _gdn_layer(geom, x, mods, w_in, conv_w, a_log, dt_bias, norm_w, w_o, ln_g, ln_b, alpha, tm=256):
    t, d = x.shape
    n_heads = a_log.shape[-1]
    head = d // n_heads
    n_taps = conv_w.shape[0]
    assert head == LANES and geom.ctx == tm and geom.seq % tm == 0 and n_heads % 2 == 0
    assert 4 * n_heads <= LANES
    w_qkv = w_in[:, :3 * d].astype(BF16)
    w_z = w_in[:, 3 * d:4 * d].astype(BF16)
    w_ab = jnp.zeros((d, LANES), F32).at[:, :4 * n_heads].set(w_in[:, 4 * d:])
    alog = jnp.zeros((1, LANES), F32).at[0, :2 * n_heads].set(a_log.reshape(-1))
    dtb = jnp.zeros((1, LANES), F32).at[0, :2 * n_heads].set(dt_bias.reshape(-1))

    nb = t // HALO
    nt = geom.ctx + geom.seq
    per = tm // HALO
    tile = pl.BlockSpec((tm, d), lambda i: (i, 0))
    scan_tile = pl.BlockSpec((None, tm, d), lambda i: (*_scan_tile_index(geom, i, tm), 0))
    small = pl.BlockSpec((None, tm, LANES), lambda i: (*_scan_tile_index(geom, i, tm), 0))
    mod_spec = pl.BlockSpec((None, 6, d), lambda i: (geom.mod_row(i, tm), 0, 0))
    q, k, v, z, gb = pl.pallas_call(
        functools.partial(_gdn_feat_body, n_lat_tiles=geom.n_lat // tm,
                          tiles_per_batch=geom.seq // tm, head=head, n_taps=n_taps),
        grid=(t // tm,),
        in_specs=[tile,
                  pl.BlockSpec((HALO, d), lambda i: (jnp.maximum(i * per - 1, 0), 0)),
                  pl.BlockSpec((HALO, d), lambda i: (jnp.minimum(i * per + per, nb - 1), 0)),
                  mod_spec, _resident((d, 3 * d)), _resident((d, d)), _full((d, LANES)),
                  _full((n_taps, 3 * d)), _full((1, LANES)), _full((1, LANES))],
        out_specs=[scan_tile, scan_tile, scan_tile, tile, small],
        out_shape=([jax.ShapeDtypeStruct((geom.b, nt, d), F32)] * 3 + [jax.ShapeDtypeStruct((t, d), F32)]
                   + [jax.ShapeDtypeStruct((geom.b, nt, LANES), F32)]),
        scratch_shapes=[pltpu.VMEM((tm + 2 * HALO, 3 * d), F32)],
        compiler_params=_params(("parallel",)),
        name="gdn_feat",
    )(x, x, x, mods, w_qkv, w_z, w_ab, conv_w, alog, dtb)

    n_steps = nt // CHUNK
    shared = [pl.BlockSpec((geom.b, CHUNK, d), lambda s, zz=zz: (0, _scan_step_chunk(geom, zz, s), 0))
              for zz in range(2)]
    shared_small = [pl.BlockSpec((geom.b, CHUNK, LANES), lambda s, zz=zz: (0, _scan_step_chunk(geom, zz, s), 0))
                    for zz in range(2)]
    of, ob = pl.pallas_call(
        functools.partial(_gdn_scan_body, n_heads=n_heads, batches_per_trip=2),
        grid=(n_steps,),
        in_specs=shared * 3 + shared_small,
        out_specs=shared,
        out_shape=[jax.ShapeDtypeStruct((geom.b, nt, d), F32)] * 2,
        scratch_shapes=[pltpu.VMEM((2, geom.b, n_heads, head, head), F32)],
        compiler_params=_params(("arbitrary",)),
        name="gdn_scan",
    )(q, q, k, k, v, v, gb, gb)

    return pl.pallas_call(
        functools.partial(_gdn_out_body, alpha=alpha, head=head),
        grid=(t // tm,),
        in_specs=[tile, scan_tile, scan_tile, tile, mod_spec, _full((1, head)), _resident((d, d)),
                  _full((1, d)), _full((1, d))],
        out_specs=tile,
        out_shape=jax.ShapeDtypeStruct((t, d), F32),
        compiler_params=_params(("parallel",)),
        name="gdn_out",
    )(x, of, ob, z, mods, norm_w.reshape(1, head), w_o.astype(BF16), ln_g.reshape(1, d), ln_b.reshape(1, d))


def kernel(x, c, ctx, c_ctx, ada_w, ada_b, ln_g, ln_b, rk_mu, rk_w_rkv, rk_w0, rk_w1, rk_w2, rk_a0, rk_a1, rk_a2, rk_g1, rk_g2, rk_k_k, rk_k_a, rk_r_k, rk_lnx_g, rk_lnx_b, rk_w_o, pool_w, pool_scale, gdn_w_in, gdn_conv_w, gdn_a_log, gdn_dt_bias, gdn_norm_w, gdn_w_o, ffn_w1, ffn_w3, ffn_w2, moe_router_w, moe_router_b, moe_w1, moe_w3, moe_w2):
    batch, seq, d = x.shape
    ctx_len = ctx.shape[1]
    depth = ada_w.shape[0]
    geom = _Geom(batch, seq, ctx_len, d)
    alpha = (2 * depth) ** 0.25
    n_mixers = 3

    rows = -(-(batch + 1) // 8) * 8
    cvec = jnp.zeros((rows, d), F32).at[:batch].set(c).at[batch].set(c_ctx)
    table = _ada_table(cvec, ada_w, ada_b)
    mods_all = table[:, :batch + 1].reshape(depth, batch + 1, 6, d)

    xs = jnp.concatenate([x.reshape(batch * seq, d), ctx.reshape(batch * ctx_len, d)], axis=0)
    for i in range(depth):
        mods = mods_all[i]
        kind, j = i % n_mixers, i // n_mixers
        if kind == 0:
            xs = _rwkv_layer(geom, xs, mods, rk_mu[j], rk_w_rkv[j], rk_w0[j], rk_w1[j], rk_w2[j],
                             rk_a0[j], rk_a1[j], rk_a2[j], rk_g1[j], rk_g2[j], rk_k_k[j], rk_k_a[j],
                             rk_r_k[j], rk_lnx_g[j], rk_lnx_b[j], rk_w_o[j], ln_g[i, 0], ln_b[i, 0], alpha,
                             want_ctx=i < depth - 1)
        elif kind == 1:
            xs = _pool_layer(geom, xs, mods, pool_w[j], pool_scale[j], ln_g[i, 0], ln_b[i, 0], alpha)
        else:
            xs = _gdn_layer(geom, xs, mods, gdn_w_in[j], gdn_conv_w[j], gdn_a_log[j], gdn_dt_bias[j],
                            gdn_norm_w[j], gdn_w_o[j], ln_g[i, 0], ln_b[i, 0], alpha)
        e = i // 2
        if i % 2 == 0:
            xs = _ffn_layer(geom, xs, mods, ffn_w1[e], ffn_w3[e], ffn_w2[e], ln_g[i, 1], ln_b[i, 1], alpha)
        else:
            xs = _moe_layer(geom, xs, mods, moe_router_w[e], moe_router_b[e], moe_w1[e], moe_w3[e],
                            moe_w2[e], ln_g[i, 1], ln_b[i, 1], alpha)
    return xs[:batch * seq].reshape(batch, seq, d)
```

```python
import functools
import math

import jax
import jax.numpy as jnp
import numpy as np
from jax import lax
from jax.experimental import pallas as pl
from jax.experimental.pallas import tpu as pltpu

F32 = jnp.float32
BF16 = jnp.bfloat16

GRID_W = 64
CHUNK = 64
LANES = 128
POOL_WINDOWS = (2, 4, 8, 16)
LN_EPS = 1e-5
RK_DECAY_SCALE = math.exp(-0.5)
RK_GN_EPS = 64e-5
GDN_NORM_EPS = 1e-6
GDN_L2_EPS = 1e-6
TOP_K = 2
VMEM_LIMIT = 56 * 1024 * 1024


def _sigmoid(x):
    return 1.0 / (1.0 + jnp.exp(-x))


def _silu(x):
    return x * _sigmoid(x)


def _softplus(x):
    return jnp.maximum(x, 0.0) + jnp.log(1.0 + jnp.exp(-jnp.abs(x)))


def _dot(a, b):
    return jnp.dot(a.astype(BF16), b.astype(BF16), preferred_element_type=F32)


def _dot_nt(a, b):
    return lax.dot_general(a.astype(BF16), b.astype(BF16), (((1,), (1,)), ((), ())),
                           preferred_element_type=F32)


def _dot_tn(a, b):
    return lax.dot_general(a.astype(BF16), b.astype(BF16), (((0,), (0,)), ((), ())),
                           preferred_element_type=F32)


def _split(x):
    hi = x.astype(BF16)
    lo = (x - hi.astype(F32)).astype(BF16)
    return hi, lo


def _dot_lhs2(a, b_exact):
    hi, lo = _split(a)
    return (jnp.dot(hi, b_exact, preferred_element_type=F32)
            + jnp.dot(lo, b_exact, preferred_element_type=F32))


def _dot_rhs2(a_exact, b):
    hi, lo = _split(b)
    return (jnp.dot(a_exact, hi, preferred_element_type=F32)
            + jnp.dot(a_exact, lo, preferred_element_type=F32))


def _dot3(a, b):
    ah, al = _split(a)
    bh, bl = _split(b)
    return (jnp.dot(ah, bh, preferred_element_type=F32)
            + jnp.dot(ah, bl, preferred_element_type=F32)
            + jnp.dot(al, bh, preferred_element_type=F32))


def _segsum(x, e_ref, et_ref):
    s = _dot_lhs2(x, e_ref[...])
    return _dot_lhs2(s, et_ref[...])


def _layer_norm(v, g, b):
    mean = jnp.mean(v, axis=-1, keepdims=True)
    c = v - mean
    var = jnp.mean(c * c, axis=-1, keepdims=True)
    return c * lax.rsqrt(var + LN_EPS) * g + b


def _head_indicator(d_model, head):
    n_heads = d_model // head
    e = np.zeros((d_model, LANES), np.float32)
    e[np.arange(d_model), np.arange(d_model) // head] = 1.0
    assert n_heads <= LANES
    return jnp.asarray(e, BF16), jnp.asarray(e.T.copy(), BF16)


def _full(shape):
    n = len(shape)
    return pl.BlockSpec(shape, lambda *_: (0,) * n)


def _resident(shape):
    n = len(shape)
    return pl.BlockSpec(shape, lambda *_: (0,) * n, pipeline_mode=pl.Buffered(1))


def _params(sem):
    return pltpu.CompilerParams(dimension_semantics=sem, vmem_limit_bytes=VMEM_LIMIT)


class _Geom:
    def __init__(self, batch, seq, ctx_len, d_model):
        self.b, self.seq, self.ctx, self.d = batch, seq, ctx_len, d_model
        self.n_lat = batch * seq
        self.t = batch * (seq + ctx_len)

    def mod_row(self, i, tm):
        return jnp.where(i < self.n_lat // tm, i // (self.seq // tm), self.b)


def _ada_body(c_ref, w_ref, b_ref, o_ref):
    o_ref[...] = _dot3(_silu(c_ref[...]), w_ref[...]) + b_ref[...]


def _ada_table(cvec, ada_w, ada_b):
    depth, d, d6 = ada_w.shape
    rows = cvec.shape[0]
    return pl.pallas_call(
        _ada_body,
        grid=(depth, d6 // d),
        in_specs=[_full((rows, d)),
                  pl.BlockSpec((None, d, d), lambda l, j: (l, 0, j)),
                  pl.BlockSpec((None, 1, d), lambda l, j: (l, 0, j))],
        out_specs=pl.BlockSpec((None, rows, d), lambda l, j: (l, 0, j)),
        out_shape=jax.ShapeDtypeStruct((depth, rows, d6), F32),
        compiler_params=_params(("parallel", "parallel")),
        name="ada_table",
    )(cvec, ada_w, ada_b.reshape(depth, 1, d6))


def _ffn_body(x_ref, mod_ref, w1_ref, w3_ref, w2_ref, lng_ref, lnb_ref, o_ref, *, alpha, n_split):
    x = x_ref[...]
    sh, sc, gt = mod_ref[3:4, :], mod_ref[4:5, :], mod_ref[5:6, :]
    h = (x * (1.0 + sc) + sh).astype(BF16)
    ff = w1_ref.shape[1] // n_split
    acc = jnp.zeros(x.shape, F32)
    for s in range(n_split):
        a = jnp.dot(h, w1_ref[:, s * ff:(s + 1) * ff], preferred_element_type=F32)
        g = jnp.dot(h, w3_ref[:, s * ff:(s + 1) * ff], preferred_element_type=F32)
        acc = acc + jnp.dot((_silu(a) * g).astype(BF16), w2_ref[s * ff:(s + 1) * ff, :],
                            preferred_element_type=F32)
    o_ref[...] = _layer_norm(alpha * x + gt * acc, lng_ref[...], lnb_ref[...])


def _ffn_layer(geom, x, mods, w1, w3, w2, ln_g, ln_b, alpha, tm=512):
    t, d = x.shape
    ff = w1.shape[1]
    return pl.pallas_call(
        functools.partial(_ffn_body, alpha=alpha, n_split=2),
        grid=(t // tm,),
        in_specs=[pl.BlockSpec((tm, d), lambda i: (i, 0)),
                  pl.BlockSpec((None, 6, d), lambda i: (geom.mod_row(i, tm), 0, 0)),
                  _resident((d, ff)), _resident((d, ff)), _resident((ff, d)),
                  _full((1, d)), _full((1, d))],
        out_specs=pl.BlockSpec((tm, d), lambda i: (i, 0)),
        out_shape=jax.ShapeDtypeStruct((t, d), F32),
        compiler_params=_params(("parallel",)),
        name="ffn",
    )(x, mods, w1.astype(BF16), w3.astype(BF16), w2.astype(BF16),
      ln_g.reshape(1, d), ln_b.reshape(1, d))


MOE_GROUP = 512
MOE_ROWS = 160
MOE_TAIL_ROWS = 64


def _moe_body(x_ref, mod_ref, rw_ref, rb_ref, tri_ref, w1_ref, w3_ref, w2_ref, lng_ref, lnb_ref, o_ref,
              h_scr, gate_scr, sel_scr, rank_scr, selt_scr, rankt_scr, acc_scr, *, alpha, n_experts):
    e = pl.program_id(1)
    tm = x_ref.shape[0]
    n_groups = tm // MOE_GROUP

    @pl.when(e == 0)
    def _():
        x = x_ref[...]
        sh, sc = mod_ref[3:4, :], mod_ref[4:5, :]
        h = x * (1.0 + sc) + sh
        h_scr[...] = h.astype(BF16)
        logits = _dot3(h, rw_ref[...]) + rb_ref[...]
        lane = lax.broadcasted_iota(jnp.int32, logits.shape, 1)
        neg = jnp.float32(-jnp.inf)
        logits = jnp.where(lane < n_experts, logits, neg)
        m1 = jnp.max(logits, axis=-1, keepdims=True)
        i1 = jnp.min(jnp.where(logits == m1, lane, LANES), axis=-1, keepdims=True)
        rest = jnp.where(lane == i1, neg, logits)
        m2 = jnp.max(rest, axis=-1, keepdims=True)
        i2 = jnp.min(jnp.where(rest == m2, lane, LANES), axis=-1, keepdims=True)
        e2 = jnp.exp(m2 - m1)
        p1 = 1.0 / (1.0 + e2)
        p2 = e2 / (1.0 + e2)
        gate_scr[...] = jnp.where(lane == i1, p1, 0.0) + jnp.where(lane == i2, p2, 0.0)
        sel = jnp.where(lane == i1, 1.0, 0.0) + jnp.where(lane == i2, 1.0, 0.0)
        sel_scr[...] = sel
        for s in range(n_groups):
            rows = slice(s * MOE_GROUP, (s + 1) * MOE_GROUP)
            rank = jnp.dot(tri_ref[...], sel[rows].astype(BF16), preferred_element_type=F32)
            rank_scr[rows, :] = rank
            rankt_scr[s] = rank.T
            selt_scr[s] = sel[rows].T
        acc_scr[...] = jnp.zeros(acc_scr.shape, F32)

    lane = lax.broadcasted_iota(jnp.int32, (MOE_GROUP, LANES), 1)
    column = lambda ref, rows: jnp.sum(jnp.where(lane == e, ref[rows, :], 0.0), axis=-1, keepdims=True)
    groups = []
    most = jnp.int32(0)
    for s in range(n_groups):
        rows = slice(s * MOE_GROUP, (s + 1) * MOE_GROUP)
        sel_col = column(sel_scr, rows) > 0.5
        groups.append(dict(rows=rows, gate_col=column(gate_scr, rows), sel_col=sel_col,
                           rank_col=column(rank_scr, rows),
                           sel_row=selt_scr[s, pl.ds(e, 1), :] > 0.5,
                           rank_row=rankt_scr[s, pl.ds(e, 1), :]))
        most = jnp.maximum(most, jnp.sum(jnp.where(sel_col, 1.0, 0.0)).astype(jnp.int32))

    def run_pass(first_slot, n_rows):
        base = first_slot.astype(F32)
        slot_col = lax.broadcasted_iota(jnp.int32, (n_rows, 1), 0).astype(F32) + base
        slot_row = lax.broadcasted_iota(jnp.int32, (1, n_rows), 1).astype(F32) + base
        xg = []
        for gr in groups:
            take = jnp.where((gr["rank_row"] == slot_col) & gr["sel_row"], 1.0, 0.0).astype(BF16)
            xg.append(jnp.dot(take, h_scr[gr["rows"], :], preferred_element_type=F32).astype(BF16))
        xg = jnp.concatenate(xg, axis=0)
        a = jnp.dot(xg, w1_ref[...], preferred_element_type=F32)
        g = jnp.dot(xg, w3_ref[...], preferred_element_type=F32)
        y = jnp.dot((_silu(a) * g).astype(BF16), w2_ref[...], preferred_element_type=F32)
        for k, gr in enumerate(groups):
            put = jnp.where((gr["rank_col"] == slot_row) & gr["sel_col"], 1.0, 0.0).astype(BF16)
            acc_scr[gr["rows"], :] += gr["gate_col"] * _dot_rhs2(put, y[k * n_rows:(k + 1) * n_rows])

    n_full = most // MOE_ROWS
    left = most - n_full * MOE_ROWS

    def full_pass(j, carry):
        run_pass(j * MOE_ROWS, MOE_ROWS)
        return carry

    lax.fori_loop(0, n_full, full_pass, 0)

    @pl.when(left > MOE_TAIL_ROWS)
    def _():
        run_pass(n_full * MOE_ROWS, MOE_ROWS)

    @pl.when(jnp.logical_and(left > 0, left <= MOE_TAIL_ROWS))
    def _():
        run_pass(n_full * MOE_ROWS, MOE_TAIL_ROWS)

    @pl.when(e == n_experts - 1)
    def _():
        x = x_ref[...]
        gt = mod_ref[5:6, :]
        o_ref[...] = _layer_norm(alpha * x + gt * acc_scr[...], lng_ref[...], lnb_ref[...])


def _moe_layer(geom, x, mods, router_w, router_b, w1, w3, w2, ln_g, ln_b, alpha, tm=1024):
    t, d = x.shape
    n_e, _, ffe = w1.shape
    assert tm % MOE_GROUP == 0 and n_e <= LANES
    rw = jnp.zeros((d, LANES), F32).at[:, :n_e].set(router_w)
    rb = jnp.zeros((1, LANES), F32).at[0, :n_e].set(router_b)
    idx = np.arange(MOE_GROUP)
    tri = jnp.asarray(idx[None, :] < idx[:, None], BF16)
    n_groups = tm // MOE_GROUP
    return pl.pallas_call(
        functools.partial(_moe_body, alpha=alpha, n_experts=n_e),
        grid=(t // tm, n_e),
        in_specs=[pl.BlockSpec((tm, d), lambda i, e: (i, 0)),
                  pl.BlockSpec((None, 6, d), lambda i, e: (geom.mod_row(i, tm), 0, 0)),
                  _full((d, LANES)), _full((1, LANES)), _full((MOE_GROUP, MOE_GROUP)),
                  pl.BlockSpec((None, d, ffe), lambda i, e: (e, 0, 0)),
                  pl.BlockSpec((None, d, ffe), lambda i, e: (e, 0, 0)),
                  pl.BlockSpec((None, ffe, d), lambda i, e: (e, 0, 0)),
                  _full((1, d)), _full((1, d))],
        out_specs=pl.BlockSpec((tm, d), lambda i, e: (i, 0)),
        out_shape=jax.ShapeDtypeStruct((t, d), F32),
        scratch_shapes=[pltpu.VMEM((tm, d), BF16), pltpu.VMEM((tm, LANES), F32),
                        pltpu.VMEM((tm, LANES), F32), pltpu.VMEM((tm, LANES), F32),
                        pltpu.VMEM((n_groups, LANES, MOE_GROUP), F32),
                        pltpu.VMEM((n_groups, LANES, MOE_GROUP), F32),
                        pltpu.VMEM((tm, d), F32)],
        compiler_params=_params(("parallel", "arbitrary")),
        name="moe",
    )(x, mods, rw, rb, tri, w1.astype(BF16), w3.astype(BF16), w2.astype(BF16),
      ln_g.reshape(1, d), ln_b.reshape(1, d))


def _pool_matrices(tm, seq_len):
    t = np.arange(tm)
    pos = t % seq_len
    mats, inv = [], []
    for win in POOL_WINDOWS:
        lo = np.clip(pos - win // 2, 0, seq_len)
        hi = np.clip(pos + win // 2, 0, seq_len)
        base = t - pos
        j = t[None, :]
        mats.append(((j >= (base + lo)[:, None]) & (j < (base + hi)[:, None])).astype(np.float32))
        inv.append((hi - lo).astype(np.float32))
    return np.stack(mats), np.stack(inv)


def _pool_body(x_ref, mod_ref, pm_ref, cnt_ref, pw_ref, ps_ref, lng_ref, lnb_ref, o_ref, *, alpha):
    x = x_ref[...]
    sh, sc, gt = mod_ref[0:1, :], mod_ref[1:2, :], mod_ref[2:3, :]
    h = x * (1.0 + sc) + sh
    n_g = pm_ref.shape[0]
    gw = x.shape[1] // n_g
    outs = []
    for g in range(n_g):
        hg = h[:, g * gw:(g + 1) * gw]
        total = _dot_rhs2(pm_ref[g], hg)
        pooled = total / cnt_ref[g] - hg
        outs.append(_dot(pooled, pw_ref[g]))
    y = jnp.concatenate(outs, axis=-1) * ps_ref[...]
    o_ref[...] = _layer_norm(alpha * x + gt * y, lng_ref[...], lnb_ref[...])


def _pool_layer(geom, x, mods, pool_w, pool_scale, ln_g, ln_b, alpha, tm=256):
    t, d = x.shape
    n_g = len(POOL_WINDOWS)
    m_lat, c_lat = _pool_matrices(tm, GRID_W)
    m_ctx, c_ctx = _pool_matrices(tm, geom.ctx)
    pm = jnp.asarray(np.stack([m_lat, m_ctx]), BF16)
    cnt = jnp.asarray(np.stack([c_lat, c_ctx])[..., None], F32)
    n_lat_tiles = geom.n_lat // tm
    kind = lambda i: jnp.where(i < n_lat_tiles, 0, 1)
    return pl.pallas_call(
        functools.partial(_pool_body, alpha=alpha),
        grid=(t // tm,),
        in_specs=[pl.BlockSpec((tm, d), lambda i: (i, 0)),
                  pl.BlockSpec((None, 6, d), lambda i: (geom.mod_row(i, tm), 0, 0)),
                  pl.BlockSpec((None, n_g, tm, tm), lambda i: (kind(i), 0, 0, 0)),
                  pl.BlockSpec((None, n_g, tm, 1), lambda i: (kind(i), 0, 0, 0)),
                  _full((n_g, d // n_g, d // n_g)), _full((1, d)), _full((1, d)), _full((1, d))],
        out_specs=pl.BlockSpec((tm, d), lambda i: (i, 0)),
        out_shape=jax.ShapeDtypeStruct((t, d), F32),
        compiler_params=_params(("parallel",)),
        name="pool",
    )(x, mods, pm, cnt, pool_w.astype(BF16), pool_scale.reshape(1, d),
      ln_g.reshape(1, d), ln_b.reshape(1, d))


def _scan_tile_index(geom, i, tm):
    n_lat_tiles = geom.n_lat // tm
    tpb = geom.seq // tm
    is_lat = i < n_lat_tiles
    return jnp.where(is_lat, i // tpb, i - n_lat_tiles), jnp.where(is_lat, geom.ctx // tm + i % tpb, 0)


def _scan_step_chunk(geom, direction, s):
    if direction == 0:
        return s
    nc_ctx = geom.ctx // CHUNK
    nc = (geom.ctx + geom.seq) // CHUNK
    return jnp.where(s < nc_ctx, nc_ctx - 1 - s, nc - 1 + nc_ctx - s)


def _order_masks(direction):
    n = 2 * CHUNK
    ri = lax.broadcasted_iota(jnp.int32, (n, n), 0)
    ci = lax.broadcasted_iota(jnp.int32, (n, n), 1)
    rt, ct = ri & (CHUNK - 1), ci & (CHUNK - 1)
    ahead = rt - ct if direction == 0 else ct - rt
    return ahead > 0, ahead >= 0, ri < CHUNK, ci < CHUNK


INV_BASE = 8


def _inverse_masks(n):
    ri = lax.broadcasted_iota(jnp.int32, (n, n), 0)
    ci = lax.broadcasted_iota(jnp.int32, (n, n), 1)
    eye = (ri == ci).astype(F32)
    same = lambda size: (ri // size) == (ci // size)
    base = same(INV_BASE)
    levels = []
    size = INV_BASE
    while size < CHUNK:
        levels.append(same(2 * size) & ~same(size))
        size *= 2
    return eye, base, levels


def _unit_tri_inverse(mats, masks):
    eye, base, levels = masks
    ps = [jnp.where(base, -a, 0.0) for a in mats]
    ts = [eye + p for p in ps]
    for _ in range(int(math.log2(INV_BASE)) - 1):
        ps = [_dot(p, p) for p in ps]
        ts = [t + _dot(t, p) for t, p in zip(ts, ps)]
    for off in levels:
        cs = [_dot(jnp.where(off, a, 0.0), t) for a, t in zip(mats, ts)]
        ts = [t - _dot(t, c) for t, c in zip(ts, cs)]
    return ts


def _cum_matrices():
    i = np.arange(CHUNK)
    fwd = (i[None, :] <= i[:, None]).astype(np.float32)
    return jnp.asarray(np.stack([fwd, fwd.T]), BF16)


def _rk_feat_body(x_ref, xp_ref, xn_ref, mod_ref, mu_ref, wrkv_ref, w1_ref, w2_ref, w0_ref,
                  a1_ref, a2_ref, a0_ref, g1_ref, g2_ref, kk_ref, ka_ref, rk_ref, e_ref, et_ref,
                  r_out, kkn_out, v_out, gate_out, bonus_out, lw_out, b_out, kd_out,
                  *, n_lat_tiles, tiles_per_batch, seq):
    i = pl.program_id(0)
    tm, d = x_ref.shape
    q = d // 4
    sh, sc = mod_ref[0:1, :], mod_ref[1:2, :]
    h = x_ref[...] * (1.0 + sc) + sh
    hp = xp_ref[...] * (1.0 + sc) + sh
    hn = xn_ref[...] * (1.0 + sc) + sh
    is_ctx = i >= n_lat_tiles
    t = lax.broadcasted_iota(jnp.int32, (tm, 1), 0)
    col = t & (GRID_W - 1)
    pos = (i % tiles_per_batch) * tm + t
    to_end = (tm - 1) - t

    def prev_tok(z):
        return pltpu.roll(z, 1, 0)

    def next_tok(z):
        return pltpu.roll(z, tm - 1, 0)

    h0, h1, h2, h3 = (h[:, k * q:(k + 1) * q] for k in range(4))
    up = jnp.concatenate([hp[:, 2 * q:3 * q], h2[:tm - GRID_W]], axis=0)
    down = jnp.concatenate([h3[GRID_W:], hn[:, 3 * q:]], axis=0)
    edge0 = jnp.where(is_ctx, t, col)
    edge1 = jnp.where(is_ctx, t, (GRID_W - 1) - col)
    edge2 = jnp.where(is_ctx, to_end, jnp.maximum(pos - (GRID_W - 1), 0))
    edge3 = jnp.where(is_ctx, to_end, jnp.maximum((seq - GRID_W) - pos, 0))
    s0 = jnp.where(edge0 == 0, 0.0, prev_tok(h0))
    s1 = jnp.where(edge1 == 0, 0.0, jnp.where(is_ctx, prev_tok(h1), next_tok(h1)))
    s2 = jnp.where(edge2 == 0, 0.0, jnp.where(is_ctx, next_tok(h2), up))
    s3 = jnp.where(edge3 == 0, 0.0, jnp.where(is_ctx, next_tok(h3), down))
    xx = jnp.concatenate([s0, s1, s2, s3], axis=-1) - h
    xr, xw, xk, xv, xa, xg = (h + xx * mu_ref[m:m + 1, :] for m in range(6))

    r = _dot(xr, wrkv_ref[0])
    k = _dot(xk, wrkv_ref[1])
    v = _dot(xv, wrkv_ref[2])
    dl = jnp.tanh(_dot(xw, w1_ref[...]))
    al = _dot(xa, a1_ref[...])
    kkr = k * kk_ref[...]
    ss = _segsum(kkr * kkr, e_ref, et_ref)
    kkn = kkr / jnp.maximum(jnp.sqrt(ss), 1e-12)
    kd_sum = jnp.zeros_like(k)
    for z in range(2):
        lw = -RK_DECAY_SCALE * _sigmoid(w0_ref[z:z + 1, :] + _dot(dl, w2_ref[z]))
        a = _sigmoid(a0_ref[z:z + 1, :] + _dot(al, a2_ref[z]))
        kd = k * (1.0 + (a - 1.0) * ka_ref[...])
        lw_out[z] = lw
        b_out[z] = kkn * a
        kd_out[z] = kd
        kd_sum = kd_sum + kd
    coef = _segsum(r * kd_sum * rk_ref[...], e_ref, et_ref)
    r_out[...] = r
    kkn_out[...] = kkn
    v_out[...] = v
    bonus_out[...] = coef * v
    gate_out[...] = _dot(_sigmoid(_dot(xg, g1_ref[...])), g2_ref[...])


def _rk_scan_chunks(streams, cum_ref, s_scr):
    c = CHUNK
    zero = jnp.float32(0.0)
    inv_masks = _inverse_masks(2 * c)
    lane = lax.broadcasted_iota(jnp.int32, (1, LANES), 1)
    m0, m1 = lane < c, lane >= c

    def pick(m_first, m_second, first, second):
        return jnp.where(m_first, first, jnp.where(m_second, second, zero))

    chains = []
    for direction, b, (r_ref, kk_ref, v_ref, lw_ref, b_ref, kd_ref, y_ref) in streams:
        lw = lw_ref[b]
        cs = _dot_rhs2(cum_ref[direction], lw)
        cs_last = cs[c - 1:c, :] if direction == 0 else cs[0:1, :]
        eg, en, el = jnp.exp(cs), jnp.exp(-cs), jnp.exp(cs_last - cs)
        rt = r_ref[b] * eg
        kkt = kk_ref[b] * jnp.exp(cs - lw)
        bb, kd, v = b_ref[b], kd_ref[b], v_ref[b]
        bt, kt, bh, kh = bb * en, kd * en, bb * el, kd * el
        gl = jnp.exp(cs_last)
        strict, incl, top, left = _order_masks(direction)
        tl, br = top & left, (~top) & (~left)
        tr, bl = top & (~left), (~top) & left
        masks = dict(bk=(tl & strict, br & strict), kk=(tr & strict, bl & strict),
                     rb=(bl & incl, tr & incl), rk=(br & incl, tl & incl), diag=top == left)
        for p in range(r_ref.shape[-1] // LANES):
            s = slice(p * LANES, (p + 1) * LANES)
            chains.append(dict(rt=rt[:, s], kkt=kkt[:, s], bt=bt[:, s], kt=kt[:, s], bh=bh[:, s],
                               kh=kh[:, s], v=v[:, s], gl=gl[:, s], m=masks, y_ref=y_ref,
                               where=(direction, b, p), lanes=s))

    r0 = [_dot_nt(jnp.concatenate([jnp.where(m0, ch["kkt"], zero), jnp.where(m0, ch["rt"], zero)], 0),
                  jnp.concatenate([ch["bt"], ch["kt"]], 0)) for ch in chains]
    r1 = [_dot_nt(jnp.concatenate([jnp.where(m1, ch["rt"], zero), jnp.where(m1, ch["kkt"], zero)], 0),
                  jnp.concatenate([ch["kt"], ch["bt"]], 0)) for ch in chains]
    s_old = [s_scr[ch["where"]] for ch in chains]
    proj = [_dot_nt(jnp.concatenate([ch["kkt"], ch["rt"]], 0), st) for ch, st in zip(chains, s_old)]
    a_bk = [pick(*ch["m"]["bk"], x0, x1) for ch, x0, x1 in zip(chains, r0, r1)]
    a_kk = [pick(*ch["m"]["kk"], x0, x1) for ch, x0, x1 in zip(chains, r0, r1)]
    a_rbk = [jnp.concatenate([pick(*ch["m"]["rb"], x0, x1), pick(*ch["m"]["rk"], x0, x1)], 1)
             for ch, x0, x1 in zip(chains, r0, r1)]
    v_sw = [jnp.concatenate([jnp.where(m1, ch["v"], zero), jnp.where(m0, ch["v"], zero)], 0) for ch in chains]
    akkv = [_dot(a, x) for a, x in zip(a_kk, v_sw)]
    t_inv = _unit_tri_inverse(a_bk, inv_masks)
    ks_sm = [jnp.concatenate([jnp.where(m0, pj[:c], zero), jnp.where(m1, pj[:c], zero)], 0) for pj in proj]
    u_sm = [-_dot(t, k_ + a) for t, k_, a in zip(t_inv, ks_sm, akkv)]
    y_sm = [_dot(a, jnp.concatenate([u, x], 0)) for a, u, x in zip(a_rbk, u_sm, v_sw)]
    upd = [_dot_tn(jnp.concatenate([u[:c] + u[c:], ch["v"]], 0), jnp.concatenate([ch["bh"], ch["kh"]], 0))
           for ch, u in zip(chains, u_sm)]
    for i, ch in enumerate(chains):
        _, b, _ = ch["where"]
        ch["y_ref"][b, :, ch["lanes"]] = proj[i][c:] + y_sm[i][:c] + y_sm[i][c:]
        s_scr[ch["where"]] = s_old[i] * ch["gl"] + jnp.where(ch["m"]["diag"], upd[i], zero)


def _rk_scan_body(rf, rb, kkf, kkb, vf, vb, lwf, lwb, bf, bb, kdf, kdb, cum_ref, yf, yb, s_scr,
                  *, batches_per_trip):
    @pl.when(pl.program_id(0) == 0)
    def _():
        s_scr[...] = jnp.zeros(s_scr.shape, F32)

    fwd = (rf, kkf, vf, lwf, bf, kdf, yf)
    bwd = (rb, kkb, vb, lwb, bb, kdb, yb)

    def trip(i, carry):
        streams = []
        for k in range(batches_per_trip):
            b = i * batches_per_trip + k
            streams += [(0, b, fwd), (1, b, bwd)]
        _rk_scan_chunks(streams, cum_ref, s_scr)
        return carry

    lax.fori_loop(0, rf.shape[0] // batches_per_trip, trip, 0)


def _rk_out_body(x_ref, yf_ref, yb_ref, bonus_ref, gate_ref, mod_ref, lxg_ref, lxb_ref, wo_ref, e_ref,
                 et_ref, lng_ref, lnb_ref, o_ref, *, alpha, head):
    x = x_ref[...]
    y = yf_ref[...] + yb_ref[...]
    mean = _segsum(y, e_ref, et_ref) * (1.0 / head)
    cen = y - mean
    var = _segsum(cen * cen, e_ref, et_ref) * (1.0 / head)
    yn = cen * lax.rsqrt(var + RK_GN_EPS) * lxg_ref[...] + lxb_ref[...]
    out = _dot((yn + bonus_ref[...]) * gate_ref[...], wo_ref[...])
    gt = mod_ref[2:3, :]
    o_ref[...] = _layer_norm(alpha * x + gt * out, lng_ref[...], lnb_ref[...])


def _rwkv_layer(geom, x, mods, mu, w_rkv, w0, w1, w2, a0, a1, a2, g1, g2, k_k, k_a, r_k,
                lnx_g, lnx_b, w_o, ln_g, ln_b, alpha, want_ctx=True, tm=256):
    t, d = x.shape
    t_out = t if want_ctx else geom.n_lat
    head = r_k.shape[-1]
    assert 2 * head == LANES and geom.ctx == tm and geom.seq % tm == 0
    lora_d, lora_a, lora_g = w1.shape[-1], a1.shape[-1], g1.shape[-1]
    e, et = _head_indicator(d, head)

    def cat_dirs(w):
        return jnp.concatenate([w[0], w[1]], axis=-1).astype(BF16)

    def pad_dirs(w):
        z = jnp.zeros_like(w[0])
        return jnp.stack([jnp.concatenate([w[0], z], 0), jnp.concatenate([z, w[1]], 0)]).astype(BF16)

    nb = t // GRID_W
    nt = geom.ctx + geom.seq
    row = lambda i: (i, 0)
    tile = pl.BlockSpec((tm, d), row)
    scan_tile = pl.BlockSpec((None, tm, d), lambda i: (*_scan_tile_index(geom, i, tm), 0))
    scan_tile2 = pl.BlockSpec((2, None, tm, d), lambda i: (0, *_scan_tile_index(geom, i, tm), 0))
    mod_spec = pl.BlockSpec((None, 6, d), lambda i: (geom.mod_row(i, tm), 0, 0))
    per = tm // GRID_W
    feat = pl.pallas_call(
        functools.partial(_rk_feat_body, n_lat_tiles=geom.n_lat // tm,
                          tiles_per_batch=geom.seq // tm, seq=geom.seq),
        grid=(t // tm,),
        in_specs=[tile,
                  pl.BlockSpec((GRID_W, d), lambda i: (jnp.maximum(i * per - 1, 0), 0)),
                  pl.BlockSpec((GRID_W, d), lambda i: (jnp.minimum(i * per + per, nb - 1), 0)),
                  mod_spec, _full((6, d)), _resident((3, d, d)),
                  _full((d, 2 * lora_d)), _full((2, 2 * lora_d, d)), _full((2, d)),
                  _full((d, 2 * lora_a)), _full((2, 2 * lora_a, d)), _full((2, d)),
                  _full((d, lora_g)), _full((lora_g, d)),
                  _full((1, d)), _full((1, d)), _full((1, d)), _full((d, LANES)), _full((LANES, d))],
        out_specs=[scan_tile, scan_tile, scan_tile, tile, tile, scan_tile2, scan_tile2, scan_tile2],
        out_shape=([jax.ShapeDtypeStruct((geom.b, nt, d), F32)] * 3 + [jax.ShapeDtypeStruct((t, d), F32)] * 2
                   + [jax.ShapeDtypeStruct((2, geom.b, nt, d), F32)] * 3),
        compiler_params=_params(("parallel",)),
        name="rwkv_feat",
    )
    r, kkn, v, gate, bonus, lw, bb, kd = feat(
        x, x, x, mods, mu, w_rkv.astype(BF16), cat_dirs(w1), pad_dirs(w2), w0,
        cat_dirs(a1), pad_dirs(a2), a0, g1.astype(BF16), g2.astype(BF16),
        k_k.reshape(1, d), k_a.reshape(1, d), r_k.reshape(1, d), e, et)

    n_steps = nt // CHUNK
    shared = [pl.BlockSpec((geom.b, CHUNK, d), lambda s, z=z: (0, _scan_step_chunk(geom, z, s), 0))
              for z in range(2)]
    per_dir = [pl.BlockSpec((None, geom.b, CHUNK, d), lambda s, z=z: (z, 0, _scan_step_chunk(geom, z, s), 0))
               for z in range(2)]
    yf, yb = pl.pallas_call(
        functools.partial(_rk_scan_body, batches_per_trip=1),
        grid=(n_steps,),
        in_specs=shared * 3 + per_dir * 3 + [_full((2, CHUNK, CHUNK))],
        out_specs=shared,
        out_shape=[jax.ShapeDtypeStruct((geom.b, nt, d), F32)] * 2,
        scratch_shapes=[pltpu.VMEM((2, geom.b, d // LANES, LANES, LANES), F32)],
        compiler_params=_params(("arbitrary",)),
        name="rwkv_scan",
    )(r, r, kkn, kkn, v, v, lw, lw, bb, bb, kd, kd, _cum_matrices())

    return pl.pallas_call(
        functools.partial(_rk_out_body, alpha=alpha, head=head),
        grid=(t_out // tm,),
        in_specs=[tile, scan_tile, scan_tile, tile, tile, mod_spec, _full((1, d)), _full((1, d)),
                  _resident((d, d)), _full((d, LANES)), _full((LANES, d)), _full((1, d)), _full((1, d))],
        out_specs=tile,
        out_shape=jax.ShapeDtypeStruct((t_out, d), F32),
        compiler_params=_params(("parallel",)),
        name="rwkv_out",
    )(x, yf, yb, bonus, gate, mods, lnx_g.reshape(1, d), lnx_b.reshape(1, d), w_o.astype(BF16), e, et,
      ln_g.reshape(1, d), ln_b.reshape(1, d))


HALO = 8


def _gdn_feat_body(x_ref, xp_ref, xn_ref, mod_ref, wq_ref, wz_ref, wab_ref, cw_ref, alog_ref, dtb_ref,
                   q_out, k_out, v_out, z_out, gb_out, proj_scr,
                   *, n_lat_tiles, tiles_per_batch, head, n_taps):
    i = pl.program_id(0)
    tm, d = x_ref.shape
    sh, sc = mod_ref[0:1, :], mod_ref[1:2, :]
    h = x_ref[...] * (1.0 + sc) + sh
    is_ctx = i >= n_lat_tiles
    j = i % tiles_per_batch
    has_prev = jnp.logical_and(jnp.logical_not(is_ctx), j > 0)
    has_next = jnp.logical_and(jnp.logical_not(is_ctx), j < tiles_per_batch - 1)
    hp = jnp.where(has_prev, xp_ref[...] * (1.0 + sc) + sh, 0.0)
    hn = jnp.where(has_next, xn_ref[...] * (1.0 + sc) + sh, 0.0)
    hb = h.astype(BF16)
    proj_scr[...] = _dot(jnp.concatenate([hp, h, hn], axis=0), wq_ref[...])
    left = n_taps // 2
    conv = jnp.zeros((tm, 3 * d), F32)
    for tap in range(n_taps):
        conv = conv + proj_scr[pl.ds(HALO + tap - left, tm), :] * cw_ref[tap:tap + 1, :]
    qkv = _silu(conv)
    n_heads = d // head
    for hd in range(n_heads):
        qs = qkv[:, hd * head:(hd + 1) * head]
        ks = qkv[:, d + hd * head:d + (hd + 1) * head]
        q_out[:, hd * head:(hd + 1) * head] = (
            qs * lax.rsqrt(jnp.sum(qs * qs, axis=-1, keepdims=True) + GDN_L2_EPS) * head ** -0.5)
        k_out[:, hd * head:(hd + 1) * head] = (
            ks * lax.rsqrt(jnp.sum(ks * ks, axis=-1, keepdims=True) + GDN_L2_EPS))
    v_out[...] = qkv[:, 2 * d:]
    z_out[...] = jnp.dot(hb, wz_ref[...], preferred_element_type=F32)
    ab = _dot3(h, wab_ref[...])
    lane = lax.broadcasted_iota(jnp.int32, ab.shape, 1)
    g = -jnp.exp(alog_ref[...]) * _softplus(ab + dtb_ref[...])
    gb_out[...] = jnp.where(lane < 2 * n_heads, g, _sigmoid(ab))


def _gdn_scan_chunks(streams, s_scr, n_heads):
    c = CHUNK
    zero = jnp.float32(0.0)
    inv_masks = _inverse_masks(2 * c)
    row = lax.broadcasted_iota(jnp.int32, (2 * c, 1), 0)
    top_rows = row < c

    chains = []
    for direction, b, (q_ref, k_ref, v_ref, gb_ref, y_ref) in streams:
        head = q_ref.shape[-1] // n_heads
        strict, incl, top, left = _order_masks(direction)
        same = top == left
        masks = dict(strict=same & strict, incl=same & incl,
                     cum=jnp.where(same & incl, 1.0, 0.0).astype(BF16),
                     nxt=jnp.where(same & strict, 1.0, 0.0))
        first, last = (0, c - 1) if direction == 0 else (c - 1, 0)
        gb = gb_ref[b]
        col = lambda idx: gb[:, idx:idx + 1]
        for p in range(n_heads // 2):
            h0, h1 = 2 * p, 2 * p + 1
            stack = lambda ref: jnp.concatenate([ref[b, :, h0 * head:(h0 + 1) * head],
                                                 ref[b, :, h1 * head:(h1 + 1) * head]], 0)
            chains.append(dict(
                g=jnp.concatenate([col(direction * n_heads + h0), col(direction * n_heads + h1)], 0),
                beta=jnp.concatenate([col((2 + direction) * n_heads + h0),
                                      col((2 + direction) * n_heads + h1)], 0),
                q=stack(q_ref), k=stack(k_ref), v=stack(v_ref),
                m=masks, first=first, last=last, y_ref=y_ref, b=b, head=head,
                where=((direction, b, h0), (direction, b, h1)),
                lanes=(slice(h0 * head, (h0 + 1) * head), slice(h1 * head, (h1 + 1) * head))))

    dlog = [_dot_rhs2(ch["m"]["cum"], ch["g"] * ch["m"]["nxt"]) for ch in chains]
    cs, cs_l0, cs_l1 = [], [], []
    for ch, dl in zip(chains, dlog):
        f, l, g = ch["first"], ch["last"], ch["g"]
        x = jnp.where(top_rows, dl[:, f:f + 1] + g[f:f + 1], dl[:, c + f:c + f + 1] + g[c + f:c + f + 1])
        cs.append(x)
        cs_l0.append(x[l:l + 1])
        cs_l1.append(x[c + l:c + l + 1])
    cs_end = [jnp.where(top_rows, x0, x1) for x0, x1 in zip(cs_l0, cs_l1)]
    decay = [jnp.exp(jnp.where(ch["m"]["incl"], x, -jnp.inf)) for ch, x in zip(chains, dlog)]
    kb = [ch["k"] * ch["beta"] for ch in chains]
    vb = [ch["v"] * ch["beta"] for ch in chains]
    kkt = [_dot_nt(x, ch["k"]) for x, ch in zip(kb, chains)]
    qkt = [_dot_nt(ch["q"], ch["k"]) for ch in chains]
    lower = [jnp.where(ch["m"]["strict"], x * dc, zero) for ch, x, dc in zip(chains, kkt, decay)]
    a_qk = [jnp.where(ch["m"]["incl"], x * dc, zero) for ch, x, dc in zip(chains, qkt, decay)]
    t_inv = _unit_tri_inverse(lower, inv_masks)
    eg = [jnp.exp(x) for x in cs]
    uw = [_dot(t, jnp.concatenate([x, y * e], axis=1)) for t, x, y, e in zip(t_inv, vb, kb, eg)]
    s_old = [(s_scr[ch["where"][0]], s_scr[ch["where"][1]]) for ch in chains]
    ws = [jnp.concatenate([_dot(x[:c, ch["head"]:], s0), _dot(x[c:, ch["head"]:], s1)], 0)
          for ch, x, (s0, s1) in zip(chains, uw, s_old)]
    qs = [jnp.concatenate([_dot((ch["q"] * e)[:c], s0), _dot((ch["q"] * e)[c:], s1)], 0)
          for ch, e, (s0, s1) in zip(chains, eg, s_old)]
    v_new = [x[:, :ch["head"]] - y for ch, x, y in zip(chains, uw, ws)]
    o = [x + _dot(a, y) for x, a, y in zip(qs, a_qk, v_new)]
    k_dec = [ch["k"] * jnp.exp(ce - y) for ch, ce, y in zip(chains, cs_end, cs)]
    for i, ch in enumerate(chains):
        ch["y_ref"][ch["b"], :, ch["lanes"][0]] = o[i][:c]
        ch["y_ref"][ch["b"], :, ch["lanes"][1]] = o[i][c:]
        s_scr[ch["where"][0]] = s_old[i][0] * jnp.exp(cs_l0[i]) + _dot_tn(k_dec[i][:c], v_new[i][:c])
        s_scr[ch["where"][1]] = s_old[i][1] * jnp.exp(cs_l1[i]) + _dot_tn(k_dec[i][c:], v_new[i][c:])


def _gdn_scan_body(qf, qb, kf, kb, vf, vb, gf, gb, yf, yb, s_scr, *, n_heads, batches_per_trip):
    @pl.when(pl.program_id(0) == 0)
    def _():
        s_scr[...] = jnp.zeros(s_scr.shape, F32)

    fwd = (qf, kf, vf, gf, yf)
    bwd = (qb, kb, vb, gb, yb)

    def trip(i, carry):
        streams = []
        for j in range(batches_per_trip):
            b = i * batches_per_trip + j
            streams += [(0, b, fwd), (1, b, bwd)]
        _gdn_scan_chunks(streams, s_scr, n_heads)
        return carry

    lax.fori_loop(0, qf.shape[0] // batches_per_trip, trip, 0)


def _gdn_out_body(x_ref, of_ref, ob_ref, z_ref, mod_ref, nw_ref, wo_ref, lng_ref, lnb_ref, out_ref,
                  *, alpha, head):
    x = x_ref[...]
    o = of_ref[...] + ob_ref[...]
    z = z_ref[...]
    d = x.shape[1]
    parts = []
    for hd in range(d // head):
        oh = o[:, hd * head:(hd + 1) * head]
        on = oh * lax.rsqrt(jnp.mean(oh * oh, axis=-1, keepdims=True) + GDN_NORM_EPS) * nw_ref[...]
        parts.append(on * _silu(z[:, hd * head:(hd + 1) * head]))
    y = _dot(jnp.concatenate(parts, axis=-1), wo_ref[...])
    gt = mod_ref[2:3, :]
    out_ref[...] = _layer_norm(alpha * x + gt * y, lng_ref[...], lnb_ref[...])


def _gdn_layer(geom, x, mods, w_in, conv_w, a_log, dt_bias, norm_w, w_o, ln_g, ln_b, alpha, tm=256):
    t, d = x.shape
    n_heads = a_log.shape[-1]
    head = d // n_heads
    n_taps = conv_w.shape[0]
    assert head == LANES and geom.ctx == tm and geom.seq % tm == 0 and n_heads % 2 == 0
    assert 4 * n_heads <= LANES
    w_qkv = w_in[:, :3 * d].astype(BF16)
    w_z = w_in[:, 3 * d:4 * d].astype(BF16)
    w_ab = jnp.zeros((d, LANES), F32).at[:, :4 * n_heads].set(w_in[:, 4 * d:])
    alog = jnp.zeros((1, LANES), F32).at[0, :2 * n_heads].set(a_log.reshape(-1))
    dtb = jnp.zeros((1, LANES), F32).at[0, :2 * n_heads].set(dt_bias.reshape(-1))

    nb = t // HALO
    nt = geom.ctx + geom.seq
    per = tm // HALO
    tile = pl.BlockSpec((tm, d), lambda i: (i, 0))
    scan_tile = pl.BlockSpec((None, tm, d), lambda i: (*_scan_tile_index(geom, i, tm), 0))
    small = pl.BlockSpec((None, tm, LANES), lambda i: (*_scan_tile_index(geom, i, tm), 0))
    mod_spec = pl.BlockSpec((None, 6, d), lambda i: (geom.mod_row(i, tm), 0, 0))
    q, k, v, z, gb = pl.pallas_call(
        functools.partial(_gdn_feat_body, n_lat_tiles=geom.n_lat // tm,
                          tiles_per_batch=geom.seq // tm, head=head, n_taps=n_taps),
        grid=(t // tm,),
        in_specs=[tile,
                  pl.BlockSpec((HALO, d), lambda i: (jnp.maximum(i * per - 1, 0), 0)),
                  pl.BlockSpec((HALO, d), lambda i: (jnp.minimum(i * per + per, nb - 1), 0)),
                  mod_spec, _resident((d, 3 * d)), _resident((d, d)), _full((d, LANES)),
                  _full((n_taps, 3 * d)), _full((1, LANES)), _full((1, LANES))],
        out_specs=[scan_tile, scan_tile, scan_tile, tile, small],
        out_shape=([jax.ShapeDtypeStruct((geom.b, nt, d), F32)] * 3 + [jax.ShapeDtypeStruct((t, d), F32)]
                   + [jax.ShapeDtypeStruct((geom.b, nt, LANES), F32)]),
        scratch_shapes=[pltpu.VMEM((tm + 2 * HALO, 3 * d), F32)],
        compiler_params=_params(("parallel",)),
        name="gdn_feat",
    )(x, x, x, mods, w_qkv, w_z, w_ab, conv_w, alog, dtb)

    n_steps = nt // CHUNK
    shared = [pl.BlockSpec((geom.b, CHUNK, d), lambda s, zz=zz: (0, _scan_step_chunk(geom, zz, s), 0))
              for zz in range(2)]
    shared_small = [pl.BlockSpec((geom.b, CHUNK, LANES), lambda s, zz=zz: (0, _scan_step_chunk(geom, zz, s), 0))
                    for zz in range(2)]
    of, ob = pl.pallas_call(
        functools.partial(_gdn_scan_body, n_heads=n_heads, batches_per_trip=2),
        grid=(n_steps,),
        in_specs=shared * 3 + shared_small,
        out_specs=shared,
        out_shape=[jax.ShapeDtypeStruct((geom.b, nt, d), F32)] * 2,
        scratch_shapes=[pltpu.VMEM((2, geom.b, n_heads, head, head), F32)],
        compiler_params=_params(("arbitrary",)),
        name="gdn_scan",
    )(q, q, k, k, v, v, gb, gb)

    return pl.pallas_call(
        functools.partial(_gdn_out_body, alpha=alpha, head=head),
        grid=(t // tm,),
        in_specs=[tile, scan_tile, scan_tile, tile, mod_spec, _full((1, head)), _resident((d, d)),
                  _full((1, d)), _full((1, d))],
        out_specs=tile,
        out_shape=jax.ShapeDtypeStruct((t, d), F32),
        compiler_params=_params(("parallel",)),
        name="gdn_out",
    )(x, of, ob, z, mods, norm_w.reshape(1, head), w_o.astype(BF16), ln_g.reshape(1, d), ln_b.reshape(1, d))


def kernel(x, c, ctx, c_ctx, ada_w, ada_b, ln_g, ln_b, rk_mu, rk_w_rkv, rk_w0, rk_w1, rk_w2, rk_a0, rk_a1, rk_a2, rk_g1, rk_g2, rk_k_k, rk_k_a, rk_r_k, rk_lnx_g, rk_lnx_b, rk_w_o, pool_w, pool_scale, gdn_w_in, gdn_conv_w, gdn_a_log, gdn_dt_bias, gdn_norm_w, gdn_w_o, ffn_w1, ffn_w3, ffn_w2, moe_router_w, moe_router_b, moe_w1, moe_w3, moe_w2):
    batch, seq, d = x.shape
    ctx_len = ctx.shape[1]
    depth = ada_w.shape[0]
    geom = _Geom(batch, seq, ctx_len, d)
    alpha = (2 * depth) ** 0.25
    n_mixers = 3

    rows = -(-(batch + 1) // 8) * 8
    cvec = jnp.zeros((rows, d), F32).at[:batch].set(c).at[batch].set(c_ctx)
    table = _ada_table(cvec, ada_w, ada_b)
    mods_all = table[:, :batch + 1].reshape(depth, batch + 1, 6, d)

    xs = jnp.concatenate([x.reshape(batch * seq, d), ctx.reshape(batch * ctx_len, d)], axis=0)
    for i in range(depth):
        mods = mods_all[i]
        kind, j = i % n_mixers, i // n_mixers
        if kind == 0:
            xs = _rwkv_layer(geom, xs, mods, rk_mu[j], rk_w_rkv[j], rk_w0[j], rk_w1[j], rk_w2[j],
                             rk_a0[j], rk_a1[j], rk_a2[j], rk_g1[j], rk_g2[j], rk_k_k[j], rk_k_a[j],
                             rk_r_k[j], rk_lnx_g[j], rk_lnx_b[j], rk_w_o[j], ln_g[i, 0], ln_b[i, 0], alpha,
                             want_ctx=i < depth - 1)
        elif kind == 1:
            xs = _pool_layer(geom, xs, mods, pool_w[j], pool_scale[j], ln_g[i, 0], ln_b[i, 0], alpha)
        else:
            xs = _gdn_layer(geom, xs, mods, gdn_w_in[j], gdn_conv_w[j], gdn_a_log[j], gdn_dt_bias[j],
                            gdn_norm_w[j], gdn_w_o[j], ln_g[i, 0], ln_b[i, 0], alpha)
        e = i // 2
        if i % 2 == 0:
            xs = _ffn_layer(geom, xs, mods, ffn_w1[e], ffn_w3[e], ffn_w2[e], ln_g[i, 1], ln_b[i, 1], alpha)
        else:
            xs = _moe_layer(geom, xs, mods, moe_router_w[e], moe_router_b[e], moe_w1[e], moe_w3[e],
                            moe_w2[e], ln_g[i, 1], ln_b[i, 1], alpha)
    return xs[:batch * seq].reshape(batch, seq, d)
```

```python
import functools
import math

import jax
import jax.numpy as jnp
import numpy as np
from jax import lax
from jax.experimental import pallas as pl
from jax.experimental.pallas import tpu as pltpu

F32 = jnp.float32
BF16 = jnp.bfloat16

GRID_W = 64
CHUNK = 64
LANES = 128
POOL_WINDOWS = (2, 4, 8, 16)
LN_EPS = 1e-5
RK_DECAY_SCALE = math.exp(-0.5)
RK_GN_EPS = 64e-5
GDN_NORM_EPS = 1e-6
GDN_L2_EPS = 1e-6
TOP_K = 2
VMEM_LIMIT = 56 * 1024 * 1024


def _sigmoid(x):
    return 1.0 / (1.0 + jnp.exp(-x))


def _silu(x):
    return x * _sigmoid(x)


def _softplus(x):
    return jnp.maximum(x, 0.0) + jnp.log(1.0 + jnp.exp(-jnp.abs(x)))


def _dot(a, b):
    return jnp.dot(a.astype(BF16), b.astype(BF16), preferred_element_type=F32)


def _dot_nt(a, b):
    return lax.dot_general(a.astype(BF16), b.astype(BF16), (((1,), (1,)), ((), ())),
                           preferred_element_type=F32)


def _dot_tn(a, b):
    return lax.dot_general(a.astype(BF16), b.astype(BF16), (((0,), (0,)), ((), ())),
                           preferred_element_type=F32)


def _split(x):
    hi = x.astype(BF16)
    lo = (x - hi.astype(F32)).astype(BF16)
    return hi, lo


def _dot_lhs2(a, b_exact):
    hi, lo = _split(a)
    return (jnp.dot(hi, b_exact, preferred_element_type=F32)
            + jnp.dot(lo, b_exact, preferred_element_type=F32))


def _dot_rhs2(a_exact, b):
    hi, lo = _split(b)
    return (jnp.dot(a_exact, hi, preferred_element_type=F32)
            + jnp.dot(a_exact, lo, preferred_element_type=F32))


def _dot3(a, b):
    ah, al = _split(a)
    bh, bl = _split(b)
    return (jnp.dot(ah, bh, preferred_element_type=F32)
            + jnp.dot(ah, bl, preferred_element_type=F32)
            + jnp.dot(al, bh, preferred_element_type=F32))


def _segsum(x, e_ref, et_ref, split=True):
    if not split:
        return _dot(_dot(x, e_ref[...]), et_ref[...])
    s = _dot_lhs2(x, e_ref[...])
    return _dot_lhs2(s, et_ref[...])


def _layer_norm(v, g, b):
    mean = jnp.mean(v, axis=-1, keepdims=True)
    c = v - mean
    var = jnp.mean(c * c, axis=-1, keepdims=True)
    return c * lax.rsqrt(var + LN_EPS) * g + b


def _head_indicator(d_model, head):
    n_heads = d_model // head
    e = np.zeros((d_model, LANES), np.float32)
    e[np.arange(d_model), np.arange(d_model) // head] = 1.0
    assert n_heads <= LANES
    return jnp.asarray(e, BF16), jnp.asarray(e.T.copy(), BF16)


def _full(shape):
    n = len(shape)
    return pl.BlockSpec(shape, lambda *_: (0,) * n)


def _resident(shape):
    n = len(shape)
    return pl.BlockSpec(shape, lambda *_: (0,) * n, pipeline_mode=pl.Buffered(1))


def _params(sem):
    return pltpu.CompilerParams(dimension_semantics=sem, vmem_limit_bytes=VMEM_LIMIT)


class _Geom:
    def __init__(self, batch, seq, ctx_len, d_model):
        self.b, self.seq, self.ctx, self.d = batch, seq, ctx_len, d_model
        self.n_lat = batch * seq
        self.t = batch * (seq + ctx_len)

    def mod_row(self, i, tm):
        return jnp.where(i < self.n_lat // tm, i // (self.seq // tm), self.b)


def _ada_body(c_ref, w_ref, b_ref, o_ref):
    o_ref[...] = _dot3(_silu(c_ref[...]), w_ref[...]) + b_ref[...]


def _ada_table(cvec, ada_w, ada_b):
    depth, d, d6 = ada_w.shape
    rows = cvec.shape[0]
    return pl.pallas_call(
        _ada_body,
        grid=(depth, d6 // d),
        in_specs=[_full((rows, d)),
                  pl.BlockSpec((None, d, d), lambda l, j: (l, 0, j)),
                  pl.BlockSpec((None, 1, d), lambda l, j: (l, 0, j))],
        out_specs=pl.BlockSpec((None, rows, d), lambda l, j: (l, 0, j)),
        out_shape=jax.ShapeDtypeStruct((depth, rows, d6), F32),
        compiler_params=_params(("parallel", "parallel")),
        name="ada_table",
    )(cvec, ada_w, ada_b.reshape(depth, 1, d6))


def _ffn_body(x_ref, mod_ref, w1_ref, w3_ref, w2_ref, lng_ref, lnb_ref, o_ref, *, alpha, n_split):
    x = x_ref[...]
    sh, sc, gt = mod_ref[3:4, :], mod_ref[4:5, :], mod_ref[5:6, :]
    h = (x * (1.0 + sc) + sh).astype(BF16)
    ff = w1_ref.shape[1] // n_split
    acc = jnp.zeros(x.shape, F32)
    for s in range(n_split):
        a = jnp.dot(h, w1_ref[:, s * ff:(s + 1) * ff], preferred_element_type=F32)
        g = jnp.dot(h, w3_ref[:, s * ff:(s + 1) * ff], preferred_element_type=F32)
        acc = acc + jnp.dot((_silu(a) * g).astype(BF16), w2_ref[s * ff:(s + 1) * ff, :],
                            preferred_element_type=F32)
    o_ref[...] = _layer_norm(alpha * x + gt * acc, lng_ref[...], lnb_ref[...])


def _ffn_layer(geom, x, mods, w1, w3, w2, ln_g, ln_b, alpha, tm=512):
    t, d = x.shape
    ff = w1.shape[1]
    return pl.pallas_call(
        functools.partial(_ffn_body, alpha=alpha, n_split=2),
        grid=(t // tm,),
        in_specs=[pl.BlockSpec((tm, d), lambda i: (i, 0)),
                  pl.BlockSpec((None, 6, d), lambda i: (geom.mod_row(i, tm), 0, 0)),
                  _resident((d, ff)), _resident((d, ff)), _resident((ff, d)),
                  _full((1, d)), _full((1, d))],
        out_specs=pl.BlockSpec((tm, d), lambda i: (i, 0)),
        out_shape=jax.ShapeDtypeStruct((t, d), F32),
        compiler_params=_params(("parallel",)),
        name="ffn",
    )(x, mods, w1.astype(BF16), w3.astype(BF16), w2.astype(BF16),
      ln_g.reshape(1, d), ln_b.reshape(1, d))


MOE_GROUP = 512
MOE_ROWS = 160
MOE_TAIL_ROWS = 64


def _moe_body(x_ref, mod_ref, rw_ref, rb_ref, tri_ref, w1_ref, w3_ref, w2_ref, lng_ref, lnb_ref, o_ref,
              h_scr, gate_scr, sel_scr, rank_scr, selt_scr, rankt_scr, acc_scr, *, alpha, n_experts):
    e = pl.program_id(1)
    tm = x_ref.shape[0]
    n_groups = tm // MOE_GROUP

    @pl.when(e == 0)
    def _():
        x = x_ref[...]
        sh, sc = mod_ref[3:4, :], mod_ref[4:5, :]
        h = x * (1.0 + sc) + sh
        h_scr[...] = h.astype(BF16)
        logits = _dot3(h, rw_ref[...]) + rb_ref[...]
        lane = lax.broadcasted_iota(jnp.int32, logits.shape, 1)
        neg = jnp.float32(-jnp.inf)
        logits = jnp.where(lane < n_experts, logits, neg)
        m1 = jnp.max(logits, axis=-1, keepdims=True)
        i1 = jnp.min(jnp.where(logits == m1, lane, LANES), axis=-1, keepdims=True)
        rest = jnp.where(lane == i1, neg, logits)
        m2 = jnp.max(rest, axis=-1, keepdims=True)
        i2 = jnp.min(jnp.where(rest == m2, lane, LANES), axis=-1, keepdims=True)
        e2 = jnp.exp(m2 - m1)
        p1 = 1.0 / (1.0 + e2)
        p2 = e2 / (1.0 + e2)
        gate_scr[...] = jnp.where(lane == i1, p1, 0.0) + jnp.where(lane == i2, p2, 0.0)
        sel = jnp.where(lane == i1, 1.0, 0.0) + jnp.where(lane == i2, 1.0, 0.0)
        sel_scr[...] = sel
        for s in range(n_groups):
            rows = slice(s * MOE_GROUP, (s + 1) * MOE_GROUP)
            rank = jnp.dot(tri_ref[...], sel[rows].astype(BF16), preferred_element_type=F32)
            rank_scr[rows, :] = rank
            rankt_scr[s] = rank.T
            selt_scr[s] = sel[rows].T
        acc_scr[...] = jnp.zeros(acc_scr.shape, F32)

    lane = lax.broadcasted_iota(jnp.int32, (MOE_GROUP, LANES), 1)
    column = lambda ref, rows: jnp.sum(jnp.where(lane == e, ref[rows, :], 0.0), axis=-1, keepdims=True)
    groups = []
    most = jnp.int32(0)
    for s in range(n_groups):
        rows = slice(s * MOE_GROUP, (s + 1) * MOE_GROUP)
        sel_col = column(sel_scr, rows) > 0.5
        groups.append(dict(rows=rows, gate_col=column(gate_scr, rows), sel_col=sel_col,
                           rank_col=column(rank_scr, rows),
                           sel_row=selt_scr[s, pl.ds(e, 1), :] > 0.5,
                           rank_row=rankt_scr[s, pl.ds(e, 1), :]))
        most = jnp.maximum(most, jnp.sum(jnp.where(sel_col, 1.0, 0.0)).astype(jnp.int32))

    def run_pass(first_slot, n_rows):
        base = first_slot.astype(F32)
        slot_col = lax.broadcasted_iota(jnp.int32, (n_rows, 1), 0).astype(F32) + base
        slot_row = lax.broadcasted_iota(jnp.int32, (1, n_rows), 1).astype(F32) + base
        xg = []
        for gr in groups:
            take = jnp.where((gr["rank_row"] == slot_col) & gr["sel_row"], 1.0, 0.0).astype(BF16)
            xg.append(jnp.dot(take, h_scr[gr["rows"], :], preferred_element_type=F32).astype(BF16))
        xg = jnp.concatenate(xg, axis=0)
        a = jnp.dot(xg, w1_ref[...], preferred_element_type=F32)
        g = jnp.dot(xg, w3_ref[...], preferred_element_type=F32)
        y = jnp.dot((_silu(a) * g).astype(BF16), w2_ref[...], preferred_element_type=F32)
        for k, gr in enumerate(groups):
            put = jnp.where((gr["rank_col"] == slot_row) & gr["sel_col"], 1.0, 0.0).astype(BF16)
            acc_scr[gr["rows"], :] += gr["gate_col"] * _dot(put, y[k * n_rows:(k + 1) * n_rows])

    n_full = most // MOE_ROWS
    left = most - n_full * MOE_ROWS

    def full_pass(j, carry):
        run_pass(j * MOE_ROWS, MOE_ROWS)
        return carry

    lax.fori_loop(0, n_full, full_pass, 0)

    @pl.when(left > MOE_TAIL_ROWS)
    def _():
        run_pass(n_full * MOE_ROWS, MOE_ROWS)

    @pl.when(jnp.logical_and(left > 0, left <= MOE_TAIL_ROWS))
    def _():
        run_pass(n_full * MOE_ROWS, MOE_TAIL_ROWS)

    @pl.when(e == n_experts - 1)
    def _():
        x = x_ref[...]
        gt = mod_ref[5:6, :]
        o_ref[...] = _layer_norm(alpha * x + gt * acc_scr[...], lng_ref[...], lnb_ref[...])


def _moe_layer(geom, x, mods, router_w, router_b, w1, w3, w2, ln_g, ln_b, alpha, tm=1024):
    t, d = x.shape
    n_e, _, ffe = w1.shape
    assert tm % MOE_GROUP == 0 and n_e <= LANES
    rw = jnp.zeros((d, LANES), F32).at[:, :n_e].set(router_w)
    rb = jnp.zeros((1, LANES), F32).at[0, :n_e].set(router_b)
    idx = np.arange(MOE_GROUP)
    tri = jnp.asarray(idx[None, :] < idx[:, None], BF16)
    n_groups = tm // MOE_GROUP
    return pl.pallas_call(
        functools.partial(_moe_body, alpha=alpha, n_experts=n_e),
        grid=(t // tm, n_e),
        in_specs=[pl.BlockSpec((tm, d), lambda i, e: (i, 0)),
                  pl.BlockSpec((None, 6, d), lambda i, e: (geom.mod_row(i, tm), 0, 0)),
                  _full((d, LANES)), _full((1, LANES)), _full((MOE_GROUP, MOE_GROUP)),
                  pl.BlockSpec((None, d, ffe), lambda i, e: (e, 0, 0)),
                  pl.BlockSpec((None, d, ffe), lambda i, e: (e, 0, 0)),
                  pl.BlockSpec((None, ffe, d), lambda i, e: (e, 0, 0)),
                  _full((1, d)), _full((1, d))],
        out_specs=pl.BlockSpec((tm, d), lambda i, e: (i, 0)),
        out_shape=jax.ShapeDtypeStruct((t, d), F32),
        scratch_shapes=[pltpu.VMEM((tm, d), BF16), pltpu.VMEM((tm, LANES), F32),
                        pltpu.VMEM((tm, LANES), F32), pltpu.VMEM((tm, LANES), F32),
                        pltpu.VMEM((n_groups, LANES, MOE_GROUP), F32),
                        pltpu.VMEM((n_groups, LANES, MOE_GROUP), F32),
                        pltpu.VMEM((tm, d), F32)],
        compiler_params=_params(("parallel", "arbitrary")),
        name="moe",
    )(x, mods, rw, rb, tri, w1.astype(BF16), w3.astype(BF16), w2.astype(BF16),
      ln_g.reshape(1, d), ln_b.reshape(1, d))


def _pool_matrices(tm, seq_len):
    t = np.arange(tm)
    pos = t % seq_len
    mats, inv = [], []
    for win in POOL_WINDOWS:
        lo = np.clip(pos - win // 2, 0, seq_len)
        hi = np.clip(pos + win // 2, 0, seq_len)
        base = t - pos
        j = t[None, :]
        mats.append(((j >= (base + lo)[:, None]) & (j < (base + hi)[:, None])).astype(np.float32))
        inv.append((hi - lo).astype(np.float32))
    return np.stack(mats), np.stack(inv)


def _pool_body(x_ref, mod_ref, pm_ref, cnt_ref, pw_ref, ps_ref, lng_ref, lnb_ref, o_ref, *, alpha):
    x = x_ref[...]
    sh, sc, gt = mod_ref[0:1, :], mod_ref[1:2, :], mod_ref[2:3, :]
    h = x * (1.0 + sc) + sh
    n_g = pm_ref.shape[0]
    gw = x.shape[1] // n_g
    outs = []
    for g in range(n_g):
        hg = h[:, g * gw:(g + 1) * gw]
        total = _dot_rhs2(pm_ref[g], hg)
        pooled = total / cnt_ref[g] - hg
        outs.append(_dot(pooled, pw_ref[g]))
    y = jnp.concatenate(outs, axis=-1) * ps_ref[...]
    o_ref[...] = _layer_norm(alpha * x + gt * y, lng_ref[...], lnb_ref[...])


def _pool_layer(geom, x, mods, pool_w, pool_scale, ln_g, ln_b, alpha, tm=256):
    t, d = x.shape
    n_g = len(POOL_WINDOWS)
    m_lat, c_lat = _pool_matrices(tm, GRID_W)
    m_ctx, c_ctx = _pool_matrices(tm, geom.ctx)
    pm = jnp.asarray(np.stack([m_lat, m_ctx]), BF16)
    cnt = jnp.asarray(np.stack([c_lat, c_ctx])[..., None], F32)
    n_lat_tiles = geom.n_lat // tm
    kind = lambda i: jnp.where(i < n_lat_tiles, 0, 1)
    return pl.pallas_call(
        functools.partial(_pool_body, alpha=alpha),
        grid=(t // tm,),
        in_specs=[pl.BlockSpec((tm, d), lambda i: (i, 0)),
                  pl.BlockSpec((None, 6, d), lambda i: (geom.mod_row(i, tm), 0, 0)),
                  pl.BlockSpec((None, n_g, tm, tm), lambda i: (kind(i), 0, 0, 0)),
                  pl.BlockSpec((None, n_g, tm, 1), lambda i: (kind(i), 0, 0, 0)),
                  _full((n_g, d // n_g, d // n_g)), _full((1, d)), _full((1, d)), _full((1, d))],
        out_specs=pl.BlockSpec((tm, d), lambda i: (i, 0)),
        out_shape=jax.ShapeDtypeStruct((t, d), F32),
        compiler_params=_params(("parallel",)),
        name="pool",
    )(x, mods, pm, cnt, pool_w.astype(BF16), pool_scale.reshape(1, d),
      ln_g.reshape(1, d), ln_b.reshape(1, d))


def _scan_tile_index(geom, i, tm):
    n_lat_tiles = geom.n_lat // tm
    tpb = geom.seq // tm
    is_lat = i < n_lat_tiles
    return jnp.where(is_lat, i // tpb, i - n_lat_tiles), jnp.where(is_lat, geom.ctx // tm + i % tpb, 0)


def _scan_step_chunk(geom, direction, s):
    if direction == 0:
        return s
    nc_ctx = geom.ctx // CHUNK
    nc = (geom.ctx + geom.seq) // CHUNK
    return jnp.where(s < nc_ctx, nc_ctx - 1 - s, nc - 1 + nc_ctx - s)


def _order_masks(direction):
    n = 2 * CHUNK
    ri = lax.broadcasted_iota(jnp.int32, (n, n), 0)
    ci = lax.broadcasted_iota(jnp.int32, (n, n), 1)
    rt, ct = ri & (CHUNK - 1), ci & (CHUNK - 1)
    ahead = rt - ct if direction == 0 else ct - rt
    return ahead > 0, ahead >= 0, ri < CHUNK, ci < CHUNK


INV_BASE = 8


def _inverse_masks(n):
    ri = lax.broadcasted_iota(jnp.int32, (n, n), 0)
    ci = lax.broadcasted_iota(jnp.int32, (n, n), 1)
    eye = (ri == ci).astype(F32)
    same = lambda size: (ri // size) == (ci // size)
    base = same(INV_BASE)
    levels = []
    size = INV_BASE
    while size < CHUNK:
        levels.append(same(2 * size) & ~same(size))
        size *= 2
    return eye, base, levels


def _unit_tri_inverse(mats, masks):
    eye, base, levels = masks
    ps = [jnp.where(base, -a, 0.0) for a in mats]
    ts = [eye + p for p in ps]
    for _ in range(int(math.log2(INV_BASE)) - 1):
        ps = [_dot(p, p) for p in ps]
        ts = [t + _dot(t, p) for t, p in zip(ts, ps)]
    for off in levels:
        cs = [_dot(jnp.where(off, a, 0.0), t) for a, t in zip(mats, ts)]
        ts = [t - _dot(t, c) for t, c in zip(ts, cs)]
    return ts


def _cum_matrices():
    i = np.arange(CHUNK)
    fwd = (i[None, :] <= i[:, None]).astype(np.float32)
    return jnp.asarray(np.stack([fwd, fwd.T]), BF16)


def _rk_feat_body(x_ref, xp_ref, xn_ref, mod_ref, mu_ref, wrkv_ref, w1_ref, w2_ref, w0_ref,
                  a1_ref, a2_ref, a0_ref, g1_ref, g2_ref, kk_ref, ka_ref, rk_ref, e_ref, et_ref,
                  r_out, kkn_out, v_out, gate_out, bonus_out, lw_out, b_out, kd_out,
                  *, n_lat_tiles, tiles_per_batch, seq):
    i = pl.program_id(0)
    tm, d = x_ref.shape
    q = d // 4
    sh, sc = mod_ref[0:1, :], mod_ref[1:2, :]
    h = x_ref[...] * (1.0 + sc) + sh
    hp = xp_ref[...] * (1.0 + sc) + sh
    hn = xn_ref[...] * (1.0 + sc) + sh
    is_ctx = i >= n_lat_tiles
    t = lax.broadcasted_iota(jnp.int32, (tm, 1), 0)
    col = t & (GRID_W - 1)
    pos = (i % tiles_per_batch) * tm + t
    to_end = (tm - 1) - t

    def prev_tok(z):
        return pltpu.roll(z, 1, 0)

    def next_tok(z):
        return pltpu.roll(z, tm - 1, 0)

    h0, h1, h2, h3 = (h[:, k * q:(k + 1) * q] for k in range(4))
    up = jnp.concatenate([hp[:, 2 * q:3 * q], h2[:tm - GRID_W]], axis=0)
    down = jnp.concatenate([h3[GRID_W:], hn[:, 3 * q:]], axis=0)
    edge0 = jnp.where(is_ctx, t, col)
    edge1 = jnp.where(is_ctx, t, (GRID_W - 1) - col)
    edge2 = jnp.where(is_ctx, to_end, jnp.maximum(pos - (GRID_W - 1), 0))
    edge3 = jnp.where(is_ctx, to_end, jnp.maximum((seq - GRID_W) - pos, 0))
    s0 = jnp.where(edge0 == 0, 0.0, prev_tok(h0))
    s1 = jnp.where(edge1 == 0, 0.0, jnp.where(is_ctx, prev_tok(h1), next_tok(h1)))
    s2 = jnp.where(edge2 == 0, 0.0, jnp.where(is_ctx, next_tok(h2), up))
    s3 = jnp.where(edge3 == 0, 0.0, jnp.where(is_ctx, next_tok(h3), down))
    xx = jnp.concatenate([s0, s1, s2, s3], axis=-1) - h
    xr, xw, xk, xv, xa, xg = (h + xx * mu_ref[m:m + 1, :] for m in range(6))

    r = _dot(xr, wrkv_ref[0])
    k = _dot(xk, wrkv_ref[1])
    v = _dot(xv, wrkv_ref[2])
    dl = jnp.tanh(_dot(xw, w1_ref[...]))
    al = _dot(xa, a1_ref[...])
    kkr = k * kk_ref[...]
    ss = _segsum(kkr * kkr, e_ref, et_ref, split=False)
    kkn = kkr / jnp.maximum(jnp.sqrt(ss), 1e-12)
    kd_sum = jnp.zeros_like(k)
    for z in range(2):
        lw = -RK_DECAY_SCALE * _sigmoid(w0_ref[z:z + 1, :] + _dot(dl, w2_ref[z]))
        a = _sigmoid(a0_ref[z:z + 1, :] + _dot(al, a2_ref[z]))
        kd = k * (1.0 + (a - 1.0) * ka_ref[...])
        lw_out[z] = lw
        b_out[z] = kkn * a
        kd_out[z] = kd
        kd_sum = kd_sum + kd
    coef = _segsum(r * kd_sum * rk_ref[...], e_ref, et_ref)
    r_out[...] = r
    kkn_out[...] = kkn
    v_out[...] = v
    bonus_out[...] = coef * v
    gate_out[...] = _dot(_sigmoid(_dot(xg, g1_ref[...])), g2_ref[...])


def _rk_scan_chunks(streams, cum_ref, s_scr):
    c = CHUNK
    zero = jnp.float32(0.0)
    inv_masks = _inverse_masks(2 * c)
    lane = lax.broadcasted_iota(jnp.int32, (1, LANES), 1)
    m0, m1 = lane < c, lane >= c

    def pick(m_first, m_second, first, second):
        return jnp.where(m_first, first, jnp.where(m_second, second, zero))

    chains = []
    for direction, b, (r_ref, kk_ref, v_ref, lw_ref, b_ref, kd_ref, y_ref) in streams:
        lw = lw_ref[b]
        cs = _dot_rhs2(cum_ref[direction], lw)
        cs_last = cs[c - 1:c, :] if direction == 0 else cs[0:1, :]
        eg, en, el = jnp.exp(cs), jnp.exp(-cs), jnp.exp(cs_last - cs)
        rt = r_ref[b] * eg
        kkt = kk_ref[b] * jnp.exp(cs - lw)
        bb, kd, v = b_ref[b], kd_ref[b], v_ref[b]
        bt, kt, bh, kh = bb * en, kd * en, bb * el, kd * el
        gl = jnp.exp(cs_last)
        strict, incl, top, left = _order_masks(direction)
        tl, br = top & left, (~top) & (~left)
        tr, bl = top & (~left), (~top) & left
        masks = dict(bk=(tl & strict, br & strict), kk=(tr & strict, bl & strict),
                     rb=(bl & incl, tr & incl), rk=(br & incl, tl & incl), diag=top == left)
        for p in range(r_ref.shape[-1] // LANES):
            s = slice(p * LANES, (p + 1) * LANES)
            chains.append(dict(rt=rt[:, s], kkt=kkt[:, s], bt=bt[:, s], kt=kt[:, s], bh=bh[:, s],
                               kh=kh[:, s], v=v[:, s], gl=gl[:, s], m=masks, y_ref=y_ref,
                               where=(direction, b, p), lanes=s))

    r0 = [_dot_nt(jnp.concatenate([jnp.where(m0, ch["kkt"], zero), jnp.where(m0, ch["rt"], zero)], 0),
                  jnp.concatenate([ch["bt"], ch["kt"]], 0)) for ch in chains]
    r1 = [_dot_nt(jnp.concatenate([jnp.where(m1, ch["rt"], zero), jnp.where(m1, ch["kkt"], zero)], 0),
                  jnp.concatenate([ch["kt"], ch["bt"]], 0)) for ch in chains]
    s_old = [s_scr[ch["where"]] for ch in chains]
    proj = [_dot_nt(jnp.concatenate([ch["kkt"], ch["rt"]], 0), st) for ch, st in zip(chains, s_old)]
    a_bk = [pick(*ch["m"]["bk"], x0, x1) for ch, x0, x1 in zip(chains, r0, r1)]
    a_kk = [pick(*ch["m"]["kk"], x0, x1) for ch, x0, x1 in zip(chains, r0, r1)]
    a_rbk = [jnp.concatenate([pick(*ch["m"]["rb"], x0, x1), pick(*ch["m"]["rk"], x0, x1)], 1)
             for ch, x0, x1 in zip(chains, r0, r1)]
    v_sw = [jnp.concatenate([jnp.where(m1, ch["v"], zero), jnp.where(m0, ch["v"], zero)], 0) for ch in chains]
    akkv = [_dot(a, x) for a, x in zip(a_kk, v_sw)]
    t_inv = _unit_tri_inverse(a_bk, inv_masks)
    ks_sm = [jnp.concatenate([jnp.where(m0, pj[:c], zero), jnp.where(m1, pj[:c], zero)], 0) for pj in proj]
    u_sm = [-_dot(t, k_ + a) for t, k_, a in zip(t_inv, ks_sm, akkv)]
    y_sm = [_dot(a, jnp.concatenate([u, x], 0)) for a, u, x in zip(a_rbk, u_sm, v_sw)]
    upd = [_dot_tn(jnp.concatenate([u[:c] + u[c:], ch["v"]], 0), jnp.concatenate([ch["bh"], ch["kh"]], 0))
           for ch, u in zip(chains, u_sm)]
    for i, ch in enumerate(chains):
        _, b, _ = ch["where"]
        ch["y_ref"][b, :, ch["lanes"]] = proj[i][c:] + y_sm[i][:c] + y_sm[i][c:]
        s_scr[ch["where"]] = s_old[i] * ch["gl"] + jnp.where(ch["m"]["diag"], upd[i], zero)


def _rk_scan_body(rf, rb, kkf, kkb, vf, vb, lwf, lwb, bf, bb, kdf, kdb, cum_ref, yf, yb, s_scr,
                  *, batches_per_trip):
    @pl.when(pl.program_id(0) == 0)
    def _():
        s_scr[...] = jnp.zeros(s_scr.shape, F32)

    fwd = (rf, kkf, vf, lwf, bf, kdf, yf)
    bwd = (rb, kkb, vb, lwb, bb, kdb, yb)

    def trip(i, carry):
        streams = []
        for k in range(batches_per_trip):
            b = i * batches_per_trip + k
            streams += [(0, b, fwd), (1, b, bwd)]
        _rk_scan_chunks(streams, cum_ref, s_scr)
        return carry

    lax.fori_loop(0, rf.shape[0] // batches_per_trip, trip, 0)


def _rk_out_body(x_ref, yf_ref, yb_ref, bonus_ref, gate_ref, mod_ref, lxg_ref, lxb_ref, wo_ref, e_ref,
                 et_ref, lng_ref, lnb_ref, o_ref, *, alpha, head):
    x = x_ref[...]
    y = yf_ref[...] + yb_ref[...]
    mean = _segsum(y, e_ref, et_ref) * (1.0 / head)
    cen = y - mean
    var = _segsum(cen * cen, e_ref, et_ref, split=False) * (1.0 / head)
    yn = cen * lax.rsqrt(var + RK_GN_EPS) * lxg_ref[...] + lxb_ref[...]
    out = _dot((yn + bonus_ref[...]) * gate_ref[...], wo_ref[...])
    gt = mod_ref[2:3, :]
    o_ref[...] = _layer_norm(alpha * x + gt * out, lng_ref[...], lnb_ref[...])


def _rwkv_layer(geom, x, mods, mu, w_rkv, w0, w1, w2, a0, a1, a2, g1, g2, k_k, k_a, r_k,
                lnx_g, lnx_b, w_o, ln_g, ln_b, alpha, want_ctx=True, tm=256):
    t, d = x.shape
    t_out = t if want_ctx else geom.n_lat
    head = r_k.shape[-1]
    assert 2 * head == LANES and geom.ctx == tm and geom.seq % tm == 0
    lora_d, lora_a, lora_g = w1.shape[-1], a1.shape[-1], g1.shape[-1]
    e, et = _head_indicator(d, head)

    def cat_dirs(w):
        return jnp.concatenate([w[0], w[1]], axis=-1).astype(BF16)

    def pad_dirs(w):
        z = jnp.zeros_like(w[0])
        return jnp.stack([jnp.concatenate([w[0], z], 0), jnp.concatenate([z, w[1]], 0)]).astype(BF16)

    nb = t // GRID_W
    nt = geom.ctx + geom.seq
    row = lambda i: (i, 0)
    tile = pl.BlockSpec((tm, d), row)
    scan_tile = pl.BlockSpec((None, tm, d), lambda i: (*_scan_tile_index(geom, i, tm), 0))
    scan_tile2 = pl.BlockSpec((2, None, tm, d), lambda i: (0, *_scan_tile_index(geom, i, tm), 0))
    mod_spec = pl.BlockSpec((None, 6, d), lambda i: (geom.mod_row(i, tm), 0, 0))
    per = tm // GRID_W
    feat = pl.pallas_call(
        functools.partial(_rk_feat_body, n_lat_tiles=geom.n_lat // tm,
                          tiles_per_batch=geom.seq // tm, seq=geom.seq),
        grid=(t // tm,),
        in_specs=[tile,
                  pl.BlockSpec((GRID_W, d), lambda i: (jnp.maximum(i * per - 1, 0), 0)),
                  pl.BlockSpec((GRID_W, d), lambda i: (jnp.minimum(i * per + per, nb - 1), 0)),
                  mod_spec, _full((6, d)), _resident((3, d, d)),
                  _full((d, 2 * lora_d)), _full((2, 2 * lora_d, d)), _full((2, d)),
                  _full((d, 2 * lora_a)), _full((2, 2 * lora_a, d)), _full((2, d)),
                  _full((d, lora_g)), _full((lora_g, d)),
                  _full((1, d)), _full((1, d)), _full((1, d)), _full((d, LANES)), _full((LANES, d))],
        out_specs=[scan_tile, scan_tile, scan_tile, tile, tile, scan_tile2, scan_tile2, scan_tile2],
        out_shape=([jax.ShapeDtypeStruct((geom.b, nt, d), F32)] * 3 + [jax.ShapeDtypeStruct((t, d), F32)] * 2
                   + [jax.ShapeDtypeStruct((2, geom.b, nt, d), F32)] * 3),
        compiler_params=_params(("parallel",)),
        name="rwkv_feat",
    )
    r, kkn, v, gate, bonus, lw, bb, kd = feat(
        x, x, x, mods, mu, w_rkv.astype(BF16), cat_dirs(w1), pad_dirs(w2), w0,
        cat_dirs(a1), pad_dirs(a2), a0, g1.astype(BF16), g2.astype(BF16),
        k_k.reshape(1, d), k_a.reshape(1, d), r_k.reshape(1, d), e, et)

    n_steps = nt // CHUNK
    shared = [pl.BlockSpec((geom.b, CHUNK, d), lambda s, z=z: (0, _scan_step_chunk(geom, z, s), 0))
              for z in range(2)]
    per_dir = [pl.BlockSpec((None, geom.b, CHUNK, d), lambda s, z=z: (z, 0, _scan_step_chunk(geom, z, s), 0))
               for z in range(2)]
    yf, yb = pl.pallas_call(
        functools.partial(_rk_scan_body, batches_per_trip=1),
        grid=(n_steps,),
        in_specs=shared * 3 + per_dir * 3 + [_full((2, CHUNK, CHUNK))],
        out_specs=shared,
        out_shape=[jax.ShapeDtypeStruct((geom.b, nt, d), F32)] * 2,
        scratch_shapes=[pltpu.VMEM((2, geom.b, d // LANES, LANES, LANES), F32)],
        compiler_params=_params(("arbitrary",)),
        name="rwkv_scan",
    )(r, r, kkn, kkn, v, v, lw, lw, bb, bb, kd, kd, _cum_matrices())

    return pl.pallas_call(
        functools.partial(_rk_out_body, alpha=alpha, head=head),
        grid=(t_out // tm,),
        in_specs=[tile, scan_tile, scan_tile, tile, tile, mod_spec, _full((1, d)), _full((1, d)),
                  _resident((d, d)), _full((d, LANES)), _full((LANES, d)), _full((1, d)), _full((1, d))],
        out_specs=tile,
        out_shape=jax.ShapeDtypeStruct((t_out, d), F32),
        compiler_params=_params(("parallel",)),
        name="rwkv_out",
    )(x, yf, yb, bonus, gate, mods, lnx_g.reshape(1, d), lnx_b.reshape(1, d), w_o.astype(BF16), e, et,
      ln_g.reshape(1, d), ln_b.reshape(1, d))


HALO = 8


def _gdn_feat_body(x_ref, xp_ref, xn_ref, mod_ref, wq_ref, wz_ref, wab_ref, cw_ref, alog_ref, dtb_ref,
                   q_out, k_out, v_out, z_out, gb_out, proj_scr,
                   *, n_lat_tiles, tiles_per_batch, head, n_taps):
    i = pl.program_id(0)
    tm, d = x_ref.shape
    sh, sc = mod_ref[0:1, :], mod_ref[1:2, :]
    h = x_ref[...] * (1.0 + sc) + sh
    is_ctx = i >= n_lat_tiles
    j = i % tiles_per_batch
    has_prev = jnp.logical_and(jnp.logical_not(is_ctx), j > 0)
    has_next = jnp.logical_and(jnp.logical_not(is_ctx), j < tiles_per_batch - 1)
    hp = jnp.where(has_prev, xp_ref[...] * (1.0 + sc) + sh, 0.0)
    hn = jnp.where(has_next, xn_ref[...] * (1.0 + sc) + sh, 0.0)
    hb = h.astype(BF16)
    proj_scr[...] = _dot(jnp.concatenate([hp, h, hn], axis=0), wq_ref[...])
    left = n_taps // 2
    conv = jnp.zeros((tm, 3 * d), F32)
    for tap in range(n_taps):
        conv = conv + proj_scr[pl.ds(HALO + tap - left, tm), :] * cw_ref[tap:tap + 1, :]
    qkv = _silu(conv)
    n_heads = d // head
    for hd in range(n_heads):
        qs = qkv[:, hd * head:(hd + 1) * head]
        ks = qkv[:, d + hd * head:d + (hd + 1) * head]
        q_out[:, hd * head:(hd + 1) * head] = (
            qs * lax.rsqrt(jnp.sum(qs * qs, axis=-1, keepdims=True) + GDN_L2_EPS) * head ** -0.5)
        k_out[:, hd * head:(hd + 1) * head] = (
            ks * lax.rsqrt(jnp.sum(ks * ks, axis=-1, keepdims=True) + GDN_L2_EPS))
    v_out[...] = qkv[:, 2 * d:]
    z_out[...] = jnp.dot(hb, wz_ref[...], preferred_element_type=F32)
    ab = _dot3(h, wab_ref[...])
    lane = lax.broadcasted_iota(jnp.int32, ab.shape, 1)
    g = -jnp.exp(alog_ref[...]) * _softplus(ab + dtb_ref[...])
    gb_out[...] = jnp.where(lane < 2 * n_heads, g, _sigmoid(ab))


def _gdn_scan_chunks(streams, s_scr, n_heads):
    c = CHUNK
    zero = jnp.float32(0.0)
    inv_masks = _inverse_masks(2 * c)
    row = lax.broadcasted_iota(jnp.int32, (2 * c, 1), 0)
    top_rows = row < c

    chains = []
    for direction, b, (q_ref, k_ref, v_ref, gb_ref, y_ref) in streams:
        head = q_ref.shape[-1] // n_heads
        strict, incl, top, left = _order_masks(direction)
        same = top == left
        masks = dict(strict=same & strict, incl=same & incl,
                     cum=jnp.where(same & incl, 1.0, 0.0).astype(BF16),
                     nxt=jnp.where(same & strict, 1.0, 0.0))
        first, last = (0, c - 1) if direction == 0 else (c - 1, 0)
        gb = gb_ref[b]
        col = lambda idx: gb[:, idx:idx + 1]
        for p in range(n_heads // 2):
            h0, h1 = 2 * p, 2 * p + 1
            stack = lambda ref: jnp.concatenate([ref[b, :, h0 * head:(h0 + 1) * head],
                                                 ref[b, :, h1 * head:(h1 + 1) * head]], 0)
            chains.append(dict(
                g=jnp.concatenate([col(direction * n_heads + h0), col(direction * n_heads + h1)], 0),
                beta=jnp.concatenate([col((2 + direction) * n_heads + h0),
                                      col((2 + direction) * n_heads + h1)], 0),
                q=stack(q_ref), k=stack(k_ref), v=stack(v_ref),
                m=masks, first=first, last=last, y_ref=y_ref, b=b, head=head,
                where=((direction, b, h0), (direction, b, h1)),
                lanes=(slice(h0 * head, (h0 + 1) * head), slice(h1 * head, (h1 + 1) * head))))

    dlog = [_dot_rhs2(ch["m"]["cum"], ch["g"] * ch["m"]["nxt"]) for ch in chains]
    cs, cs_l0, cs_l1 = [], [], []
    for ch, dl in zip(chains, dlog):
        f, l, g = ch["first"], ch["last"], ch["g"]
        x = jnp.where(top_rows, dl[:, f:f + 1] + g[f:f + 1], dl[:, c + f:c + f + 1] + g[c + f:c + f + 1])
        cs.append(x)
        cs_l0.append(x[l:l + 1])
        cs_l1.append(x[c + l:c + l + 1])
    cs_end = [jnp.where(top_rows, x0, x1) for x0, x1 in zip(cs_l0, cs_l1)]
    decay = [jnp.exp(jnp.where(ch["m"]["incl"], x, -jnp.inf)) for ch, x in zip(chains, dlog)]
    kb = [ch["k"] * ch["beta"] for ch in chains]
    vb = [ch["v"] * ch["beta"] for ch in chains]
    kkt = [_dot_nt(x, ch["k"]) for x, ch in zip(kb, chains)]
    qkt = [_dot_nt(ch["q"], ch["k"]) for ch in chains]
    lower = [jnp.where(ch["m"]["strict"], x * dc, zero) for ch, x, dc in zip(chains, kkt, decay)]
    a_qk = [jnp.where(ch["m"]["incl"], x * dc, zero) for ch, x, dc in zip(chains, qkt, decay)]
    t_inv = _unit_tri_inverse(lower, inv_masks)
    eg = [jnp.exp(x) for x in cs]
    uw = [_dot(t, jnp.concatenate([x, y * e], axis=1)) for t, x, y, e in zip(t_inv, vb, kb, eg)]
    s_old = [(s_scr[ch["where"][0]], s_scr[ch["where"][1]]) for ch in chains]
    ws = [jnp.concatenate([_dot(x[:c, ch["head"]:], s0), _dot(x[c:, ch["head"]:], s1)], 0)
          for ch, x, (s0, s1) in zip(chains, uw, s_old)]
    qs = [jnp.concatenate([_dot((ch["q"] * e)[:c], s0), _dot((ch["q"] * e)[c:], s1)], 0)
          for ch, e, (s0, s1) in zip(chains, eg, s_old)]
    v_new = [x[:, :ch["head"]] - y for ch, x, y in zip(chains, uw, ws)]
    o = [x + _dot(a, y) for x, a, y in zip(qs, a_qk, v_new)]
    k_dec = [ch["k"] * jnp.exp(ce - y) for ch, ce, y in zip(chains, cs_end, cs)]
    for i, ch in enumerate(chains):
        ch["y_ref"][ch["b"], :, ch["lanes"][0]] = o[i][:c]
        ch["y_ref"][ch["b"], :, ch["lanes"][1]] = o[i][c:]
        s_scr[ch["where"][0]] = s_old[i][0] * jnp.exp(cs_l0[i]) + _dot_tn(k_dec[i][:c], v_new[i][:c])
        s_scr[ch["where"][1]] = s_old[i][1] * jnp.exp(cs_l1[i]) + _dot_tn(k_dec[i][c:], v_new[i][c:])


def _gdn_scan_body(qf, qb, kf, kb, vf, vb, gf, gb, yf, yb, s_scr, *, n_heads, batches_per_trip):
    @pl.when(pl.program_id(0) == 0)
    def _():
        s_scr[...] = jnp.zeros(s_scr.shape, F32)

    fwd = (qf, kf, vf, gf, yf)
    bwd = (qb, kb, vb, gb, yb)

    def trip(i, carry):
        streams = []
        for j in range(batches_per_trip):
            b = i * batches_per_trip + j
            streams += [(0, b, fwd), (1, b, bwd)]
        _gdn_scan_chunks(streams, s_scr, n_heads)
        return carry

    lax.fori_loop(0, qf.shape[0] // batches_per_trip, trip, 0)


def _gdn_out_body(x_ref, of_ref, ob_ref, z_ref, mod_ref, nw_ref, wo_ref, lng_ref, lnb_ref, out_ref,
                  *, alpha, head):
    x = x_ref[...]
    o = of_ref[...] + ob_ref[...]
    z = z_ref[...]
    d = x.shape[1]
    parts = []
    for hd in range(d // head):
        oh = o[:, hd * head:(hd + 1) * head]
        on = oh * lax.rsqrt(jnp.mean(oh * oh, axis=-1, keepdims=True) + GDN_NORM_EPS) * nw_ref[...]
        parts.append(on * _silu(z[:, hd * head:(hd + 1) * head]))
    y = _dot(jnp.concatenate(parts, axis=-1), wo_ref[...])
    gt = mod_ref[2:3, :]
    out_ref[...] = _layer_norm(alpha * x + gt * y, lng_ref[...], lnb_ref[...])


def _gdn_layer(geom, x, mods, w_in, conv_w, a_log, dt_bias, norm_w, w_o, ln_g, ln_b, alpha, tm=256):
    t, d = x.shape
    n_heads = a_log.shape[-1]
    head = d // n_heads
    n_taps = conv_w.shape[0]
    assert head == LANES and geom.ctx == tm and geom.seq % tm == 0 and n_heads % 2 == 0
    assert 4 * n_heads <= LANES
    w_qkv = w_in[:, :3 * d].astype(BF16)
    w_z = w_in[:, 3 * d:4 * d].astype(BF16)
    w_ab = jnp.zeros((d, LANES), F32).at[:, :4 * n_heads].set(w_in[:, 4 * d:])
    alog = jnp.zeros((1, LANES), F32).at[0, :2 * n_heads].set(a_log.reshape(-1))
    dtb = jnp.zeros((1, LANES), F32).at[0, :2 * n_heads].set(dt_bias.reshape(-1))

    nb = t // HALO
    nt = geom.ctx + geom.seq
    per = tm // HALO
    tile = pl.BlockSpec((tm, d), lambda i: (i, 0))
    scan_tile = pl.BlockSpec((None, tm, d), lambda i: (*_scan_tile_index(geom, i, tm), 0))
    small = pl.BlockSpec((None, tm, LANES), lambda i: (*_scan_tile_index(geom, i, tm), 0))
    mod_spec = pl.BlockSpec((None, 6, d), lambda i: (geom.mod_row(i, tm), 0, 0))
    q, k, v, z, gb = pl.pallas_call(
        functools.partial(_gdn_feat_body, n_lat_tiles=geom.n_lat // tm,
                          tiles_per_batch=geom.seq // tm, head=head, n_taps=n_taps),
        grid=(t // tm,),
        in_specs=[tile,
                  pl.BlockSpec((HALO, d), lambda i: (jnp.maximum(i * per - 1, 0), 0)),
                  pl.BlockSpec((HALO, d), lambda i: (jnp.minimum(i * per + per, nb - 1), 0)),
                  mod_spec, _resident((d, 3 * d)), _resident((d, d)), _full((d, LANES)),
                  _full((n_taps, 3 * d)), _full((1, LANES)), _full((1, LANES))],
        out_specs=[scan_tile, scan_tile, scan_tile, tile, small],
        out_shape=([jax.ShapeDtypeStruct((geom.b, nt, d), F32)] * 3 + [jax.ShapeDtypeStruct((t, d), F32)]
                   + [jax.ShapeDtypeStruct((geom.b, nt, LANES), F32)]),
        scratch_shapes=[pltpu.VMEM((tm + 2 * HALO, 3 * d), F32)],
        compiler_params=_params(("parallel",)),
        name="gdn_feat",
    )(x, x, x, mods, w_qkv, w_z, w_ab, conv_w, alog, dtb)

    n_steps = nt // CHUNK
    shared = [pl.BlockSpec((geom.b, CHUNK, d), lambda s, zz=zz: (0, _scan_step_chunk(geom, zz, s), 0))
              for zz in range(2)]
    shared_small = [pl.BlockSpec((geom.b, CHUNK, LANES), lambda s, zz=zz: (0, _scan_step_chunk(geom, zz, s), 0))
                    for zz in range(2)]
    of, ob = pl.pallas_call(
        functools.partial(_gdn_scan_body, n_heads=n_heads, batches_per_trip=2),
        grid=(n_steps,),
        in_specs=shared * 3 + shared_small,
        out_specs=shared,
        out_shape=[jax.ShapeDtypeStruct((geom.b, nt, d), F32)] * 2,
        scratch_shapes=[pltpu.VMEM((2, geom.b, n_heads, head, head), F32)],
        compiler_params=_params(("arbitrary",)),
        name="gdn_scan",
    )(q, q, k, k, v, v, gb, gb)

    return pl.pallas_call(
        functools.partial(_gdn_out_body, alpha=alpha, head=head),
        grid=(t // tm,),
        in_specs=[tile, scan_tile, scan_tile, tile, mod_spec, _full((1, head)), _resident((d, d)),
                  _full((1, d)), _full((1, d))],
        out_specs=tile,
        out_shape=jax.ShapeDtypeStruct((t, d), F32),
        compiler_params=_params(("parallel",)),
        name="gdn_out",
    )(x, of, ob, z, mods, norm_w.reshape(1, head), w_o.astype(BF16), ln_g.reshape(1, d), ln_b.reshape(1, d))


def kernel(x, c, ctx, c_ctx, ada_w, ada_b, ln_g, ln_b, rk_mu, rk_w_rkv, rk_w0, rk_w1, rk_w2, rk_a0, rk_a1, rk_a2, rk_g1, rk_g2, rk_k_k, rk_k_a, rk_r_k, rk_lnx_g, rk_lnx_b, rk_w_o, pool_w, pool_scale, gdn_w_in, gdn_conv_w, gdn_a_log, gdn_dt_bias, gdn_norm_w, gdn_w_o, ffn_w1, ffn_w3, ffn_w2, moe_router_w, moe_router_b, moe_w1, moe_w3, moe_w2):
    batch, seq, d = x.shape
    ctx_len = ctx.shape[1]
    depth = ada_w.shape[0]
    geom = _Geom(batch, seq, ctx_len, d)
    alpha = (2 * depth) ** 0.25
    n_mixers = 3

    rows = -(-(batch + 1) // 8) * 8
    cvec = jnp.zeros((rows, d), F32).at[:batch].set(c).at[batch].set(c_ctx)
    table = _ada_table(cvec, ada_w, ada_b)
    mods_all = table[:, :batch + 1].reshape(depth, batch + 1, 6, d)

    xs = jnp.concatenate([x.reshape(batch * seq, d), ctx.reshape(batch * ctx_len, d)], axis=0)
    for i in range(depth):
        mods = mods_all[i]
        kind, j = i % n_mixers, i // n_mixers
        if kind == 0:
            xs = _rwkv_layer(geom, xs, mods, rk_mu[j], rk_w_rkv[j], rk_w0[j], rk_w1[j], rk_w2[j],
                             rk_a0[j], rk_a1[j], rk_a2[j], rk_g1[j], rk_g2[j], rk_k_k[j], rk_k_a[j],
                             rk_r_k[j], rk_lnx_g[j], rk_lnx_b[j], rk_w_o[j], ln_g[i, 0], ln_b[i, 0], alpha,
                             want_ctx=i < depth - 1)
        elif kind == 1:
            xs = _pool_layer(geom, xs, mods, pool_w[j], pool_scale[j], ln_g[i, 0], ln_b[i, 0], alpha)
        else:
            xs = _gdn_layer(geom, xs, mods, gdn_w_in[j], gdn_conv_w[j], gdn_a_log[j], gdn_dt_bias[j],
                            gdn_norm_w[j], gdn_w_o[j], ln_g[i, 0], ln_b[i, 0], alpha)
        e = i // 2
        if i % 2 == 0:
            xs = _ffn_layer(geom, xs, mods, ffn_w1[e], ffn_w3[e], ffn_w2[e], ln_g[i, 1], ln_b[i, 1], alpha)
        else:
            xs = _moe_layer(geom, xs, mods, moe_router_w[e], moe_router_b[e], moe_w1[e], moe_w3[e],
                            moe_w2[e], ln_g[i, 1], ln_b[i, 1], alpha)
    return xs[:batch * seq].reshape(batch, seq, d)
```

```python
import functools
import math

import jax
import jax.numpy as jnp
import numpy as np
from jax import lax
from jax.experimental import pallas as pl
from jax.experimental.pallas import tpu as pltpu

F32 = jnp.float32
BF16 = jnp.bfloat16

GRID_W = 64
CHUNK = 64
LANES = 128
POOL_WINDOWS = (2, 4, 8, 16)
LN_EPS = 1e-5
RK_DECAY_SCALE = math.exp(-0.5)
RK_GN_EPS = 64e-5
GDN_NORM_EPS = 1e-6
GDN_L2_EPS = 1e-6
TOP_K = 2
VMEM_LIMIT = 56 * 1024 * 1024


def _sigmoid(x):
    return 1.0 / (1.0 + jnp.exp(-x))


def _silu(x):
    return x * _sigmoid(x)


def _softplus(x):
    return jnp.maximum(x, 0.0) + jnp.log(1.0 + jnp.exp(-jnp.abs(x)))


def _dot(a, b):
    return jnp.dot(a.astype(BF16), b.astype(BF16), preferred_element_type=F32)


def _dot_nt(a, b):
    return lax.dot_general(a.astype(BF16), b.astype(BF16), (((1,), (1,)), ((), ())),
                           preferred_element_type=F32)


def _dot_tn(a, b):
    return lax.dot_general(a.astype(BF16), b.astype(BF16), (((0,), (0,)), ((), ())),
                           preferred_element_type=F32)


def _split(x):
    hi = x.astype(BF16)
    lo = (x - hi.astype(F32)).astype(BF16)
    return hi, lo


def _dot_lhs2(a, b_exact):
    hi, lo = _split(a)
    return (jnp.dot(hi, b_exact, preferred_element_type=F32)
            + jnp.dot(lo, b_exact, preferred_element_type=F32))


def _dot_rhs2(a_exact, b):
    hi, lo = _split(b)
    return (jnp.dot(a_exact, hi, preferred_element_type=F32)
            + jnp.dot(a_exact, lo, preferred_element_type=F32))


def _dot3(a, b):
    ah, al = _split(a)
    bh, bl = _split(b)
    return (jnp.dot(ah, bh, preferred_element_type=F32)
            + jnp.dot(ah, bl, preferred_element_type=F32)
            + jnp.dot(al, bh, preferred_element_type=F32))


def _segsum(x, e_ref, et_ref, split=True):
    if not split:
        return _dot(_dot(x, e_ref[...]), et_ref[...])
    s = _dot_lhs2(x, e_ref[...])
    return _dot_lhs2(s, et_ref[...])


def _layer_norm(v, g, b):
    mean = jnp.mean(v, axis=-1, keepdims=True)
    c = v - mean
    var = jnp.mean(c * c, axis=-1, keepdims=True)
    return c * lax.rsqrt(var + LN_EPS) * g + b


def _head_indicator(d_model, head):
    n_heads = d_model // head
    e = np.zeros((d_model, LANES), np.float32)
    e[np.arange(d_model), np.arange(d_model) // head] = 1.0
    assert n_heads <= LANES
    return jnp.asarray(e, BF16), jnp.asarray(e.T.copy(), BF16)


def _full(shape):
    n = len(shape)
    return pl.BlockSpec(shape, lambda *_: (0,) * n)


def _resident(shape):
    n = len(shape)
    return pl.BlockSpec(shape, lambda *_: (0,) * n, pipeline_mode=pl.Buffered(1))


def _params(sem):
    return pltpu.CompilerParams(dimension_semantics=sem, vmem_limit_bytes=VMEM_LIMIT)


class _Geom:
    def __init__(self, batch, seq, ctx_len, d_model):
        self.b, self.seq, self.ctx, self.d = batch, seq, ctx_len, d_model
        self.n_lat = batch * seq
        self.t = batch * (seq + ctx_len)

    def mod_row(self, i, tm):
        return jnp.where(i < self.n_lat // tm, i // (self.seq // tm), self.b)


def _ada_body(c_ref, w_ref, b_ref, o_ref):
    o_ref[...] = _dot3(_silu(c_ref[...]), w_ref[...]) + b_ref[...]


def _ada_table(cvec, ada_w, ada_b):
    depth, d, d6 = ada_w.shape
    rows = cvec.shape[0]
    return pl.pallas_call(
        _ada_body,
        grid=(depth, d6 // d),
        in_specs=[_full((rows, d)),
                  pl.BlockSpec((None, d, d), lambda l, j: (l, 0, j)),
                  pl.BlockSpec((None, 1, d), lambda l, j: (l, 0, j))],
        out_specs=pl.BlockSpec((None, rows, d), lambda l, j: (l, 0, j)),
        out_shape=jax.ShapeDtypeStruct((depth, rows, d6), F32),
        compiler_params=_params(("parallel", "parallel")),
        name="ada_table",
    )(cvec, ada_w, ada_b.reshape(depth, 1, d6))


def _ffn_body(x_ref, mod_ref, w1_ref, w3_ref, w2_ref, lng_ref, lnb_ref, o_ref, *, alpha, n_split):
    x = x_ref[...]
    sh, sc, gt = mod_ref[3:4, :], mod_ref[4:5, :], mod_ref[5:6, :]
    h = (x * (1.0 + sc) + sh).astype(BF16)
    ff = w1_ref.shape[1] // n_split
    acc = jnp.zeros(x.shape, F32)
    for s in range(n_split):
        a = jnp.dot(h, w1_ref[:, s * ff:(s + 1) * ff], preferred_element_type=F32)
        g = jnp.dot(h, w3_ref[:, s * ff:(s + 1) * ff], preferred_element_type=F32)
        acc = acc + jnp.dot((_silu(a) * g).astype(BF16), w2_ref[s * ff:(s + 1) * ff, :],
                            preferred_element_type=F32)
    o_ref[...] = _layer_norm(alpha * x + gt * acc, lng_ref[...], lnb_ref[...])


def _ffn_layer(geom, x, mods, w1, w3, w2, ln_g, ln_b, alpha, tm=512):
    t, d = x.shape
    ff = w1.shape[1]
    return pl.pallas_call(
        functools.partial(_ffn_body, alpha=alpha, n_split=2),
        grid=(t // tm,),
        in_specs=[pl.BlockSpec((tm, d), lambda i: (i, 0)),
                  pl.BlockSpec((None, 6, d), lambda i: (geom.mod_row(i, tm), 0, 0)),
                  _resident((d, ff)), _resident((d, ff)), _resident((ff, d)),
                  _full((1, d)), _full((1, d))],
        out_specs=pl.BlockSpec((tm, d), lambda i: (i, 0)),
        out_shape=jax.ShapeDtypeStruct((t, d), F32),
        compiler_params=_params(("parallel",)),
        name="ffn",
    )(x, mods, w1.astype(BF16), w3.astype(BF16), w2.astype(BF16),
      ln_g.reshape(1, d), ln_b.reshape(1, d))


MOE_GROUP = 512
MOE_ROWS = 160
MOE_TAIL_ROWS = 64


def _moe_body(x_ref, mod_ref, rw_ref, rb_ref, tri_ref, w1_ref, w3_ref, w2_ref, lng_ref, lnb_ref, o_ref,
              h_scr, gate_scr, sel_scr, rank_scr, selt_scr, rankt_scr, acc_scr, *, alpha, n_experts):
    e = pl.program_id(1)
    tm = x_ref.shape[0]
    n_groups = tm // MOE_GROUP

    @pl.when(e == 0)
    def _():
        x = x_ref[...]
        sh, sc = mod_ref[3:4, :], mod_ref[4:5, :]
        h = x * (1.0 + sc) + sh
        h_scr[...] = h.astype(BF16)
        logits = _dot3(h, rw_ref[...]) + rb_ref[...]
        lane = lax.broadcasted_iota(jnp.int32, logits.shape, 1)
        neg = jnp.float32(-jnp.inf)
        logits = jnp.where(lane < n_experts, logits, neg)
        m1 = jnp.max(logits, axis=-1, keepdims=True)
        i1 = jnp.min(jnp.where(logits == m1, lane, LANES), axis=-1, keepdims=True)
        rest = jnp.where(lane == i1, neg, logits)
        m2 = jnp.max(rest, axis=-1, keepdims=True)
        i2 = jnp.min(jnp.where(rest == m2, lane, LANES), axis=-1, keepdims=True)
        e2 = jnp.exp(m2 - m1)
        p1 = 1.0 / (1.0 + e2)
        p2 = e2 / (1.0 + e2)
        gate_scr[...] = jnp.where(lane == i1, p1, 0.0) + jnp.where(lane == i2, p2, 0.0)
        sel = jnp.where(lane == i1, 1.0, 0.0) + jnp.where(lane == i2, 1.0, 0.0)
        sel_scr[...] = sel
        for s in range(n_groups):
            rows = slice(s * MOE_GROUP, (s + 1) * MOE_GROUP)
            rank = jnp.dot(tri_ref[...], sel[rows].astype(BF16), preferred_element_type=F32)
            rank_scr[rows, :] = rank
            rankt_scr[s] = rank.T
            selt_scr[s] = sel[rows].T
        acc_scr[...] = jnp.zeros(acc_scr.shape, F32)

    lane = lax.broadcasted_iota(jnp.int32, (MOE_GROUP, LANES), 1)
    column = lambda ref, rows: jnp.sum(jnp.where(lane == e, ref[rows, :], 0.0), axis=-1, keepdims=True)
    groups = []
    most = jnp.int32(0)
    for s in range(n_groups):
        rows = slice(s * MOE_GROUP, (s + 1) * MOE_GROUP)
        sel_col = column(sel_scr, rows) > 0.5
        groups.append(dict(rows=rows, gate_col=column(gate_scr, rows), sel_col=sel_col,
                           rank_col=column(rank_scr, rows),
                           sel_row=selt_scr[s, pl.ds(e, 1), :] > 0.5,
                           rank_row=rankt_scr[s, pl.ds(e, 1), :]))
        most = jnp.maximum(most, jnp.sum(jnp.where(sel_col, 1.0, 0.0)).astype(jnp.int32))

    def run_pass(first_slot, n_rows):
        base = first_slot.astype(F32)
        slot_col = lax.broadcasted_iota(jnp.int32, (n_rows, 1), 0).astype(F32) + base
        slot_row = lax.broadcasted_iota(jnp.int32, (1, n_rows), 1).astype(F32) + base
        xg = []
        for gr in groups:
            take = jnp.where((gr["rank_row"] == slot_col) & gr["sel_row"], 1.0, 0.0).astype(BF16)
            xg.append(jnp.dot(take, h_scr[gr["rows"], :], preferred_element_type=F32).astype(BF16))
        xg = jnp.concatenate(xg, axis=0)
        a = jnp.dot(xg, w1_ref[...], preferred_element_type=F32)
        g = jnp.dot(xg, w3_ref[...], preferred_element_type=F32)
        y = jnp.dot((_silu(a) * g).astype(BF16), w2_ref[...], preferred_element_type=F32)
        for k, gr in enumerate(groups):
            put = jnp.where((gr["rank_col"] == slot_row) & gr["sel_col"], 1.0, 0.0).astype(BF16)
            acc_scr[gr["rows"], :] += gr["gate_col"] * _dot(put, y[k * n_rows:(k + 1) * n_rows])

    n_full = most // MOE_ROWS
    left = most - n_full * MOE_ROWS

    def full_pass(j, carry):
        run_pass(j * MOE_ROWS, MOE_ROWS)
        return carry

    lax.fori_loop(0, n_full, full_pass, 0)

    @pl.when(left > MOE_TAIL_ROWS)
    def _():
        run_pass(n_full * MOE_ROWS, MOE_ROWS)

    @pl.when(jnp.logical_and(left > 0, left <= MOE_TAIL_ROWS))
    def _():
        run_pass(n_full * MOE_ROWS, MOE_TAIL_ROWS)

    @pl.when(e == n_experts - 1)
    def _():
        x = x_ref[...]
        gt = mod_ref[5:6, :]
        o_ref[...] = _layer_norm(alpha * x + gt * acc_scr[...], lng_ref[...], lnb_ref[...])


def _moe_layer(geom, x, mods, router_w, router_b, w1, w3, w2, ln_g, ln_b, alpha, tm=1024):
    t, d = x.shape
    n_e, _, ffe = w1.shape
    assert tm % MOE_GROUP == 0 and n_e <= LANES
    rw = jnp.zeros((d, LANES), F32).at[:, :n_e].set(router_w)
    rb = jnp.zeros((1, LANES), F32).at[0, :n_e].set(router_b)
    idx = np.arange(MOE_GROUP)
    tri = jnp.asarray(idx[None, :] < idx[:, None], BF16)
    n_groups = tm // MOE_GROUP
    return pl.pallas_call(
        functools.partial(_moe_body, alpha=alpha, n_experts=n_e),
        grid=(t // tm, n_e),
        in_specs=[pl.BlockSpec((tm, d), lambda i, e: (i, 0)),
                  pl.BlockSpec((None, 6, d), lambda i, e: (geom.mod_row(i, tm), 0, 0)),
                  _full((d, LANES)), _full((1, LANES)), _full((MOE_GROUP, MOE_GROUP)),
                  pl.BlockSpec((None, d, ffe), lambda i, e: (e, 0, 0)),
                  pl.BlockSpec((None, d, ffe), lambda i, e: (e, 0, 0)),
                  pl.BlockSpec((None, ffe, d), lambda i, e: (e, 0, 0)),
                  _full((1, d)), _full((1, d))],
        out_specs=pl.BlockSpec((tm, d), lambda i, e: (i, 0)),
        out_shape=jax.ShapeDtypeStruct((t, d), F32),
        scratch_shapes=[pltpu.VMEM((tm, d), BF16), pltpu.VMEM((tm, LANES), F32),
                        pltpu.VMEM((tm, LANES), F32), pltpu.VMEM((tm, LANES), F32),
                        pltpu.VMEM((n_groups, LANES, MOE_GROUP), F32),
                        pltpu.VMEM((n_groups, LANES, MOE_GROUP), F32),
                        pltpu.VMEM((tm, d), F32)],
        compiler_params=_params(("parallel", "arbitrary")),
        name="moe",
    )(x, mods, rw, rb, tri, w1.astype(BF16), w3.astype(BF16), w2.astype(BF16),
      ln_g.reshape(1, d), ln_b.reshape(1, d))


def _pool_matrices(tm, seq_len):
    t = np.arange(tm)
    pos = t % seq_len
    mats, inv = [], []
    for win in POOL_WINDOWS:
        lo = np.clip(pos - win // 2, 0, seq_len)
        hi = np.clip(pos + win // 2, 0, seq_len)
        base = t - pos
        j = t[None, :]
        mats.append(((j >= (base + lo)[:, None]) & (j < (base + hi)[:, None])).astype(np.float32))
        inv.append((hi - lo).astype(np.float32))
    return np.stack(mats), np.stack(inv)


def _pool_body(x_ref, mod_ref, pm_ref, cnt_ref, pw_ref, ps_ref, lng_ref, lnb_ref, o_ref, *, alpha):
    x = x_ref[...]
    sh, sc, gt = mod_ref[0:1, :], mod_ref[1:2, :], mod_ref[2:3, :]
    h = x * (1.0 + sc) + sh
    n_g = pm_ref.shape[0]
    gw = x.shape[1] // n_g
    outs = []
    for g in range(n_g):
        hg = h[:, g * gw:(g + 1) * gw]
        total = _dot_rhs2(pm_ref[g], hg)
        pooled = total / cnt_ref[g] - hg
        outs.append(_dot(pooled, pw_ref[g]))
    y = jnp.concatenate(outs, axis=-1) * ps_ref[...]
    o_ref[...] = _layer_norm(alpha * x + gt * y, lng_ref[...], lnb_ref[...])


def _pool_layer(geom, x, mods, pool_w, pool_scale, ln_g, ln_b, alpha, tm=256):
    t, d = x.shape
    n_g = len(POOL_WINDOWS)
    m_lat, c_lat = _pool_matrices(tm, GRID_W)
    m_ctx, c_ctx = _pool_matrices(tm, geom.ctx)
    pm = jnp.asarray(np.stack([m_lat, m_ctx]), BF16)
    cnt = jnp.asarray(np.stack([c_lat, c_ctx])[..., None], F32)
    n_lat_tiles = geom.n_lat // tm
    kind = lambda i: jnp.where(i < n_lat_tiles, 0, 1)
    return pl.pallas_call(
        functools.partial(_pool_body, alpha=alpha),
        grid=(t // tm,),
        in_specs=[pl.BlockSpec((tm, d), lambda i: (i, 0)),
                  pl.BlockSpec((None, 6, d), lambda i: (geom.mod_row(i, tm), 0, 0)),
                  pl.BlockSpec((None, n_g, tm, tm), lambda i: (kind(i), 0, 0, 0)),
                  pl.BlockSpec((None, n_g, tm, 1), lambda i: (kind(i), 0, 0, 0)),
                  _full((n_g, d // n_g, d // n_g)), _full((1, d)), _full((1, d)), _full((1, d))],
        out_specs=pl.BlockSpec((tm, d), lambda i: (i, 0)),
        out_shape=jax.ShapeDtypeStruct((t, d), F32),
        compiler_params=_params(("parallel",)),
        name="pool",
    )(x, mods, pm, cnt, pool_w.astype(BF16), pool_scale.reshape(1, d),
      ln_g.reshape(1, d), ln_b.reshape(1, d))


def _scan_tile_index(geom, i, tm):
    n_lat_tiles = geom.n_lat // tm
    tpb = geom.seq // tm
    is_lat = i < n_lat_tiles
    return jnp.where(is_lat, i // tpb, i - n_lat_tiles), jnp.where(is_lat, geom.ctx // tm + i % tpb, 0)


def _scan_step_chunk(geom, direction, s):
    if direction == 0:
        return s
    nc_ctx = geom.ctx // CHUNK
    nc = (geom.ctx + geom.seq) // CHUNK
    return jnp.where(s < nc_ctx, nc_ctx - 1 - s, nc - 1 + nc_ctx - s)


def _order_masks(direction):
    n = 2 * CHUNK
    ri = lax.broadcasted_iota(jnp.int32, (n, n), 0)
    ci = lax.broadcasted_iota(jnp.int32, (n, n), 1)
    rt, ct = ri & (CHUNK - 1), ci & (CHUNK - 1)
    ahead = rt - ct if direction == 0 else ct - rt
    return ahead > 0, ahead >= 0, ri < CHUNK, ci < CHUNK


INV_BASE = 8


def _inverse_masks(n):
    ri = lax.broadcasted_iota(jnp.int32, (n, n), 0)
    ci = lax.broadcasted_iota(jnp.int32, (n, n), 1)
    eye = (ri == ci).astype(F32)
    same = lambda size: (ri // size) == (ci // size)
    base = same(INV_BASE)
    levels = []
    size = INV_BASE
    while size < CHUNK:
        levels.append(same(2 * size) & ~same(size))
        size *= 2
    return eye, base, levels


def _unit_tri_inverse(mats, masks):
    assert INV_BASE == 8
    eye, base, levels = masks
    n = mats[0].shape[0]
    ps = [jnp.where(base, -a, 0.0) for a in mats]
    p2 = [_dot(p, p) for p in ps]
    p34 = [_dot(q, jnp.concatenate([p, q], axis=1)) for p, q in zip(ps, p2)]
    s3 = [eye + p + q + r[:, :n] for p, q, r in zip(ps, p2, p34)]
    ts = [s + _dot(s, r[:, n:]) for s, r in zip(s3, p34)]
    for off in levels:
        cs = [_dot(jnp.where(off, a, 0.0), t) for a, t in zip(mats, ts)]
        ts = [t - _dot(t, c) for t, c in zip(ts, cs)]
    return ts


def _cum_matrices():
    i = np.arange(CHUNK)
    fwd = (i[None, :] <= i[:, None]).astype(np.float32)
    return jnp.asarray(np.stack([fwd, fwd.T]), BF16)


def _rk_feat_body(x_ref, xp_ref, xn_ref, mod_ref, mu_ref, wrkv_ref, w1_ref, w2_ref, w0_ref,
                  a1_ref, a2_ref, a0_ref, g1_ref, g2_ref, kk_ref, ka_ref, rk_ref, e_ref, et_ref,
                  r_out, kkn_out, v_out, gate_out, bonus_out, lw_out, b_out, kd_out,
                  *, n_lat_tiles, tiles_per_batch, seq):
    i = pl.program_id(0)
    tm, d = x_ref.shape
    q = d // 4
    sh, sc = mod_ref[0:1, :], mod_ref[1:2, :]
    h = x_ref[...] * (1.0 + sc) + sh
    hp = xp_ref[...] * (1.0 + sc) + sh
    hn = xn_ref[...] * (1.0 + sc) + sh
    is_ctx = i >= n_lat_tiles
    t = lax.broadcasted_iota(jnp.int32, (tm, 1), 0)
    col = t & (GRID_W - 1)
    pos = (i % tiles_per_batch) * tm + t
    to_end = (tm - 1) - t

    def prev_tok(z):
        return pltpu.roll(z, 1, 0)

    def next_tok(z):
        return pltpu.roll(z, tm - 1, 0)

    h0, h1, h2, h3 = (h[:, k * q:(k + 1) * q] for k in range(4))
    up = jnp.concatenate([hp[:, 2 * q:3 * q], h2[:tm - GRID_W]], axis=0)
    down = jnp.concatenate([h3[GRID_W:], hn[:, 3 * q:]], axis=0)
    edge0 = jnp.where(is_ctx, t, col)
    edge1 = jnp.where(is_ctx, t, (GRID_W - 1) - col)
    edge2 = jnp.where(is_ctx, to_end, jnp.maximum(pos - (GRID_W - 1), 0))
    edge3 = jnp.where(is_ctx, to_end, jnp.maximum((seq - GRID_W) - pos, 0))
    s0 = jnp.where(edge0 == 0, 0.0, prev_tok(h0))
    s1 = jnp.where(edge1 == 0, 0.0, jnp.where(is_ctx, prev_tok(h1), next_tok(h1)))
    s2 = jnp.where(edge2 == 0, 0.0, jnp.where(is_ctx, next_tok(h2), up))
    s3 = jnp.where(edge3 == 0, 0.0, jnp.where(is_ctx, next_tok(h3), down))
    xx = jnp.concatenate([s0, s1, s2, s3], axis=-1) - h
    xr, xw, xk, xv, xa, xg = (h + xx * mu_ref[m:m + 1, :] for m in range(6))

    r = _dot(xr, wrkv_ref[0])
    k = _dot(xk, wrkv_ref[1])
    v = _dot(xv, wrkv_ref[2])
    dl = jnp.tanh(_dot(xw, w1_ref[...]))
    al = _dot(xa, a1_ref[...])
    kkr = k * kk_ref[...]
    ss = _segsum(kkr * kkr, e_ref, et_ref, split=False)
    kkn = kkr / jnp.maximum(jnp.sqrt(ss), 1e-12)
    kd_sum = jnp.zeros_like(k)
    for z in range(2):
        lw = -RK_DECAY_SCALE * _sigmoid(w0_ref[z:z + 1, :] + _dot(dl, w2_ref[z]))
        a = _sigmoid(a0_ref[z:z + 1, :] + _dot(al, a2_ref[z]))
        kd = k * (1.0 + (a - 1.0) * ka_ref[...])
        lw_out[z] = lw
        b_out[z] = kkn * a
        kd_out[z] = kd
        kd_sum = kd_sum + kd
    coef = _segsum(r * kd_sum * rk_ref[...], e_ref, et_ref)
    r_out[...] = r
    kkn_out[...] = kkn
    v_out[...] = v
    bonus_out[...] = coef * v
    gate_out[...] = _dot(_sigmoid(_dot(xg, g1_ref[...])), g2_ref[...])


def _rk_scan_chunks(streams, cum_ref, s_scr):
    c = CHUNK
    zero = jnp.float32(0.0)
    inv_masks = _inverse_masks(2 * c)
    lane = lax.broadcasted_iota(jnp.int32, (1, LANES), 1)
    m0, m1 = lane < c, lane >= c

    def pick(m_first, m_second, first, second):
        return jnp.where(m_first, first, jnp.where(m_second, second, zero))

    chains = []
    for direction, b, (r_ref, kk_ref, v_ref, lw_ref, b_ref, kd_ref, y_ref) in streams:
        lw = lw_ref[b]
        cs = _dot_rhs2(cum_ref[direction], lw)
        cs_last = cs[c - 1:c, :] if direction == 0 else cs[0:1, :]
        eg, en, el = jnp.exp(cs), jnp.exp(-cs), jnp.exp(cs_last - cs)
        rt = r_ref[b] * eg
        kkt = kk_ref[b] * jnp.exp(cs - lw)
        bb, kd, v = b_ref[b], kd_ref[b], v_ref[b]
        bt, kt, bh, kh = bb * en, kd * en, bb * el, kd * el
        gl = jnp.exp(cs_last)
        strict, incl, top, left = _order_masks(direction)
        tl, br = top & left, (~top) & (~left)
        tr, bl = top & (~left), (~top) & left
        masks = dict(bk=(tl & strict, br & strict), kk=(tr & strict, bl & strict),
                     rb=(bl & incl, tr & incl), rk=(br & incl, tl & incl), diag=top == left)
        for p in range(r_ref.shape[-1] // LANES):
            s = slice(p * LANES, (p + 1) * LANES)
            chains.append(dict(rt=rt[:, s], kkt=kkt[:, s], bt=bt[:, s], kt=kt[:, s], bh=bh[:, s],
                               kh=kh[:, s], v=v[:, s], gl=gl[:, s], m=masks, y_ref=y_ref,
                               where=(direction, b, p), lanes=s))

    s_old = [s_scr[ch["where"]] for ch in chains]
    wide = [_dot_nt(jnp.concatenate([ch["kkt"], ch["rt"]], 0),
                    jnp.concatenate([jnp.where(m0, jnp.concatenate([ch["bt"], ch["kt"]], 0), zero),
                                     jnp.where(m1, jnp.concatenate([ch["kt"], ch["bt"]], 0), zero), st], 0))
            for ch, st in zip(chains, s_old)]
    n2 = 2 * c
    r0 = [w[:, :n2] for w in wide]
    r1 = [jnp.concatenate([w[c:, n2:2 * n2], w[:c, n2:2 * n2]], 0) for w in wide]
    proj = [w[:, 2 * n2:] for w in wide]
    a_bk = [pick(*ch["m"]["bk"], x0, x1) for ch, x0, x1 in zip(chains, r0, r1)]
    a_kk = [pick(*ch["m"]["kk"], x0, x1) for ch, x0, x1 in zip(chains, r0, r1)]
    a_rbk = [jnp.concatenate([pick(*ch["m"]["rb"], x0, x1), pick(*ch["m"]["rk"], x0, x1)], 1)
             for ch, x0, x1 in zip(chains, r0, r1)]
    v_sw = [jnp.concatenate([jnp.where(m1, ch["v"], zero), jnp.where(m0, ch["v"], zero)], 0) for ch in chains]
    akkv = [_dot(a, x) for a, x in zip(a_kk, v_sw)]
    t_inv = _unit_tri_inverse(a_bk, inv_masks)
    ks_sm = [jnp.concatenate([jnp.where(m0, pj[:c], zero), jnp.where(m1, pj[:c], zero)], 0) for pj in proj]
    u_sm = [-_dot(t, k_ + a) for t, k_, a in zip(t_inv, ks_sm, akkv)]
    y_sm = [_dot(a, jnp.concatenate([u, x], 0)) for a, u, x in zip(a_rbk, u_sm, v_sw)]
    upd = [_dot_tn(jnp.concatenate([u[:c] + u[c:], ch["v"]], 0), jnp.concatenate([ch["bh"], ch["kh"]], 0))
           for ch, u in zip(chains, u_sm)]
    for i, ch in enumerate(chains):
        _, b, _ = ch["where"]
        ch["y_ref"][b, :, ch["lanes"]] = proj[i][c:] + y_sm[i][:c] + y_sm[i][c:]
        s_scr[ch["where"]] = s_old[i] * ch["gl"] + jnp.where(ch["m"]["diag"], upd[i], zero)


def _rk_scan_body(rf, rb, kkf, kkb, vf, vb, lwf, lwb, bf, bb, kdf, kdb, cum_ref, yf, yb, s_scr,
                  *, batches_per_trip):
    @pl.when(pl.program_id(0) == 0)
    def _():
        s_scr[...] = jnp.zeros(s_scr.shape, F32)

    fwd = (rf, kkf, vf, lwf, bf, kdf, yf)
    bwd = (rb, kkb, vb, lwb, bb, kdb, yb)

    def trip(i, carry):
        streams = []
        for k in range(batches_per_trip):
            b = i * batches_per_trip + k
            streams += [(0, b, fwd), (1, b, bwd)]
        _rk_scan_chunks(streams, cum_ref, s_scr)
        return carry

    lax.fori_loop(0, rf.shape[0] // batches_per_trip, trip, 0)


def _rk_out_body(x_ref, yf_ref, yb_ref, bonus_ref, gate_ref, mod_ref, lxg_ref, lxb_ref, wo_ref, e_ref,
                 et_ref, lng_ref, lnb_ref, o_ref, *, alpha, head):
    x = x_ref[...]
    y = yf_ref[...] + yb_ref[...]
    mean = _segsum(y, e_ref, et_ref) * (1.0 / head)
    cen = y - mean
    var = _segsum(cen * cen, e_ref, et_ref, split=False) * (1.0 / head)
    yn = cen * lax.rsqrt(var + RK_GN_EPS) * lxg_ref[...] + lxb_ref[...]
    out = _dot((yn + bonus_ref[...]) * gate_ref[...], wo_ref[...])
    gt = mod_ref[2:3, :]
    o_ref[...] = _layer_norm(alpha * x + gt * out, lng_ref[...], lnb_ref[...])


def _rwkv_layer(geom, x, mods, mu, w_rkv, w0, w1, w2, a0, a1, a2, g1, g2, k_k, k_a, r_k,
                lnx_g, lnx_b, w_o, ln_g, ln_b, alpha, want_ctx=True, tm=256):
    t, d = x.shape
    t_out = t if want_ctx else geom.n_lat
    head = r_k.shape[-1]
    assert 2 * head == LANES and geom.ctx == tm and geom.seq % tm == 0
    lora_d, lora_a, lora_g = w1.shape[-1], a1.shape[-1], g1.shape[-1]
    e, et = _head_indicator(d, head)

    def cat_dirs(w):
        return jnp.concatenate([w[0], w[1]], axis=-1).astype(BF16)

    def pad_dirs(w):
        z = jnp.zeros_like(w[0])
        return jnp.stack([jnp.concatenate([w[0], z], 0), jnp.concatenate([z, w[1]], 0)]).astype(BF16)

    nb = t // GRID_W
    nt = geom.ctx + geom.seq
    row = lambda i: (i, 0)
    tile = pl.BlockSpec((tm, d), row)
    scan_tile = pl.BlockSpec((None, tm, d), lambda i: (*_scan_tile_index(geom, i, tm), 0))
    scan_tile2 = pl.BlockSpec((2, None, tm, d), lambda i: (0, *_scan_tile_index(geom, i, tm), 0))
    mod_spec = pl.BlockSpec((None, 6, d), lambda i: (geom.mod_row(i, tm), 0, 0))
    per = tm // GRID_W
    feat = pl.pallas_call(
        functools.partial(_rk_feat_body, n_lat_tiles=geom.n_lat // tm,
                          tiles_per_batch=geom.seq // tm, seq=geom.seq),
        grid=(t // tm,),
        in_specs=[tile,
                  pl.BlockSpec((GRID_W, d), lambda i: (jnp.maximum(i * per - 1, 0), 0)),
                  pl.BlockSpec((GRID_W, d), lambda i: (jnp.minimum(i * per + per, nb - 1), 0)),
                  mod_spec, _full((6, d)), _resident((3, d, d)),
                  _full((d, 2 * lora_d)), _full((2, 2 * lora_d, d)), _full((2, d)),
                  _full((d, 2 * lora_a)), _full((2, 2 * lora_a, d)), _full((2, d)),
                  _full((d, lora_g)), _full((lora_g, d)),
                  _full((1, d)), _full((1, d)), _full((1, d)), _full((d, LANES)), _full((LANES, d))],
        out_specs=[scan_tile, scan_tile, scan_tile, tile, tile, scan_tile2, scan_tile2, scan_tile2],
        out_shape=([jax.ShapeDtypeStruct((geom.b, nt, d), F32)] * 3 + [jax.ShapeDtypeStruct((t, d), F32)] * 2
                   + [jax.ShapeDtypeStruct((2, geom.b, nt, d), F32)] * 3),
        compiler_params=_params(("parallel",)),
        name="rwkv_feat",
    )
    r, kkn, v, gate, bonus, lw, bb, kd = feat(
        x, x, x, mods, mu, w_rkv.astype(BF16), cat_dirs(w1), pad_dirs(w2), w0,
        cat_dirs(a1), pad_dirs(a2), a0, g1.astype(BF16), g2.astype(BF16),
        k_k.reshape(1, d), k_a.reshape(1, d), r_k.reshape(1, d), e, et)

    n_steps = nt // CHUNK
    shared = [pl.BlockSpec((geom.b, CHUNK, d), lambda s, z=z: (0, _scan_step_chunk(geom, z, s), 0))
              for z in range(2)]
    per_dir = [pl.BlockSpec((None, geom.b, CHUNK, d), lambda s, z=z: (z, 0, _scan_step_chunk(geom, z, s), 0))
               for z in range(2)]
    yf, yb = pl.pallas_call(
        functools.partial(_rk_scan_body, batches_per_trip=1),
        grid=(n_steps,),
        in_specs=shared * 3 + per_dir * 3 + [_full((2, CHUNK, CHUNK))],
        out_specs=shared,
        out_shape=[jax.ShapeDtypeStruct((geom.b, nt, d), F32)] * 2,
        scratch_shapes=[pltpu.VMEM((2, geom.b, d // LANES, LANES, LANES), F32)],
        compiler_params=_params(("arbitrary",)),
        name="rwkv_scan",
    )(r, r, kkn, kkn, v, v, lw, lw, bb, bb, kd, kd, _cum_matrices())

    return pl.pallas_call(
        functools.partial(_rk_out_body, alpha=alpha, head=head),
        grid=(t_out // tm,),
        in_specs=[tile, scan_tile, scan_tile, tile, tile, mod_spec, _full((1, d)), _full((1, d)),
                  _resident((d, d)), _full((d, LANES)), _full((LANES, d)), _full((1, d)), _full((1, d))],
        out_specs=tile,
        out_shape=jax.ShapeDtypeStruct((t_out, d), F32),
        compiler_params=_params(("parallel",)),
        name="rwkv_out",
    )(x, yf, yb, bonus, gate, mods, lnx_g.reshape(1, d), lnx_b.reshape(1, d), w_o.astype(BF16), e, et,
      ln_g.reshape(1, d), ln_b.reshape(1, d))


HALO = 8


def _gdn_feat_body(x_ref, xp_ref, xn_ref, mod_ref, wq_ref, wz_ref, wab_ref, cw_ref, alog_ref, dtb_ref,
                   q_out, k_out, v_out, z_out, gb_out, proj_scr,
                   *, n_lat_tiles, tiles_per_batch, head, n_taps):
    i = pl.program_id(0)
    tm, d = x_ref.shape
    sh, sc = mod_ref[0:1, :], mod_ref[1:2, :]
    h = x_ref[...] * (1.0 + sc) + sh
    is_ctx = i >= n_lat_tiles
    j = i % tiles_per_batch
    has_prev = jnp.logical_and(jnp.logical_not(is_ctx), j > 0)
    has_next = jnp.logical_and(jnp.logical_not(is_ctx), j < tiles_per_batch - 1)
    hp = jnp.where(has_prev, xp_ref[...] * (1.0 + sc) + sh, 0.0)
    hn = jnp.where(has_next, xn_ref[...] * (1.0 + sc) + sh, 0.0)
    hb = h.astype(BF16)
    proj_scr[...] = _dot(jnp.concatenate([hp, h, hn], axis=0), wq_ref[...])
    left = n_taps // 2
    conv = jnp.zeros((tm, 3 * d), F32)
    for tap in range(n_taps):
        conv = conv + proj_scr[pl.ds(HALO + tap - left, tm), :] * cw_ref[tap:tap + 1, :]
    qkv = _silu(conv)
    n_heads = d // head
    for hd in range(n_heads):
        qs = qkv[:, hd * head:(hd + 1) * head]
        ks = qkv[:, d + hd * head:d + (hd + 1) * head]
        q_out[:, hd * head:(hd + 1) * head] = (
            qs * lax.rsqrt(jnp.sum(qs * qs, axis=-1, keepdims=True) + GDN_L2_EPS) * head ** -0.5)
        k_out[:, hd * head:(hd + 1) * head] = (
            ks * lax.rsqrt(jnp.sum(ks * ks, axis=-1, keepdims=True) + GDN_L2_EPS))
    v_out[...] = qkv[:, 2 * d:]
    z_out[...] = jnp.dot(hb, wz_ref[...], preferred_element_type=F32)
    ab = _dot3(h, wab_ref[...])
    lane = lax.broadcasted_iota(jnp.int32, ab.shape, 1)
    g = -jnp.exp(alog_ref[...]) * _softplus(ab + dtb_ref[...])
    gb_out[...] = jnp.where(lane < 2 * n_heads, g, _sigmoid(ab))


def _gdn_scan_chunks(streams, s_scr, n_heads):
    c = CHUNK
    zero = jnp.float32(0.0)
    inv_masks = _inverse_masks(2 * c)
    row = lax.broadcasted_iota(jnp.int32, (2 * c, 1), 0)
    top_rows = row < c

    chains = []
    for direction, b, (q_ref, k_ref, v_ref, gb_ref, y_ref) in streams:
        head = q_ref.shape[-1] // n_heads
        strict, incl, top, left = _order_masks(direction)
        same = top == left
        masks = dict(strict=same & strict, incl=same & incl,
                     cum=jnp.where(same & incl, 1.0, 0.0).astype(BF16),
                     nxt=jnp.where(same & strict, 1.0, 0.0))
        first, last = (0, c - 1) if direction == 0 else (c - 1, 0)
        gb = gb_ref[b]
        col = lambda idx: gb[:, idx:idx + 1]
        for p in range(n_heads // 2):
            h0, h1 = 2 * p, 2 * p + 1
            stack = lambda ref: jnp.concatenate([ref[b, :, h0 * head:(h0 + 1) * head],
                                                 ref[b, :, h1 * head:(h1 + 1) * head]], 0)
            chains.append(dict(
                g=jnp.concatenate([col(direction * n_heads + h0), col(direction * n_heads + h1)], 0),
                beta=jnp.concatenate([col((2 + direction) * n_heads + h0),
                                      col((2 + direction) * n_heads + h1)], 0),
                q=stack(q_ref), k=stack(k_ref), v=stack(v_ref),
                m=masks, first=first, last=last, y_ref=y_ref, b=b, head=head,
                where=((direction, b, h0), (direction, b, h1)),
                lanes=(slice(h0 * head, (h0 + 1) * head), slice(h1 * head, (h1 + 1) * head))))

    dlog = [_dot_rhs2(ch["m"]["cum"], ch["g"] * ch["m"]["nxt"]) for ch in chains]
    cs, cs_l0, cs_l1 = [], [], []
    for ch, dl in zip(chains, dlog):
        f, l, g = ch["first"], ch["last"], ch["g"]
        x = jnp.where(top_rows, dl[:, f:f + 1] + g[f:f + 1], dl[:, c + f:c + f + 1] + g[c + f:c + f + 1])
        cs.append(x)
        cs_l0.append(x[l:l + 1])
        cs_l1.append(x[c + l:c + l + 1])
    cs_end = [jnp.where(top_rows, x0, x1) for x0, x1 in zip(cs_l0, cs_l1)]
    decay = [jnp.exp(jnp.where(ch["m"]["incl"], x, -jnp.inf)) for ch, x in zip(chains, dlog)]
    kb = [ch["k"] * ch["beta"] for ch in chains]
    vb = [ch["v"] * ch["beta"] for ch in chains]
    kkt = [_dot_nt(x, ch["k"]) for x, ch in zip(kb, chains)]
    qkt = [_dot_nt(ch["q"], ch["k"]) for ch in chains]
    lower = [jnp.where(ch["m"]["strict"], x * dc, zero) for ch, x, dc in zip(chains, kkt, decay)]
    a_qk = [jnp.where(ch["m"]["incl"], x * dc, zero) for ch, x, dc in zip(chains, qkt, decay)]
    t_inv = _unit_tri_inverse(lower, inv_masks)
    eg = [jnp.exp(x) for x in cs]
    uw = [_dot(t, jnp.concatenate([x, y * e], axis=1)) for t, x, y, e in zip(t_inv, vb, kb, eg)]
    s_old = [(s_scr[ch["where"][0]], s_scr[ch["where"][1]]) for ch in chains]
    ws = [jnp.concatenate([_dot(x[:c, ch["head"]:], s0), _dot(x[c:, ch["head"]:], s1)], 0)
          for ch, x, (s0, s1) in zip(chains, uw, s_old)]
    qs = [jnp.concatenate([_dot((ch["q"] * e)[:c], s0), _dot((ch["q"] * e)[c:], s1)], 0)
          for ch, e, (s0, s1) in zip(chains, eg, s_old)]
    v_new = [x[:, :ch["head"]] - y for ch, x, y in zip(chains, uw, ws)]
    o = [x + _dot(a, y) for x, a, y in zip(qs, a_qk, v_new)]
    k_dec = [ch["k"] * jnp.exp(ce - y) for ch, ce, y in zip(chains, cs_end, cs)]
    for i, ch in enumerate(chains):
        ch["y_ref"][ch["b"], :, ch["lanes"][0]] = o[i][:c]
        ch["y_ref"][ch["b"], :, ch["lanes"][1]] = o[i][c:]
        s_scr[ch["where"][0]] = s_old[i][0] * jnp.exp(cs_l0[i]) + _dot_tn(k_dec[i][:c], v_new[i][:c])
        s_scr[ch["where"][1]] = s_old[i][1] * jnp.exp(cs_l1[i]) + _dot_tn(k_dec[i][c:], v_new[i][c:])


def _gdn_scan_body(qf, qb, kf, kb, vf, vb, gf, gb, yf, yb, s_scr, *, n_heads, batches_per_trip):
    @pl.when(pl.program_id(0) == 0)
    def _():
        s_scr[...] = jnp.zeros(s_scr.shape, F32)

    fwd = (qf, kf, vf, gf, yf)
    bwd = (qb, kb, vb, gb, yb)

    def trip(i, carry):
        streams = []
        for j in range(batches_per_trip):
            b = i * batches_per_trip + j
            streams += [(0, b, fwd), (1, b, bwd)]
        _gdn_scan_chunks(streams, s_scr, n_heads)
        return carry

    lax.fori_loop(0, qf.shape[0] // batches_per_trip, trip, 0)


def _gdn_out_body(x_ref, of_ref, ob_ref, z_ref, mod_ref, nw_ref, wo_ref, lng_ref, lnb_ref, out_ref,
                  *, alpha, head):
    x = x_ref[...]
    o = of_ref[...] + ob_ref[...]
    z = z_ref[...]
    d = x.shape[1]
    parts = []
    for hd in range(d // head):
        oh = o[:, hd * head:(hd + 1) * head]
        on = oh * lax.rsqrt(jnp.mean(oh * oh, axis=-1, keepdims=True) + GDN_NORM_EPS) * nw_ref[...]
        parts.append(on * _silu(z[:, hd * head:(hd + 1) * head]))
    y = _dot(jnp.concatenate(parts, axis=-1), wo_ref[...])
    gt = mod_ref[2:3, :]
    out_ref[...] = _layer_norm(alpha * x + gt * y, lng_ref[...], lnb_ref[...])


def _gdn_layer(geom, x, mods, w_in, conv_w, a_log, dt_bias, norm_w, w_o, ln_g, ln_b, alpha, tm=256):
    t, d = x.shape
    n_heads = a_log.shape[-1]
    head = d // n_heads
    n_taps = conv_w.shape[0]
    assert head == LANES and geom.ctx == tm and geom.seq % tm == 0 and n_heads % 2 == 0
    assert 4 * n_heads <= LANES
    w_qkv = w_in[:, :3 * d].astype(BF16)
    w_z = w_in[:, 3 * d:4 * d].astype(BF16)
    w_ab = jnp.zeros((d, LANES), F32).at[:, :4 * n_heads].set(w_in[:, 4 * d:])
    alog = jnp.zeros((1, LANES), F32).at[0, :2 * n_heads].set(a_log.reshape(-1))
    dtb = jnp.zeros((1, LANES), F32).at[0, :2 * n_heads].set(dt_bias.reshape(-1))

    nb = t // HALO
    nt = geom.ctx + geom.seq
    per = tm // HALO
    tile = pl.BlockSpec((tm, d), lambda i: (i, 0))
    scan_tile = pl.BlockSpec((None, tm, d), lambda i: (*_scan_tile_index(geom, i, tm), 0))
    small = pl.BlockSpec((None, tm, LANES), lambda i: (*_scan_tile_index(geom, i, tm), 0))
    mod_spec = pl.BlockSpec((None, 6, d), lambda i: (geom.mod_row(i, tm), 0, 0))
    q, k, v, z, gb = pl.pallas_call(
        functools.partial(_gdn_feat_body, n_lat_tiles=geom.n_lat // tm,
                          tiles_per_batch=geom.seq // tm, head=head, n_taps=n_taps),
        grid=(t // tm,),
        in_specs=[tile,
                  pl.BlockSpec((HALO, d), lambda i: (jnp.maximum(i * per - 1, 0), 0)),
                  pl.BlockSpec((HALO, d), lambda i: (jnp.minimum(i * per + per, nb - 1), 0)),
                  mod_spec, _resident((d, 3 * d)), _resident((d, d)), _full((d, LANES)),
                  _full((n_taps, 3 * d)), _full((1, LANES)), _full((1, LANES))],
        out_specs=[scan_tile, scan_tile, scan_tile, tile, small],
        out_shape=([jax.ShapeDtypeStruct((geom.b, nt, d), F32)] * 3 + [jax.ShapeDtypeStruct((t, d), F32)]
                   + [jax.ShapeDtypeStruct((geom.b, nt, LANES), F32)]),
        scratch_shapes=[pltpu.VMEM((tm + 2 * HALO, 3 * d), F32)],
        compiler_params=_params(("parallel",)),
        name="gdn_feat",
    )(x, x, x, mods, w_qkv, w_z, w_ab, conv_w, alog, dtb)

    n_steps = nt // CHUNK
    shared = [pl.BlockSpec((geom.b, CHUNK, d), lambda s, zz=zz: (0, _scan_step_chunk(geom, zz, s), 0))
              for zz in range(2)]
    shared_small = [pl.BlockSpec((geom.b, CHUNK, LANES), lambda s, zz=zz: (0, _scan_step_chunk(geom, zz, s), 0))
                    for zz in range(2)]
    of, ob = pl.pallas_call(
        functools.partial(_gdn_scan_body, n_heads=n_heads, batches_per_trip=2),
        grid=(n_steps,),
        in_specs=shared * 3 + shared_small,
        out_specs=shared,
        out_shape=[jax.ShapeDtypeStruct((geom.b, nt, d), F32)] * 2,
        scratch_shapes=[pltpu.VMEM((2, geom.b, n_heads, head, head), F32)],
        compiler_params=_params(("arbitrary",)),
        name="gdn_scan",
    )(q, q, k, k, v, v, gb, gb)

    return pl.pallas_call(
        functools.partial(_gdn_out_body, alpha=alpha, head=head),
        grid=(t // tm,),
        in_specs=[tile, scan_tile, scan_tile, tile, mod_spec, _full((1, head)), _resident((d, d)),
                  _full((1, d)), _full((1, d))],
        out_specs=tile,
        out_shape=jax.ShapeDtypeStruct((t, d), F32),
        compiler_params=_params(("parallel",)),
        name="gdn_out",
    )(x, of, ob, z, mods, norm_w.reshape(1, head), w_o.astype(BF16), ln_g.reshape(1, d), ln_b.reshape(1, d))


def kernel(x, c, ctx, c_ctx, ada_w, ada_b, ln_g, ln_b, rk_mu, rk_w_rkv, rk_w0, rk_w1, rk_w2, rk_a0, rk_a1, rk_a2, rk_g1, rk_g2, rk_k_k, rk_k_a, rk_r_k, rk_lnx_g, rk_lnx_b, rk_w_o, pool_w, pool_scale, gdn_w_in, gdn_conv_w, gdn_a_log, gdn_dt_bias, gdn_norm_w, gdn_w_o, ffn_w1, ffn_w3, ffn_w2, moe_router_w, moe_router_b, moe_w1, moe_w3, moe_w2):
    batch, seq, d = x.shape
    ctx_len = ctx.shape[1]
    depth = ada_w.shape[0]
    geom = _Geom(batch, seq, ctx_len, d)
    alpha = (2 * depth) ** 0.25
    n_mixers = 3

    rows = -(-(batch + 1) // 8) * 8
    cvec = jnp.zeros((rows, d), F32).at[:batch].set(c).at[batch].set(c_ctx)
    table = _ada_table(cvec, ada_w, ada_b)
    mods_all = table[:, :batch + 1].reshape(depth, batch + 1, 6, d)

    xs = jnp.concatenate([x.reshape(batch * seq, d), ctx.reshape(batch * ctx_len, d)], axis=0)
    for i in range(depth):
        mods = mods_all[i]
        kind, j = i % n_mixers, i // n_mixers
        if kind == 0:
            xs = _rwkv_layer(geom, xs, mods, rk_mu[j], rk_w_rkv[j], rk_w0[j], rk_w1[j], rk_w2[j],
                             rk_a0[j], rk_a1[j], rk_a2[j], rk_g1[j], rk_g2[j], rk_k_k[j], rk_k_a[j],
                             rk_r_k[j], rk_lnx_g[j], rk_lnx_b[j], rk_w_o[j], ln_g[i, 0], ln_b[i, 0], alpha,
                             want_ctx=i < depth - 1)
        elif kind == 1:
            xs = _pool_layer(geom, xs, mods, pool_w[j], pool_scale[j], ln_g[i, 0], ln_b[i, 0], alpha)
        else:
            xs = _gdn_layer(geom, xs, mods, gdn_w_in[j], gdn_conv_w[j], gdn_a_log[j], gdn_dt_bias[j],
                            gdn_norm_w[j], gdn_w_o[j], ln_g[i, 0], ln_b[i, 0], alpha)
        e = i // 2
        if i % 2 == 0:
            xs = _ffn_layer(geom, xs, mods, ffn_w1[e], ffn_w3[e], ffn_w2[e], ln_g[i, 1], ln_b[i, 1], alpha)
        else:
            xs = _moe_layer(geom, xs, mods, moe_router_w[e], moe_router_b[e], moe_w1[e], moe_w3[e],
                            moe_w2[e], ln_g[i, 1], ln_b[i, 1], alpha)
    return xs[:batch * seq].reshape(batch, seq, d)
```

```python
import functools
import math

import jax
import jax.numpy as jnp
import numpy as np
from jax import lax
from jax.experimental import pallas as pl
from jax.experimental.pallas import tpu as pltpu

F32 = jnp.float32
BF16 = jnp.bfloat16

GRID_W = 64
CHUNK = 64
LANES = 128
POOL_WINDOWS = (2, 4, 8, 16)
LN_EPS = 1e-5
RK_DECAY_SCALE = math.exp(-0.5)
RK_GN_EPS = 64e-5
GDN_NORM_EPS = 1e-6
GDN_L2_EPS = 1e-6
TOP_K = 2
VMEM_LIMIT = 56 * 1024 * 1024


def _sigmoid(x):
    return 1.0 / (1.0 + jnp.exp(-x))


def _silu(x):
    return x * _sigmoid(x)


def _softplus(x):
    return jnp.maximum(x, 0.0) + jnp.log(1.0 + jnp.exp(-jnp.abs(x)))


def _dot(a, b):
    return jnp.dot(a.astype(BF16), b.astype(BF16), preferred_element_type=F32)


def _dot_nt(a, b):
    return lax.dot_general(a.astype(BF16), b.astype(BF16), (((1,), (1,)), ((), ())),
                           preferred_element_type=F32)


def _dot_tn(a, b):
    return lax.dot_general(a.astype(BF16), b.astype(BF16), (((0,), (0,)), ((), ())),
                           preferred_element_type=F32)


def _split(x):
    hi = x.astype(BF16)
    lo = (x - hi.astype(F32)).astype(BF16)
    return hi, lo


def _dot_lhs2(a, b_exact):
    hi, lo = _split(a)
    return (jnp.dot(hi, b_exact, preferred_element_type=F32)
            + jnp.dot(lo, b_exact, preferred_element_type=F32))


def _dot_rhs2(a_exact, b):
    hi, lo = _split(b)
    return (jnp.dot(a_exact, hi, preferred_element_type=F32)
            + jnp.dot(a_exact, lo, preferred_element_type=F32))


def _dot3(a, b):
    ah, al = _split(a)
    bh, bl = _split(b)
    return (jnp.dot(ah, bh, preferred_element_type=F32)
            + jnp.dot(ah, bl, preferred_element_type=F32)
            + jnp.dot(al, bh, preferred_element_type=F32))


def _segsum(x, e_ref, et_ref, split=True):
    if not split:
        return _dot(_dot(x, e_ref[...]), et_ref[...])
    s = _dot_lhs2(x, e_ref[...])
    return _dot_lhs2(s, et_ref[...])


def _layer_norm(v, g, b):
    mean = jnp.mean(v, axis=-1, keepdims=True)
    c = v - mean
    var = jnp.mean(c * c, axis=-1, keepdims=True)
    return c * lax.rsqrt(var + LN_EPS) * g + b


def _head_indicator(d_model, head):
    n_heads = d_model // head
    e = np.zeros((d_model, LANES), np.float32)
    e[np.arange(d_model), np.arange(d_model) // head] = 1.0
    assert n_heads <= LANES
    return jnp.asarray(e, BF16), jnp.asarray(e.T.copy(), BF16)


def _full(shape):
    n = len(shape)
    return pl.BlockSpec(shape, lambda *_: (0,) * n)


def _resident(shape):
    n = len(shape)
    return pl.BlockSpec(shape, lambda *_: (0,) * n, pipeline_mode=pl.Buffered(1))


def _params(sem):
    return pltpu.CompilerParams(dimension_semantics=sem, vmem_limit_bytes=VMEM_LIMIT)


class _Geom:
    def __init__(self, batch, seq, ctx_len, d_model):
        self.b, self.seq, self.ctx, self.d = batch, seq, ctx_len, d_model
        self.n_lat = batch * seq
        self.t = batch * (seq + ctx_len)

    def mod_row(self, i, tm):
        return jnp.where(i < self.n_lat // tm, i // (self.seq // tm), self.b)


def _ada_body(c_ref, w_ref, b_ref, o_ref):
    o_ref[...] = _dot3(_silu(c_ref[...]), w_ref[...]) + b_ref[...]


def _ada_table(cvec, ada_w, ada_b):
    depth, d, d6 = ada_w.shape
    rows = cvec.shape[0]
    return pl.pallas_call(
        _ada_body,
        grid=(depth, d6 // d),
        in_specs=[_full((rows, d)),
                  pl.BlockSpec((None, d, d), lambda l, j: (l, 0, j)),
                  pl.BlockSpec((None, 1, d), lambda l, j: (l, 0, j))],
        out_specs=pl.BlockSpec((None, rows, d), lambda l, j: (l, 0, j)),
        out_shape=jax.ShapeDtypeStruct((depth, rows, d6), F32),
        compiler_params=_params(("parallel", "parallel")),
        name="ada_table",
    )(cvec, ada_w, ada_b.reshape(depth, 1, d6))


def _ffn_body(x_ref, mod_ref, w13_ref, w2_ref, lng_ref, lnb_ref, o_ref, *, alpha, n_split):
    x = x_ref[...]
    sh, sc, gt = mod_ref[3:4, :], mod_ref[4:5, :], mod_ref[5:6, :]
    h = (x * (1.0 + sc) + sh).astype(BF16)
    ff = w2_ref.shape[0] // n_split
    acc = jnp.zeros(x.shape, F32)
    for s in range(n_split):
        ag = jnp.dot(h, w13_ref[:, 2 * s * ff:2 * (s + 1) * ff], preferred_element_type=F32)
        acc = acc + jnp.dot((_silu(ag[:, :ff]) * ag[:, ff:]).astype(BF16), w2_ref[s * ff:(s + 1) * ff, :],
                            preferred_element_type=F32)
    o_ref[...] = _layer_norm(alpha * x + gt * acc, lng_ref[...], lnb_ref[...])


def _ffn_layer(geom, x, mods, w1, w3, w2, ln_g, ln_b, alpha, tm=512, n_split=2):
    t, d = x.shape
    ff = w1.shape[1]
    ffs = ff // n_split
    w13 = jnp.concatenate([w[:, s * ffs:(s + 1) * ffs] for s in range(n_split) for w in (w1, w3)],
                          axis=1).astype(BF16)
    return pl.pallas_call(
        functools.partial(_ffn_body, alpha=alpha, n_split=n_split),
        grid=(t // tm,),
        in_specs=[pl.BlockSpec((tm, d), lambda i: (i, 0)),
                  pl.BlockSpec((None, 6, d), lambda i: (geom.mod_row(i, tm), 0, 0)),
                  _resident((d, 2 * ff)), _resident((ff, d)),
                  _full((1, d)), _full((1, d))],
        out_specs=pl.BlockSpec((tm, d), lambda i: (i, 0)),
        out_shape=jax.ShapeDtypeStruct((t, d), F32),
        compiler_params=_params(("parallel",)),
        name="ffn",
    )(x, mods, w13, w2.astype(BF16), ln_g.reshape(1, d), ln_b.reshape(1, d))


MOE_GROUP = 512
MOE_ROWS = 160
MOE_TAIL_ROWS = 64


def _moe_body(x_ref, mod_ref, rw_ref, rb_ref, tri_ref, w13_ref, w2_ref, lng_ref, lnb_ref, o_ref,
              h_scr, gate_scr, sel_scr, rank_scr, selt_scr, rankt_scr, acc_scr, *, alpha, n_experts):
    e = pl.program_id(1)
    tm = x_ref.shape[0]
    n_groups = tm // MOE_GROUP

    @pl.when(e == 0)
    def _():
        x = x_ref[...]
        sh, sc = mod_ref[3:4, :], mod_ref[4:5, :]
        h = x * (1.0 + sc) + sh
        h_scr[...] = h.astype(BF16)
        logits = _dot3(h, rw_ref[...]) + rb_ref[...]
        lane = lax.broadcasted_iota(jnp.int32, logits.shape, 1)
        neg = jnp.float32(-jnp.inf)
        logits = jnp.where(lane < n_experts, logits, neg)
        m1 = jnp.max(logits, axis=-1, keepdims=True)
        i1 = jnp.min(jnp.where(logits == m1, lane, LANES), axis=-1, keepdims=True)
        rest = jnp.where(lane == i1, neg, logits)
        m2 = jnp.max(rest, axis=-1, keepdims=True)
        i2 = jnp.min(jnp.where(rest == m2, lane, LANES), axis=-1, keepdims=True)
        e2 = jnp.exp(m2 - m1)
        p1 = 1.0 / (1.0 + e2)
        p2 = e2 / (1.0 + e2)
        gate_scr[...] = jnp.where(lane == i1, p1, 0.0) + jnp.where(lane == i2, p2, 0.0)
        sel = jnp.where(lane == i1, 1.0, 0.0) + jnp.where(lane == i2, 1.0, 0.0)
        sel_scr[...] = sel
        for s in range(n_groups):
            rows = slice(s * MOE_GROUP, (s + 1) * MOE_GROUP)
            rank = jnp.dot(tri_ref[...], sel[rows].astype(BF16), preferred_element_type=F32)
            rank_scr[rows, :] = rank
            rankt_scr[s] = rank.T
            selt_scr[s] = sel[rows].T
        acc_scr[...] = jnp.zeros(acc_scr.shape, F32)

    lane = lax.broadcasted_iota(jnp.int32, (MOE_GROUP, LANES), 1)
    column = lambda ref, rows: jnp.sum(jnp.where(lane == e, ref[rows, :], 0.0), axis=-1, keepdims=True)
    groups = []
    most = jnp.int32(0)
    for s in range(n_groups):
        rows = slice(s * MOE_GROUP, (s + 1) * MOE_GROUP)
        sel_col = column(sel_scr, rows) > 0.5
        groups.append(dict(rows=rows, gate_col=column(gate_scr, rows), sel_col=sel_col,
                           rank_col=column(rank_scr, rows),
                           sel_row=selt_scr[s, pl.ds(e, 1), :] > 0.5,
                           rank_row=rankt_scr[s, pl.ds(e, 1), :]))
        most = jnp.maximum(most, jnp.sum(jnp.where(sel_col, 1.0, 0.0)).astype(jnp.int32))

    def run_pass(first_slot, n_rows):
        base = first_slot.astype(F32)
        slot_col = lax.broadcasted_iota(jnp.int32, (n_rows, 1), 0).astype(F32) + base
        slot_row = lax.broadcasted_iota(jnp.int32, (1, n_rows), 1).astype(F32) + base
        xg = []
        for gr in groups:
            take = jnp.where((gr["rank_row"] == slot_col) & gr["sel_row"], 1.0, 0.0).astype(BF16)
            xg.append(jnp.dot(take, h_scr[gr["rows"], :], preferred_element_type=F32).astype(BF16))
        xg = jnp.concatenate(xg, axis=0)
        ag = jnp.dot(xg, w13_ref[...], preferred_element_type=F32)
        ffe = w2_ref.shape[0]
        y = jnp.dot((_silu(ag[:, :ffe]) * ag[:, ffe:]).astype(BF16), w2_ref[...], preferred_element_type=F32)
        for k, gr in enumerate(groups):
            put = jnp.where((gr["rank_col"] == slot_row) & gr["sel_col"], 1.0, 0.0).astype(BF16)
            acc_scr[gr["rows"], :] += gr["gate_col"] * _dot(put, y[k * n_rows:(k + 1) * n_rows])

    n_full = most // MOE_ROWS
    left = most - n_full * MOE_ROWS

    def full_pass(j, carry):
        run_pass(j * MOE_ROWS, MOE_ROWS)
        return carry

    lax.fori_loop(0, n_full, full_pass, 0)

    @pl.when(left > MOE_TAIL_ROWS)
    def _():
        run_pass(n_full * MOE_ROWS, MOE_ROWS)

    @pl.when(jnp.logical_and(left > 0, left <= MOE_TAIL_ROWS))
    def _():
        run_pass(n_full * MOE_ROWS, MOE_TAIL_ROWS)

    @pl.when(e == n_experts - 1)
    def _():
        x = x_ref[...]
        gt = mod_ref[5:6, :]
        o_ref[...] = _layer_norm(alpha * x + gt * acc_scr[...], lng_ref[...], lnb_ref[...])


def _moe_layer(geom, x, mods, router_w, router_b, w1, w3, w2, ln_g, ln_b, alpha, tm=1024):
    t, d = x.shape
    n_e, _, ffe = w1.shape
    assert tm % MOE_GROUP == 0 and n_e <= LANES
    rw = jnp.zeros((d, LANES), F32).at[:, :n_e].set(router_w)
    rb = jnp.zeros((1, LANES), F32).at[0, :n_e].set(router_b)
    idx = np.arange(MOE_GROUP)
    tri = jnp.asarray(idx[None, :] < idx[:, None], BF16)
    n_groups = tm // MOE_GROUP
    return pl.pallas_call(
        functools.partial(_moe_body, alpha=alpha, n_experts=n_e),
        grid=(t // tm, n_e),
        in_specs=[pl.BlockSpec((tm, d), lambda i, e: (i, 0)),
                  pl.BlockSpec((None, 6, d), lambda i, e: (geom.mod_row(i, tm), 0, 0)),
                  _full((d, LANES)), _full((1, LANES)), _full((MOE_GROUP, MOE_GROUP)),
                  pl.BlockSpec((None, d, 2 * ffe), lambda i, e: (e, 0, 0)),
                  pl.BlockSpec((None, ffe, d), lambda i, e: (e, 0, 0)),
                  _full((1, d)), _full((1, d))],
        out_specs=pl.BlockSpec((tm, d), lambda i, e: (i, 0)),
        out_shape=jax.ShapeDtypeStruct((t, d), F32),
        scratch_shapes=[pltpu.VMEM((tm, d), BF16), pltpu.VMEM((tm, LANES), F32),
                        pltpu.VMEM((tm, LANES), F32), pltpu.VMEM((tm, LANES), F32),
                        pltpu.VMEM((n_groups, LANES, MOE_GROUP), F32),
                        pltpu.VMEM((n_groups, LANES, MOE_GROUP), F32),
                        pltpu.VMEM((tm, d), F32)],
        compiler_params=_params(("parallel", "arbitrary")),
        name="moe",
    )(x, mods, rw, rb, tri, jnp.concatenate([w1, w3], axis=-1).astype(BF16), w2.astype(BF16),
      ln_g.reshape(1, d), ln_b.reshape(1, d))


def _pool_matrices(tm, seq_len):
    t = np.arange(tm)
    pos = t % seq_len
    mats, inv = [], []
    for win in POOL_WINDOWS:
        lo = np.clip(pos - win // 2, 0, seq_len)
        hi = np.clip(pos + win // 2, 0, seq_len)
        base = t - pos
        j = t[None, :]
        mats.append(((j >= (base + lo)[:, None]) & (j < (base + hi)[:, None])).astype(np.float32))
        inv.append((hi - lo).astype(np.float32))
    return np.stack(mats), np.stack(inv)


def _pool_body(x_ref, mod_ref, pm_ref, cnt_ref, pw_ref, ps_ref, lng_ref, lnb_ref, o_ref, *, alpha):
    x = x_ref[...]
    sh, sc, gt = mod_ref[0:1, :], mod_ref[1:2, :], mod_ref[2:3, :]
    h = x * (1.0 + sc) + sh
    n_g = pm_ref.shape[0]
    gw = x.shape[1] // n_g
    outs = []
    for g in range(n_g):
        hg = h[:, g * gw:(g + 1) * gw]
        total = _dot_rhs2(pm_ref[g], hg)
        pooled = total / cnt_ref[g] - hg
        outs.append(_dot(pooled, pw_ref[g]))
    y = jnp.concatenate(outs, axis=-1) * ps_ref[...]
    o_ref[...] = _layer_norm(alpha * x + gt * y, lng_ref[...], lnb_ref[...])


def _pool_layer(geom, x, mods, pool_w, pool_scale, ln_g, ln_b, alpha, tm=256):
    t, d = x.shape
    n_g = len(POOL_WINDOWS)
    m_lat, c_lat = _pool_matrices(tm, GRID_W)
    m_ctx, c_ctx = _pool_matrices(tm, geom.ctx)
    pm = jnp.asarray(np.stack([m_lat, m_ctx]), BF16)
    cnt = jnp.asarray(np.stack([c_lat, c_ctx])[..., None], F32)
    n_lat_tiles = geom.n_lat // tm
    kind = lambda i: jnp.where(i < n_lat_tiles, 0, 1)
    return pl.pallas_call(
        functools.partial(_pool_body, alpha=alpha),
        grid=(t // tm,),
        in_specs=[pl.BlockSpec((tm, d), lambda i: (i, 0)),
                  pl.BlockSpec((None, 6, d), lambda i: (geom.mod_row(i, tm), 0, 0)),
                  pl.BlockSpec((None, n_g, tm, tm), lambda i: (kind(i), 0, 0, 0)),
                  pl.BlockSpec((None, n_g, tm, 1), lambda i: (kind(i), 0, 0, 0)),
                  _full((n_g, d // n_g, d // n_g)), _full((1, d)), _full((1, d)), _full((1, d))],
        out_specs=pl.BlockSpec((tm, d), lambda i: (i, 0)),
        out_shape=jax.ShapeDtypeStruct((t, d), F32),
        compiler_params=_params(("parallel",)),
        name="pool",
    )(x, mods, pm, cnt, pool_w.astype(BF16), pool_scale.reshape(1, d),
      ln_g.reshape(1, d), ln_b.reshape(1, d))


def _scan_tile_index(geom, i, tm):
    n_lat_tiles = geom.n_lat // tm
    tpb = geom.seq // tm
    is_lat = i < n_lat_tiles
    return jnp.where(is_lat, i // tpb, i - n_lat_tiles), jnp.where(is_lat, geom.ctx // tm + i % tpb, 0)


def _scan_step_chunk(geom, direction, s):
    if direction == 0:
        return s
    nc_ctx = geom.ctx // CHUNK
    nc = (geom.ctx + geom.seq) // CHUNK
    return jnp.where(s < nc_ctx, nc_ctx - 1 - s, nc - 1 + nc_ctx - s)


def _order_masks(direction):
    n = 2 * CHUNK
    ri = lax.broadcasted_iota(jnp.int32, (n, n), 0)
    ci = lax.broadcasted_iota(jnp.int32, (n, n), 1)
    rt, ct = ri & (CHUNK - 1), ci & (CHUNK - 1)
    ahead = rt - ct if direction == 0 else ct - rt
    return ahead > 0, ahead >= 0, ri < CHUNK, ci < CHUNK


INV_BASE = 8


def _inverse_masks(n):
    ri = lax.broadcasted_iota(jnp.int32, (n, n), 0)
    ci = lax.broadcasted_iota(jnp.int32, (n, n), 1)
    eye = (ri == ci).astype(F32)
    same = lambda size: (ri // size) == (ci // size)
    base = same(INV_BASE)
    levels = []
    size = INV_BASE
    while size < CHUNK:
        levels.append(same(2 * size) & ~same(size))
        size *= 2
    return eye, base, levels


def _unit_tri_inverse(mats, masks):
    assert INV_BASE == 8
    eye, base, levels = masks
    n = mats[0].shape[0]
    ps = [jnp.where(base, -a, 0.0) for a in mats]
    p2 = [_dot(p, p) for p in ps]
    p34 = [_dot(q, jnp.concatenate([p, q], axis=1)) for p, q in zip(ps, p2)]
    s3 = [eye + p + q + r[:, :n] for p, q, r in zip(ps, p2, p34)]
    ts = [s + _dot(s, r[:, n:]) for s, r in zip(s3, p34)]
    for off in levels:
        cs = [_dot(jnp.where(off, a, 0.0), t) for a, t in zip(mats, ts)]
        ts = [t - _dot(t, c) for t, c in zip(ts, cs)]
    return ts


def _cum_matrices():
    i = np.arange(CHUNK)
    fwd = (i[None, :] <= i[:, None]).astype(np.float32)
    return jnp.asarray(np.stack([fwd, fwd.T]), BF16)


def _rk_feat_body(x_ref, xp_ref, xn_ref, mod_ref, mu_ref, wrkv_ref, w1_ref, w2_ref, w0_ref,
                  a1_ref, a2_ref, a0_ref, g1_ref, g2_ref, kk_ref, ka_ref, rk_ref, e_ref, et_ref,
                  r_out, kkn_out, v_out, gate_out, bonus_out, lw_out, b_out, kd_out,
                  *, n_lat_tiles, tiles_per_batch, seq):
    i = pl.program_id(0)
    tm, d = x_ref.shape
    q = d // 4
    sh, sc = mod_ref[0:1, :], mod_ref[1:2, :]
    h = x_ref[...] * (1.0 + sc) + sh
    hp = xp_ref[...] * (1.0 + sc) + sh
    hn = xn_ref[...] * (1.0 + sc) + sh
    is_ctx = i >= n_lat_tiles
    t = lax.broadcasted_iota(jnp.int32, (tm, 1), 0)
    col = t & (GRID_W - 1)
    pos = (i % tiles_per_batch) * tm + t
    to_end = (tm - 1) - t

    def prev_tok(z):
        return pltpu.roll(z, 1, 0)

    def next_tok(z):
        return pltpu.roll(z, tm - 1, 0)

    h0, h1, h2, h3 = (h[:, k * q:(k + 1) * q] for k in range(4))
    up = jnp.concatenate([hp[:, 2 * q:3 * q], h2[:tm - GRID_W]], axis=0)
    down = jnp.concatenate([h3[GRID_W:], hn[:, 3 * q:]], axis=0)
    edge0 = jnp.where(is_ctx, t, col)
    edge1 = jnp.where(is_ctx, t, (GRID_W - 1) - col)
    edge2 = jnp.where(is_ctx, to_end, jnp.maximum(pos - (GRID_W - 1), 0))
    edge3 = jnp.where(is_ctx, to_end, jnp.maximum((seq - GRID_W) - pos, 0))
    s0 = jnp.where(edge0 == 0, 0.0, prev_tok(h0))
    s1 = jnp.where(edge1 == 0, 0.0, jnp.where(is_ctx, prev_tok(h1), next_tok(h1)))
    s2 = jnp.where(edge2 == 0, 0.0, jnp.where(is_ctx, next_tok(h2), up))
    s3 = jnp.where(edge3 == 0, 0.0, jnp.where(is_ctx, next_tok(h3), down))
    xx = jnp.concatenate([s0, s1, s2, s3], axis=-1) - h
    xr, xw, xk, xv, xa, xg = (h + xx * mu_ref[m:m + 1, :] for m in range(6))

    r = _dot(xr, wrkv_ref[0])
    k = _dot(xk, wrkv_ref[1])
    v = _dot(xv, wrkv_ref[2])
    dl = jnp.tanh(_dot(xw, w1_ref[...]))
    al = _dot(xa, a1_ref[...])
    kkr = k * kk_ref[...]
    ss = _segsum(kkr * kkr, e_ref, et_ref, split=False)
    kkn = kkr / jnp.maximum(jnp.sqrt(ss), 1e-12)
    kd_sum = jnp.zeros_like(k)
    for z in range(2):
        lw = -RK_DECAY_SCALE * _sigmoid(w0_ref[z:z + 1, :] + _dot(dl, w2_ref[z]))
        a = _sigmoid(a0_ref[z:z + 1, :] + _dot(al, a2_ref[z]))
        kd = k * (1.0 + (a - 1.0) * ka_ref[...])
        lw_out[z] = lw
        b_out[z] = kkn * a
        kd_out[z] = kd
        kd_sum = kd_sum + kd
    coef = _segsum(r * kd_sum * rk_ref[...], e_ref, et_ref)
    r_out[...] = r
    kkn_out[...] = kkn
    v_out[...] = v
    bonus_out[...] = coef * v
    gate_out[...] = _dot(_sigmoid(_dot(xg, g1_ref[...])), g2_ref[...])


def _rk_scan_chunks(streams, cum_ref, s_scr):
    c = CHUNK
    zero = jnp.float32(0.0)
    inv_masks = _inverse_masks(2 * c)
    lane = lax.broadcasted_iota(jnp.int32, (1, LANES), 1)
    m0, m1 = lane < c, lane >= c

    def pick(m_first, m_second, first, second):
        return jnp.where(m_first, first, jnp.where(m_second, second, zero))

    chains = []
    for direction, b, (r_ref, kk_ref, v_ref, lw_ref, b_ref, kd_ref, y_ref) in streams:
        lw = lw_ref[b]
        cs = _dot_rhs2(cum_ref[direction], lw)
        cs_last = cs[c - 1:c, :] if direction == 0 else cs[0:1, :]
        eg, en, el = jnp.exp(cs), jnp.exp(-cs), jnp.exp(cs_last - cs)
        rt = r_ref[b] * eg
        kkt = kk_ref[b] * jnp.exp(cs - lw)
        bb, kd, v = b_ref[b], kd_ref[b], v_ref[b]
        bt, kt, bh, kh = bb * en, kd * en, bb * el, kd * el
        gl = jnp.exp(cs_last)
        strict, incl, top, left = _order_masks(direction)
        tl, br = top & left, (~top) & (~left)
        tr, bl = top & (~left), (~top) & left
        masks = dict(bk=(tl & strict, br & strict), kk=(tr & strict, bl & strict),
                     rb=(bl & incl, tr & incl), rk=(br & incl, tl & incl), diag=top == left)
        for p in range(r_ref.shape[-1] // LANES):
            s = slice(p * LANES, (p + 1) * LANES)
            chains.append(dict(rt=rt[:, s], kkt=kkt[:, s], bt=bt[:, s], kt=kt[:, s], bh=bh[:, s],
                               kh=kh[:, s], v=v[:, s], gl=gl[:, s], m=masks, y_ref=y_ref,
                               where=(direction, b, p), lanes=s))

    s_old = [s_scr[ch["where"]] for ch in chains]
    wide = [_dot_nt(jnp.concatenate([ch["kkt"], ch["rt"]], 0),
                    jnp.concatenate([jnp.where(m0, jnp.concatenate([ch["bt"], ch["kt"]], 0), zero),
                                     jnp.where(m1, jnp.concatenate([ch["kt"], ch["bt"]], 0), zero), st], 0))
            for ch, st in zip(chains, s_old)]
    n2 = 2 * c
    r0 = [w[:, :n2] for w in wide]
    r1 = [jnp.concatenate([w[c:, n2:2 * n2], w[:c, n2:2 * n2]], 0) for w in wide]
    proj = [w[:, 2 * n2:] for w in wide]
    a_bk = [pick(*ch["m"]["bk"], x0, x1) for ch, x0, x1 in zip(chains, r0, r1)]
    a_kk = [pick(*ch["m"]["kk"], x0, x1) for ch, x0, x1 in zip(chains, r0, r1)]
    a_rbk = [jnp.concatenate([pick(*ch["m"]["rb"], x0, x1), pick(*ch["m"]["rk"], x0, x1)], 1)
             for ch, x0, x1 in zip(chains, r0, r1)]
    v_sw = [jnp.concatenate([jnp.where(m1, ch["v"], zero), jnp.where(m0, ch["v"], zero)], 0) for ch in chains]
    akkv = [_dot(a, x) for a, x in zip(a_kk, v_sw)]
    t_inv = _unit_tri_inverse(a_bk, inv_masks)
    ks_sm = [jnp.concatenate([jnp.where(m0, pj[:c], zero), jnp.where(m1, pj[:c], zero)], 0) for pj in proj]
    u_sm = [-_dot(t, k_ + a) for t, k_, a in zip(t_inv, ks_sm, akkv)]
    y_sm = [_dot(a, jnp.concatenate([u, x], 0)) for a, u, x in zip(a_rbk, u_sm, v_sw)]
    upd = [_dot_tn(jnp.concatenate([u[:c] + u[c:], ch["v"]], 0), jnp.concatenate([ch["bh"], ch["kh"]], 0))
           for ch, u in zip(chains, u_sm)]
    for i, ch in enumerate(chains):
        _, b, _ = ch["where"]
        ch["y_ref"][b, :, ch["lanes"]] = proj[i][c:] + y_sm[i][:c] + y_sm[i][c:]
        s_scr[ch["where"]] = s_old[i] * ch["gl"] + jnp.where(ch["m"]["diag"], upd[i], zero)


def _rk_scan_body(rf, rb, kkf, kkb, vf, vb, lwf, lwb, bf, bb, kdf, kdb, cum_ref, yf, yb, s_scr,
                  *, batches_per_trip):
    @pl.when(pl.program_id(0) == 0)
    def _():
        s_scr[...] = jnp.zeros(s_scr.shape, F32)

    fwd = (rf, kkf, vf, lwf, bf, kdf, yf)
    bwd = (rb, kkb, vb, lwb, bb, kdb, yb)

    def trip(i, carry):
        streams = []
        for k in range(batches_per_trip):
            b = i * batches_per_trip + k
            streams += [(0, b, fwd), (1, b, bwd)]
        _rk_scan_chunks(streams, cum_ref, s_scr)
        return carry

    lax.fori_loop(0, rf.shape[0] // batches_per_trip, trip, 0)


def _rk_out_body(x_ref, yf_ref, yb_ref, bonus_ref, gate_ref, mod_ref, lxg_ref, lxb_ref, wo_ref, e_ref,
                 et_ref, lng_ref, lnb_ref, o_ref, *, alpha, head):
    x = x_ref[...]
    y = yf_ref[...] + yb_ref[...]
    mean = _segsum(y, e_ref, et_ref) * (1.0 / head)
    cen = y - mean
    var = _segsum(cen * cen, e_ref, et_ref, split=False) * (1.0 / head)
    yn = cen * lax.rsqrt(var + RK_GN_EPS) * lxg_ref[...] + lxb_ref[...]
    out = _dot((yn + bonus_ref[...]) * gate_ref[...], wo_ref[...])
    gt = mod_ref[2:3, :]
    o_ref[...] = _layer_norm(alpha * x + gt * out, lng_ref[...], lnb_ref[...])


def _rwkv_layer(geom, x, mods, mu, w_rkv, w0, w1, w2, a0, a1, a2, g1, g2, k_k, k_a, r_k,
                lnx_g, lnx_b, w_o, ln_g, ln_b, alpha, want_ctx=True, tm=256):
    t, d = x.shape
    t_out = t if want_ctx else geom.n_lat
    head = r_k.shape[-1]
    assert 2 * head == LANES and geom.ctx == tm and geom.seq % tm == 0
    lora_d, lora_a, lora_g = w1.shape[-1], a1.shape[-1], g1.shape[-1]
    e, et = _head_indicator(d, head)

    def cat_dirs(w):
        return jnp.concatenate([w[0], w[1]], axis=-1).astype(BF16)

    def pad_dirs(w):
        z = jnp.zeros_like(w[0])
        return jnp.stack([jnp.concatenate([w[0], z], 0), jnp.concatenate([z, w[1]], 0)]).astype(BF16)

    nb = t // GRID_W
    nt = geom.ctx + geom.seq
    row = lambda i: (i, 0)
    tile = pl.BlockSpec((tm, d), row)
    scan_tile = pl.BlockSpec((None, tm, d), lambda i: (*_scan_tile_index(geom, i, tm), 0))
    scan_tile2 = pl.BlockSpec((2, None, tm, d), lambda i: (0, *_scan_tile_index(geom, i, tm), 0))
    mod_spec = pl.BlockSpec((None, 6, d), lambda i: (geom.mod_row(i, tm), 0, 0))
    per = tm // GRID_W
    feat = pl.pallas_call(
        functools.partial(_rk_feat_body, n_lat_tiles=geom.n_lat // tm,
                          tiles_per_batch=geom.seq // tm, seq=geom.seq),
        grid=(t // tm,),
        in_specs=[tile,
                  pl.BlockSpec((GRID_W, d), lambda i: (jnp.maximum(i * per - 1, 0), 0)),
                  pl.BlockSpec((GRID_W, d), lambda i: (jnp.minimum(i * per + per, nb - 1), 0)),
                  mod_spec, _full((6, d)), _resident((3, d, d)),
                  _full((d, 2 * lora_d)), _full((2, 2 * lora_d, d)), _full((2, d)),
                  _full((d, 2 * lora_a)), _full((2, 2 * lora_a, d)), _full((2, d)),
                  _full((d, lora_g)), _full((lora_g, d)),
                  _full((1, d)), _full((1, d)), _full((1, d)), _full((d, LANES)), _full((LANES, d))],
        out_specs=[scan_tile, scan_tile, scan_tile, tile, tile, scan_tile2, scan_tile2, scan_tile2],
        out_shape=([jax.ShapeDtypeStruct((geom.b, nt, d), F32)] * 3 + [jax.ShapeDtypeStruct((t, d), F32)] * 2
                   + [jax.ShapeDtypeStruct((2, geom.b, nt, d), F32)] * 3),
        compiler_params=_params(("parallel",)),
        name="rwkv_feat",
    )
    r, kkn, v, gate, bonus, lw, bb, kd = feat(
        x, x, x, mods, mu, w_rkv.astype(BF16), cat_dirs(w1), pad_dirs(w2), w0,
        cat_dirs(a1), pad_dirs(a2), a0, g1.astype(BF16), g2.astype(BF16),
        k_k.reshape(1, d), k_a.reshape(1, d), r_k.reshape(1, d), e, et)

    n_steps = nt // CHUNK
    shared = [pl.BlockSpec((geom.b, CHUNK, d), lambda s, z=z: (0, _scan_step_chunk(geom, z, s), 0))
              for z in range(2)]
    per_dir = [pl.BlockSpec((None, geom.b, CHUNK, d), lambda s, z=z: (z, 0, _scan_step_chunk(geom, z, s), 0))
               for z in range(2)]
    yf, yb = pl.pallas_call(
        functools.partial(_rk_scan_body, batches_per_trip=1),
        grid=(n_steps,),
        in_specs=shared * 3 + per_dir * 3 + [_full((2, CHUNK, CHUNK))],
        out_specs=shared,
        out_shape=[jax.ShapeDtypeStruct((geom.b, nt, d), F32)] * 2,
        scratch_shapes=[pltpu.VMEM((2, geom.b, d // LANES, LANES, LANES), F32)],
        compiler_params=_params(("arbitrary",)),
        name="rwkv_scan",
    )(r, r, kkn, kkn, v, v, lw, lw, bb, bb, kd, kd, _cum_matrices())

    return pl.pallas_call(
        functools.partial(_rk_out_body, alpha=alpha, head=head),
        grid=(t_out // tm,),
        in_specs=[tile, scan_tile, scan_tile, tile, tile, mod_spec, _full((1, d)), _full((1, d)),
                  _resident((d, d)), _full((d, LANES)), _full((LANES, d)), _full((1, d)), _full((1, d))],
        out_specs=tile,
        out_shape=jax.ShapeDtypeStruct((t_out, d), F32),
        compiler_params=_params(("parallel",)),
        name="rwkv_out",
    )(x, yf, yb, bonus, gate, mods, lnx_g.reshape(1, d), lnx_b.reshape(1, d), w_o.astype(BF16), e, et,
      ln_g.reshape(1, d), ln_b.reshape(1, d))


HALO = 8


def _gdn_feat_body(x_ref, xp_ref, xn_ref, mod_ref, wq_ref, wz_ref, wab_ref, cw_ref, alog_ref, dtb_ref,
                   q_out, k_out, v_out, z_out, gb_out, proj_scr,
                   *, n_lat_tiles, tiles_per_batch, head, n_taps):
    i = pl.program_id(0)
    tm, d = x_ref.shape
    sh, sc = mod_ref[0:1, :], mod_ref[1:2, :]
    h = x_ref[...] * (1.0 + sc) + sh
    is_ctx = i >= n_lat_tiles
    j = i % tiles_per_batch
    has_prev = jnp.logical_and(jnp.logical_not(is_ctx), j > 0)
    has_next = jnp.logical_and(jnp.logical_not(is_ctx), j < tiles_per_batch - 1)
    hp = jnp.where(has_prev, xp_ref[...] * (1.0 + sc) + sh, 0.0)
    hn = jnp.where(has_next, xn_ref[...] * (1.0 + sc) + sh, 0.0)
    hb = h.astype(BF16)
    proj_scr[...] = _dot(jnp.concatenate([hp, h, hn], axis=0), wq_ref[...])
    left = n_taps // 2
    conv = jnp.zeros((tm, 3 * d), F32)
    for tap in range(n_taps):
        conv = conv + proj_scr[pl.ds(HALO + tap - left, tm), :] * cw_ref[tap:tap + 1, :]
    qkv = _silu(conv)
    n_heads = d // head
    for hd in range(n_heads):
        qs = qkv[:, hd * head:(hd + 1) * head]
        ks = qkv[:, d + hd * head:d + (hd + 1) * head]
        q_out[:, hd * head:(hd + 1) * head] = (
            qs * lax.rsqrt(jnp.sum(qs * qs, axis=-1, keepdims=True) + GDN_L2_EPS) * head ** -0.5)
        k_out[:, hd * head:(hd + 1) * head] = (
            ks * lax.rsqrt(jnp.sum(ks * ks, axis=-1, keepdims=True) + GDN_L2_EPS))
    v_out[...] = qkv[:, 2 * d:]
    z_out[...] = jnp.dot(hb, wz_ref[...], preferred_element_type=F32)
    ab = _dot3(h, wab_ref[...])
    lane = lax.broadcasted_iota(jnp.int32, ab.shape, 1)
    g = -jnp.exp(alog_ref[...]) * _softplus(ab + dtb_ref[...])
    gb_out[...] = jnp.where(lane < 2 * n_heads, g, _sigmoid(ab))


def _gdn_scan_chunks(streams, s_scr, n_heads):
    c = CHUNK
    zero = jnp.float32(0.0)
    inv_masks = _inverse_masks(2 * c)
    row = lax.broadcasted_iota(jnp.int32, (2 * c, 1), 0)
    top_rows = row < c

    chains = []
    for direction, b, (q_ref, k_ref, v_ref, gb_ref, y_ref) in streams:
        head = q_ref.shape[-1] // n_heads
        strict, incl, top, left = _order_masks(direction)
        same = top == left
        masks = dict(strict=same & strict, incl=same & incl,
                     cum=jnp.where(same & incl, 1.0, 0.0).astype(BF16),
                     nxt=jnp.where(same & strict, 1.0, 0.0))
        first, last = (0, c - 1) if direction == 0 else (c - 1, 0)
        gb = gb_ref[b]
        col = lambda idx: gb[:, idx:idx + 1]
        for p in range(n_heads // 2):
            h0, h1 = 2 * p, 2 * p + 1
            stack = lambda ref: jnp.concatenate([ref[b, :, h0 * head:(h0 + 1) * head],
                                                 ref[b, :, h1 * head:(h1 + 1) * head]], 0)
            chains.append(dict(
                g=jnp.concatenate([col(direction * n_heads + h0), col(direction * n_heads + h1)], 0),
                beta=jnp.concatenate([col((2 + direction) * n_heads + h0),
                                      col((2 + direction) * n_heads + h1)], 0),
                q=stack(q_ref), k=stack(k_ref), v=stack(v_ref),
                m=masks, first=first, last=last, y_ref=y_ref, b=b, head=head,
                where=((direction, b, h0), (direction, b, h1)),
                lanes=(slice(h0 * head, (h0 + 1) * head), slice(h1 * head, (h1 + 1) * head))))

    dlog = [_dot_rhs2(ch["m"]["cum"], ch["g"] * ch["m"]["nxt"]) for ch in chains]
    cs, cs_l0, cs_l1 = [], [], []
    for ch, dl in zip(chains, dlog):
        f, l, g = ch["first"], ch["last"], ch["g"]
        x = jnp.where(top_rows, dl[:, f:f + 1] + g[f:f + 1], dl[:, c + f:c + f + 1] + g[c + f:c + f + 1])
        cs.append(x)
        cs_l0.append(x[l:l + 1])
        cs_l1.append(x[c + l:c + l + 1])
    cs_end = [jnp.where(top_rows, x0, x1) for x0, x1 in zip(cs_l0, cs_l1)]
    decay = [jnp.exp(jnp.where(ch["m"]["incl"], x, -jnp.inf)) for ch, x in zip(chains, dlog)]
    kb = [ch["k"] * ch["beta"] for ch in chains]
    vb = [ch["v"] * ch["beta"] for ch in chains]
    kkt = [_dot_nt(x, ch["k"]) for x, ch in zip(kb, chains)]
    qkt = [_dot_nt(ch["q"], ch["k"]) for ch in chains]
    lower = [jnp.where(ch["m"]["strict"], x * dc, zero) for ch, x, dc in zip(chains, kkt, decay)]
    a_qk = [jnp.where(ch["m"]["incl"], x * dc, zero) for ch, x, dc in zip(chains, qkt, decay)]
    t_inv = _unit_tri_inverse(lower, inv_masks)
    eg = [jnp.exp(x) for x in cs]
    uw = [_dot(t, jnp.concatenate([x, y * e], axis=1)) for t, x, y, e in zip(t_inv, vb, kb, eg)]
    s_old = [(s_scr[ch["where"][0]], s_scr[ch["where"][1]]) for ch in chains]
    ws = [jnp.concatenate([_dot(x[:c, ch["head"]:], s0), _dot(x[c:, ch["head"]:], s1)], 0)
          for ch, x, (s0, s1) in zip(chains, uw, s_old)]
    qs = [jnp.concatenate([_dot((ch["q"] * e)[:c], s0), _dot((ch["q"] * e)[c:], s1)], 0)
          for ch, e, (s0, s1) in zip(chains, eg, s_old)]
    v_new = [x[:, :ch["head"]] - y for ch, x, y in zip(chains, uw, ws)]
    o = [x + _dot(a, y) for x, a, y in zip(qs, a_qk, v_new)]
    k_dec = [ch["k"] * jnp.exp(ce - y) for ch, ce, y in zip(chains, cs_end, cs)]
    for i, ch in enumerate(chains):
        ch["y_ref"][ch["b"], :, ch["lanes"][0]] = o[i][:c]
        ch["y_ref"][ch["b"], :, ch["lanes"][1]] = o[i][c:]
        s_scr[ch["where"][0]] = s_old[i][0] * jnp.exp(cs_l0[i]) + _dot_tn(k_dec[i][:c], v_new[i][:c])
        s_scr[ch["where"][1]] = s_old[i][1] * jnp.exp(cs_l1[i]) + _dot_tn(k_dec[i][c:], v_new[i][c:])


def _gdn_scan_body(qf, qb, kf, kb, vf, vb, gf, gb, yf, yb, s_scr, *, n_heads, batches_per_trip):
    @pl.when(pl.program_id(0) == 0)
    def _():
        s_scr[...] = jnp.zeros(s_scr.shape, F32)

    fwd = (qf, kf, vf, gf, yf)
    bwd = (qb, kb, vb, gb, yb)

    def trip(i, carry):
        streams = []
        for j in range(batches_per_trip):
            b = i * batches_per_trip + j
            streams += [(0, b, fwd), (1, b, bwd)]
        _gdn_scan_chunks(streams, s_scr, n_heads)
        return carry

    lax.fori_loop(0, qf.shape[0] // batches_per_trip, trip, 0)


def _gdn_out_body(x_ref, of_ref, ob_ref, z_ref, mod_ref, nw_ref, wo_ref, lng_ref, lnb_ref, out_ref,
                  *, alpha, head):
    x = x_ref[...]
    o = of_ref[...] + ob_ref[...]
    z = z_ref[...]
    d = x.shape[1]
    parts = []
    for hd in range(d // head):
        oh = o[:, hd * head:(hd + 1) * head]
        on = oh * lax.rsqrt(jnp.mean(oh * oh, axis=-1, keepdims=True) + GDN_NORM_EPS) * nw_ref[...]
        parts.append(on * _silu(z[:, hd * head:(hd + 1) * head]))
    y = _dot(jnp.concatenate(parts, axis=-1), wo_ref[...])
    gt = mod_ref[2:3, :]
    out_ref[...] = _layer_norm(alpha * x + gt * y, lng_ref[...], lnb_ref[...])


def _gdn_layer(geom, x, mods, w_in, conv_w, a_log, dt_bias, norm_w, w_o, ln_g, ln_b, alpha, tm=256):
    t, d = x.shape
    n_heads = a_log.shape[-1]
    head = d // n_heads
    n_taps = conv_w.shape[0]
    assert head == LANES and geom.ctx == tm and geom.seq % tm == 0 and n_heads % 2 == 0
    assert 4 * n_heads <= LANES
    w_qkv = w_in[:, :3 * d].astype(BF16)
    w_z = w_in[:, 3 * d:4 * d].astype(BF16)
    w_ab = jnp.zeros((d, LANES), F32).at[:, :4 * n_heads].set(w_in[:, 4 * d:])
    alog = jnp.zeros((1, LANES), F32).at[0, :2 * n_heads].set(a_log.reshape(-1))
    dtb = jnp.zeros((1, LANES), F32).at[0, :2 * n_heads].set(dt_bias.reshape(-1))

    nb = t // HALO
    nt = geom.ctx + geom.seq
    per = tm // HALO
    tile = pl.BlockSpec((tm, d), lambda i: (i, 0))
    scan_tile = pl.BlockSpec((None, tm, d), lambda i: (*_scan_tile_index(geom, i, tm), 0))
    small = pl.BlockSpec((None, tm, LANES), lambda i: (*_scan_tile_index(geom, i, tm), 0))
    mod_spec = pl.BlockSpec((None, 6, d), lambda i: (geom.mod_row(i, tm), 0, 0))
    q, k, v, z, gb = pl.pallas_call(
        functools.partial(_gdn_feat_body, n_lat_tiles=geom.n_lat // tm,
                          tiles_per_batch=geom.seq // tm, head=head, n_taps=n_taps),
        grid=(t // tm,),
        in_specs=[tile,
                  pl.BlockSpec((HALO, d), lambda i: (jnp.maximum(i * per - 1, 0), 0)),
                  pl.BlockSpec((HALO, d), lambda i: (jnp.minimum(i * per + per, nb - 1), 0)),
                  mod_spec, _resident((d, 3 * d)), _resident((d, d)), _full((d, LANES)),
                  _full((n_taps, 3 * d)), _full((1, LANES)), _full((1, LANES))],
        out_specs=[scan_tile, scan_tile, scan_tile, tile, small],
        out_shape=([jax.ShapeDtypeStruct((geom.b, nt, d), F32)] * 3 + [jax.ShapeDtypeStruct((t, d), F32)]
                   + [jax.ShapeDtypeStruct((geom.b, nt, LANES), F32)]),
        scratch_shapes=[pltpu.VMEM((tm + 2 * HALO, 3 * d), F32)],
        compiler_params=_params(("parallel",)),
        name="gdn_feat",
    )(x, x, x, mods, w_qkv, w_z, w_ab, conv_w, alog, dtb)

    n_steps = nt // CHUNK
    shared = [pl.BlockSpec((geom.b, CHUNK, d), lambda s, zz=zz: (0, _scan_step_chunk(geom, zz, s), 0))
              for zz in range(2)]
    shared_small = [pl.BlockSpec((geom.b, CHUNK, LANES), lambda s, zz=zz: (0, _scan_step_chunk(geom, zz, s), 0))
                    for zz in range(2)]
    of, ob = pl.pallas_call(
        functools.partial(_gdn_scan_body, n_heads=n_heads, batches_per_trip=2),
        grid=(n_steps,),
        in_specs=shared * 3 + shared_small,
        out_specs=shared,
        out_shape=[jax.ShapeDtypeStruct((geom.b, nt, d), F32)] * 2,
        scratch_shapes=[pltpu.VMEM((2, geom.b, n_heads, head, head), F32)],
        compiler_params=_params(("arbitrary",)),
        name="gdn_scan",
    )(q, q, k, k, v, v, gb, gb)

    return pl.pallas_call(
        functools.partial(_gdn_out_body, alpha=alpha, head=head),
        grid=(t // tm,),
        in_specs=[tile, scan_tile, scan_tile, tile, mod_spec, _full((1, head)), _resident((d, d)),
                  _full((1, d)), _full((1, d))],
        out_specs=tile,
        out_shape=jax.ShapeDtypeStruct((t, d), F32),
        compiler_params=_params(("parallel",)),
        name="gdn_out",
    )(x, of, ob, z, mods, norm_w.reshape(1, head), w_o.astype(BF16), ln_g.reshape(1, d), ln_b.reshape(1, d))


def kernel(x, c, ctx, c_ctx, ada_w, ada_b, ln_g, ln_b, rk_mu, rk_w_rkv, rk_w0, rk_w1, rk_w2, rk_a0, rk_a1, rk_a2, rk_g1, rk_g2, rk_k_k, rk_k_a, rk_r_k, rk_lnx_g, rk_lnx_b, rk_w_o, pool_w, pool_scale, gdn_w_in, gdn_conv_w, gdn_a_log, gdn_dt_bias, gdn_norm_w, gdn_w_o, ffn_w1, ffn_w3, ffn_w2, moe_router_w, moe_router_b, moe_w1, moe_w3, moe_w2):
    batch, seq, d = x.shape
    ctx_len = ctx.shape[1]
    depth = ada_w.shape[0]
    geom = _Geom(batch, seq, ctx_len, d)
    alpha = (2 * depth) ** 0.25
    n_mixers = 3

    rows = -(-(batch + 1) // 8) * 8
    cvec = jnp.zeros((rows, d), F32).at[:batch].set(c).at[batch].set(c_ctx)
    table = _ada_table(cvec, ada_w, ada_b)
    mods_all = table[:, :batch + 1].reshape(depth, batch + 1, 6, d)

    xs = jnp.concatenate([x.reshape(batch * seq, d), ctx.reshape(batch * ctx_len, d)], axis=0)
    for i in range(depth):
        mods = mods_all[i]
        kind, j = i % n_mixers, i // n_mixers
        if kind == 0:
            xs = _rwkv_layer(geom, xs, mods, rk_mu[j], rk_w_rkv[j], rk_w0[j], rk_w1[j], rk_w2[j],
                             rk_a0[j], rk_a1[j], rk_a2[j], rk_g1[j], rk_g2[j], rk_k_k[j], rk_k_a[j],
                             rk_r_k[j], rk_lnx_g[j], rk_lnx_b[j], rk_w_o[j], ln_g[i, 0], ln_b[i, 0], alpha,
                             want_ctx=i < depth - 1)
        elif kind == 1:
            xs = _pool_layer(geom, xs, mods, pool_w[j], pool_scale[j], ln_g[i, 0], ln_b[i, 0], alpha)
        else:
            xs = _gdn_layer(geom, xs, mods, gdn_w_in[j], gdn_conv_w[j], gdn_a_log[j], gdn_dt_bias[j],
                            gdn_norm_w[j], gdn_w_o[j], ln_g[i, 0], ln_b[i, 0], alpha)
        e = i // 2
        if i % 2 == 0:
            xs = _ffn_layer(geom, xs, mods, ffn_w1[e], ffn_w3[e], ffn_w2[e], ln_g[i, 1], ln_b[i, 1], alpha)
        else:
            xs = _moe_layer(geom, xs, mods, moe_router_w[e], moe_router_b[e], moe_w1[e], moe_w3[e],
                            moe_w2[e], ln_g[i, 1], ln_b[i, 1], alpha)
    return xs[:batch * seq].reshape(batch, seq, d)
```

```python
import functools
import math

import jax
import jax.numpy as jnp
import numpy as np
from jax import lax
from jax.experimental import pallas as pl
from jax.experimental.pallas import tpu as pltpu

F32 = jnp.float32
BF16 = jnp.bfloat16

GRID_W = 64
CHUNK = 64
LANES = 128
POOL_WINDOWS = (2, 4, 8, 16)
LN_EPS = 1e-5
RK_DECAY_SCALE = math.exp(-0.5)
RK_GN_EPS = 64e-5
GDN_NORM_EPS = 1e-6
GDN_L2_EPS = 1e-6
TOP_K = 2
VMEM_LIMIT = 56 * 1024 * 1024


def _sigmoid(x):
    return 1.0 / (1.0 + jnp.exp(-x))


def _silu(x):
    return x * _sigmoid(x)


def _softplus(x):
    return jnp.maximum(x, 0.0) + jnp.log(1.0 + jnp.exp(-jnp.abs(x)))


def _dot(a, b):
    return jnp.dot(a.astype(BF16), b.astype(BF16), preferred_element_type=F32)


def _dot_nt(a, b):
    return lax.dot_general(a.astype(BF16), b.astype(BF16), (((1,), (1,)), ((), ())),
                           preferred_element_type=F32)


def _dot_tn(a, b):
    return lax.dot_general(a.astype(BF16), b.astype(BF16), (((0,), (0,)), ((), ())),
                           preferred_element_type=F32)


def _split(x):
    hi = x.astype(BF16)
    lo = (x - hi.astype(F32)).astype(BF16)
    return hi, lo


def _dot_lhs2(a, b_exact):
    hi, lo = _split(a)
    return (jnp.dot(hi, b_exact, preferred_element_type=F32)
            + jnp.dot(lo, b_exact, preferred_element_type=F32))


def _dot_rhs2(a_exact, b):
    hi, lo = _split(b)
    return (jnp.dot(a_exact, hi, preferred_element_type=F32)
            + jnp.dot(a_exact, lo, preferred_element_type=F32))


def _dot3(a, b):
    ah, al = _split(a)
    bh, bl = _split(b)
    return (jnp.dot(ah, bh, preferred_element_type=F32)
            + jnp.dot(ah, bl, preferred_element_type=F32)
            + jnp.dot(al, bh, preferred_element_type=F32))


def _segsum(x, e_ref, et_ref, split=True):
    if not split:
        return _dot(_dot(x, e_ref[...]), et_ref[...])
    s = _dot_lhs2(x, e_ref[...])
    return _dot_lhs2(s, et_ref[...])


def _layer_norm(v, g, b):
    mean = jnp.mean(v, axis=-1, keepdims=True)
    c = v - mean
    var = jnp.mean(c * c, axis=-1, keepdims=True)
    return c * lax.rsqrt(var + LN_EPS) * g + b


def _head_indicator(d_model, head):
    n_heads = d_model // head
    e = np.zeros((d_model, LANES), np.float32)
    e[np.arange(d_model), np.arange(d_model) // head] = 1.0
    assert n_heads <= LANES
    return jnp.asarray(e, BF16), jnp.asarray(e.T.copy(), BF16)


def _full(shape):
    n = len(shape)
    return pl.BlockSpec(shape, lambda *_: (0,) * n)


def _resident(shape):
    n = len(shape)
    return pl.BlockSpec(shape, lambda *_: (0,) * n, pipeline_mode=pl.Buffered(1))


def _params(sem):
    return pltpu.CompilerParams(dimension_semantics=sem, vmem_limit_bytes=VMEM_LIMIT)


class _Geom:
    def __init__(self, batch, seq, ctx_len, d_model):
        self.b, self.seq, self.ctx, self.d = batch, seq, ctx_len, d_model
        self.n_lat = batch * seq
        self.t = batch * (seq + ctx_len)

    def mod_row(self, i, tm):
        return jnp.where(i < self.n_lat // tm, i // (self.seq // tm), self.b)


def _ada_body(c_ref, w_ref, b_ref, o_ref):
    o_ref[...] = _dot3(_silu(c_ref[...]), w_ref[...]) + b_ref[...]


def _ada_table(cvec, ada_w, ada_b):
    depth, d, d6 = ada_w.shape
    rows = cvec.shape[0]
    return pl.pallas_call(
        _ada_body,
        grid=(depth, d6 // d),
        in_specs=[_full((rows, d)),
                  pl.BlockSpec((None, d, d), lambda l, j: (l, 0, j)),
                  pl.BlockSpec((None, 1, d), lambda l, j: (l, 0, j))],
        out_specs=pl.BlockSpec((None, rows, d), lambda l, j: (l, 0, j)),
        out_shape=jax.ShapeDtypeStruct((depth, rows, d6), F32),
        compiler_params=_params(("parallel", "parallel")),
        name="ada_table",
    )(cvec, ada_w, ada_b.reshape(depth, 1, d6))


def _ffn_body(x_ref, mod_ref, w13_ref, w2_ref, lng_ref, lnb_ref, o_ref, *, alpha, n_split):
    o_ref[...] = _ffn_apply(x_ref[...], mod_ref, w13_ref, w2_ref, lng_ref, lnb_ref, alpha, n_split)


def _ffn_apply(x, mod_ref, w13_ref, w2_ref, lng_ref, lnb_ref, alpha, n_split):
    sh, sc, gt = mod_ref[3:4, :], mod_ref[4:5, :], mod_ref[5:6, :]
    h = (x * (1.0 + sc) + sh).astype(BF16)
    ff = w2_ref.shape[0] // n_split
    acc = jnp.zeros(x.shape, F32)
    for s in range(n_split):
        ag = jnp.dot(h, w13_ref[:, 2 * s * ff:2 * (s + 1) * ff], preferred_element_type=F32)
        acc = acc + jnp.dot((_silu(ag[:, :ff]) * ag[:, ff:]).astype(BF16), w2_ref[s * ff:(s + 1) * ff, :],
                            preferred_element_type=F32)
    return _layer_norm(alpha * x + gt * acc, lng_ref[...], lnb_ref[...])


FFN_SPLIT = 2


def _ffn_operands(w1, w3, w2, ln_g, ln_b):
    d, ff = w1.shape
    ffs = ff // FFN_SPLIT
    w13 = jnp.concatenate([w[:, s * ffs:(s + 1) * ffs] for s in range(FFN_SPLIT) for w in (w1, w3)],
                          axis=1).astype(BF16)
    arrays = (w13, w2.astype(BF16), ln_g.reshape(1, d), ln_b.reshape(1, d))
    specs = [_resident((d, 2 * ff)), _resident((ff, d)), _full((1, d)), _full((1, d))]
    return arrays, specs


def _ffn_layer(geom, x, mods, w1, w3, w2, ln_g, ln_b, alpha, tm=512):
    t, d = x.shape
    arrays, specs = _ffn_operands(w1, w3, w2, ln_g, ln_b)
    return pl.pallas_call(
        functools.partial(_ffn_body, alpha=alpha, n_split=FFN_SPLIT),
        grid=(t // tm,),
        in_specs=[pl.BlockSpec((tm, d), lambda i: (i, 0)),
                  pl.BlockSpec((None, 6, d), lambda i: (geom.mod_row(i, tm), 0, 0))] + specs,
        out_specs=pl.BlockSpec((tm, d), lambda i: (i, 0)),
        out_shape=jax.ShapeDtypeStruct((t, d), F32),
        compiler_params=_params(("parallel",)),
        name="ffn",
    )(x, mods, *arrays)


MOE_GROUP = 512
MOE_ROWS = 160
MOE_TAIL_ROWS = 64


def _moe_body(x_ref, mod_ref, rw_ref, rb_ref, tri_ref, w13_ref, w2_ref, lng_ref, lnb_ref, o_ref,
              h_scr, gate_scr, sel_scr, rank_scr, selt_scr, rankt_scr, acc_scr, *, alpha, n_experts):
    e = pl.program_id(1)
    tm = x_ref.shape[0]
    n_groups = tm // MOE_GROUP

    @pl.when(e == 0)
    def _():
        x = x_ref[...]
        sh, sc = mod_ref[3:4, :], mod_ref[4:5, :]
        h = x * (1.0 + sc) + sh
        h_scr[...] = h.astype(BF16)
        logits = _dot3(h, rw_ref[...]) + rb_ref[...]
        lane = lax.broadcasted_iota(jnp.int32, logits.shape, 1)
        neg = jnp.float32(-jnp.inf)
        logits = jnp.where(lane < n_experts, logits, neg)
        m1 = jnp.max(logits, axis=-1, keepdims=True)
        i1 = jnp.min(jnp.where(logits == m1, lane, LANES), axis=-1, keepdims=True)
        rest = jnp.where(lane == i1, neg, logits)
        m2 = jnp.max(rest, axis=-1, keepdims=True)
        i2 = jnp.min(jnp.where(rest == m2, lane, LANES), axis=-1, keepdims=True)
        e2 = jnp.exp(m2 - m1)
        p1 = 1.0 / (1.0 + e2)
        p2 = e2 / (1.0 + e2)
        gate_scr[...] = jnp.where(lane == i1, p1, 0.0) + jnp.where(lane == i2, p2, 0.0)
        sel = jnp.where(lane == i1, 1.0, 0.0) + jnp.where(lane == i2, 1.0, 0.0)
        sel_scr[...] = sel
        for s in range(n_groups):
            rows = slice(s * MOE_GROUP, (s + 1) * MOE_GROUP)
            rank = jnp.dot(tri_ref[...], sel[rows].astype(BF16), preferred_element_type=F32)
            rank_scr[rows, :] = rank
            rankt_scr[s] = rank.T
            selt_scr[s] = sel[rows].T
        acc_scr[...] = jnp.zeros(acc_scr.shape, F32)

    lane = lax.broadcasted_iota(jnp.int32, (MOE_GROUP, LANES), 1)
    column = lambda ref, rows: jnp.sum(jnp.where(lane == e, ref[rows, :], 0.0), axis=-1, keepdims=True)
    groups = []
    most = jnp.int32(0)
    for s in range(n_groups):
        rows = slice(s * MOE_GROUP, (s + 1) * MOE_GROUP)
        sel_col = column(sel_scr, rows) > 0.5
        groups.append(dict(rows=rows, gate_col=column(gate_scr, rows), sel_col=sel_col,
                           rank_col=column(rank_scr, rows),
                           sel_row=selt_scr[s, pl.ds(e, 1), :] > 0.5,
                           rank_row=rankt_scr[s, pl.ds(e, 1), :]))
        most = jnp.maximum(most, jnp.sum(jnp.where(sel_col, 1.0, 0.0)).astype(jnp.int32))

    def run_pass(first_slot, n_rows):
        base = first_slot.astype(F32)
        slot_col = lax.broadcasted_iota(jnp.int32, (n_rows, 1), 0).astype(F32) + base
        slot_row = lax.broadcasted_iota(jnp.int32, (1, n_rows), 1).astype(F32) + base
        xg = []
        for gr in groups:
            take = jnp.where((gr["rank_row"] == slot_col) & gr["sel_row"], 1.0, 0.0).astype(BF16)
            xg.append(jnp.dot(take, h_scr[gr["rows"], :], preferred_element_type=F32).astype(BF16))
        xg = jnp.concatenate(xg, axis=0)
        ag = jnp.dot(xg, w13_ref[...], preferred_element_type=F32)
        ffe = w2_ref.shape[0]
        y = jnp.dot((_silu(ag[:, :ffe]) * ag[:, ffe:]).astype(BF16), w2_ref[...], preferred_element_type=F32)
        for k, gr in enumerate(groups):
            put = jnp.where((gr["rank_col"] == slot_row) & gr["sel_col"], 1.0, 0.0).astype(BF16)
            acc_scr[gr["rows"], :] += gr["gate_col"] * _dot(put, y[k * n_rows:(k + 1) * n_rows])

    n_full = most // MOE_ROWS
    left = most - n_full * MOE_ROWS

    def full_pass(j, carry):
        run_pass(j * MOE_ROWS, MOE_ROWS)
        return carry

    lax.fori_loop(0, n_full, full_pass, 0)

    @pl.when(left > MOE_TAIL_ROWS)
    def _():
        run_pass(n_full * MOE_ROWS, MOE_ROWS)

    @pl.when(jnp.logical_and(left > 0, left <= MOE_TAIL_ROWS))
    def _():
        run_pass(n_full * MOE_ROWS, MOE_TAIL_ROWS)

    @pl.when(e == n_experts - 1)
    def _():
        x = x_ref[...]
        gt = mod_ref[5:6, :]
        o_ref[...] = _layer_norm(alpha * x + gt * acc_scr[...], lng_ref[...], lnb_ref[...])


def _moe_layer(geom, x, mods, router_w, router_b, w1, w3, w2, ln_g, ln_b, alpha, tm=1024):
    t, d = x.shape
    n_e, _, ffe = w1.shape
    assert tm % MOE_GROUP == 0 and n_e <= LANES
    rw = jnp.zeros((d, LANES), F32).at[:, :n_e].set(router_w)
    rb = jnp.zeros((1, LANES), F32).at[0, :n_e].set(router_b)
    idx = np.arange(MOE_GROUP)
    tri = jnp.asarray(idx[None, :] < idx[:, None], BF16)
    n_groups = tm // MOE_GROUP
    return pl.pallas_call(
        functools.partial(_moe_body, alpha=alpha, n_experts=n_e),
        grid=(t // tm, n_e),
        in_specs=[pl.BlockSpec((tm, d), lambda i, e: (i, 0)),
                  pl.BlockSpec((None, 6, d), lambda i, e: (geom.mod_row(i, tm), 0, 0)),
                  _full((d, LANES)), _full((1, LANES)), _full((MOE_GROUP, MOE_GROUP)),
                  pl.BlockSpec((None, d, 2 * ffe), lambda i, e: (e, 0, 0)),
                  pl.BlockSpec((None, ffe, d), lambda i, e: (e, 0, 0)),
                  _full((1, d)), _full((1, d))],
        out_specs=pl.BlockSpec((tm, d), lambda i, e: (i, 0)),
        out_shape=jax.ShapeDtypeStruct((t, d), F32),
        scratch_shapes=[pltpu.VMEM((tm, d), BF16), pltpu.VMEM((tm, LANES), F32),
                        pltpu.VMEM((tm, LANES), F32), pltpu.VMEM((tm, LANES), F32),
                        pltpu.VMEM((n_groups, LANES, MOE_GROUP), F32),
                        pltpu.VMEM((n_groups, LANES, MOE_GROUP), F32),
                        pltpu.VMEM((tm, d), F32)],
        compiler_params=_params(("parallel", "arbitrary")),
        name="moe",
    )(x, mods, rw, rb, tri, jnp.concatenate([w1, w3], axis=-1).astype(BF16), w2.astype(BF16),
      ln_g.reshape(1, d), ln_b.reshape(1, d))


def _pool_matrices(tm, seq_len):
    t = np.arange(tm)
    pos = t % seq_len
    mats, inv = [], []
    for win in POOL_WINDOWS:
        lo = np.clip(pos - win // 2, 0, seq_len)
        hi = np.clip(pos + win // 2, 0, seq_len)
        base = t - pos
        j = t[None, :]
        mats.append(((j >= (base + lo)[:, None]) & (j < (base + hi)[:, None])).astype(np.float32))
        inv.append((hi - lo).astype(np.float32))
    return np.stack(mats), np.stack(inv)


def _pool_body(x_ref, mod_ref, pm_ref, cnt_ref, pw_ref, ps_ref, lng_ref, lnb_ref, o_ref, *, alpha):
    x = x_ref[...]
    sh, sc, gt = mod_ref[0:1, :], mod_ref[1:2, :], mod_ref[2:3, :]
    h = x * (1.0 + sc) + sh
    n_g = pm_ref.shape[0]
    gw = x.shape[1] // n_g
    outs = []
    for g in range(n_g):
        hg = h[:, g * gw:(g + 1) * gw]
        total = _dot_rhs2(pm_ref[g], hg)
        pooled = total / cnt_ref[g] - hg
        outs.append(_dot(pooled, pw_ref[g]))
    y = jnp.concatenate(outs, axis=-1) * ps_ref[...]
    o_ref[...] = _layer_norm(alpha * x + gt * y, lng_ref[...], lnb_ref[...])


def _pool_layer(geom, x, mods, pool_w, pool_scale, ln_g, ln_b, alpha, tm=256):
    t, d = x.shape
    n_g = len(POOL_WINDOWS)
    m_lat, c_lat = _pool_matrices(tm, GRID_W)
    m_ctx, c_ctx = _pool_matrices(tm, geom.ctx)
    pm = jnp.asarray(np.stack([m_lat, m_ctx]), BF16)
    cnt = jnp.asarray(np.stack([c_lat, c_ctx])[..., None], F32)
    n_lat_tiles = geom.n_lat // tm
    kind = lambda i: jnp.where(i < n_lat_tiles, 0, 1)
    return pl.pallas_call(
        functools.partial(_pool_body, alpha=alpha),
        grid=(t // tm,),
        in_specs=[pl.BlockSpec((tm, d), lambda i: (i, 0)),
                  pl.BlockSpec((None, 6, d), lambda i: (geom.mod_row(i, tm), 0, 0)),
                  pl.BlockSpec((None, n_g, tm, tm), lambda i: (kind(i), 0, 0, 0)),
                  pl.BlockSpec((None, n_g, tm, 1), lambda i: (kind(i), 0, 0, 0)),
                  _full((n_g, d // n_g, d // n_g)), _full((1, d)), _full((1, d)), _full((1, d))],
        out_specs=pl.BlockSpec((tm, d), lambda i: (i, 0)),
        out_shape=jax.ShapeDtypeStruct((t, d), F32),
        compiler_params=_params(("parallel",)),
        name="pool",
    )(x, mods, pm, cnt, pool_w.astype(BF16), pool_scale.reshape(1, d),
      ln_g.reshape(1, d), ln_b.reshape(1, d))


def _scan_tile_index(geom, i, tm):
    n_lat_tiles = geom.n_lat // tm
    tpb = geom.seq // tm
    is_lat = i < n_lat_tiles
    return jnp.where(is_lat, i // tpb, i - n_lat_tiles), jnp.where(is_lat, geom.ctx // tm + i % tpb, 0)


def _scan_step_chunk(geom, direction, s):
    if direction == 0:
        return s
    nc_ctx = geom.ctx // CHUNK
    nc = (geom.ctx + geom.seq) // CHUNK
    return jnp.where(s < nc_ctx, nc_ctx - 1 - s, nc - 1 + nc_ctx - s)


def _order_masks(direction):
    n = 2 * CHUNK
    ri = lax.broadcasted_iota(jnp.int32, (n, n), 0)
    ci = lax.broadcasted_iota(jnp.int32, (n, n), 1)
    rt, ct = ri & (CHUNK - 1), ci & (CHUNK - 1)
    ahead = rt - ct if direction == 0 else ct - rt
    return ahead > 0, ahead >= 0, ri < CHUNK, ci < CHUNK


INV_BASE = 8


def _inverse_masks(n):
    ri = lax.broadcasted_iota(jnp.int32, (n, n), 0)
    ci = lax.broadcasted_iota(jnp.int32, (n, n), 1)
    eye = (ri == ci).astype(F32)
    same = lambda size: (ri // size) == (ci // size)
    base = same(INV_BASE)
    levels = []
    size = INV_BASE
    while size < CHUNK:
        levels.append(same(2 * size) & ~same(size))
        size *= 2
    return eye, base, levels


def _unit_tri_inverse(mats, masks):
    assert INV_BASE == 8
    eye, base, levels = masks
    n = mats[0].shape[0]
    ps = [jnp.where(base, -a, 0.0) for a in mats]
    p2 = [_dot(p, p) for p in ps]
    p34 = [_dot(q, jnp.concatenate([p, q], axis=1)) for p, q in zip(ps, p2)]
    s3 = [eye + p + q + r[:, :n] for p, q, r in zip(ps, p2, p34)]
    ts = [s + _dot(s, r[:, n:]) for s, r in zip(s3, p34)]
    for off in levels:
        cs = [_dot(jnp.where(off, a, 0.0), t) for a, t in zip(mats, ts)]
        ts = [t - _dot(t, c) for t, c in zip(ts, cs)]
    return ts


def _cum_matrices():
    i = np.arange(CHUNK)
    fwd = (i[None, :] <= i[:, None]).astype(np.float32)
    return jnp.asarray(np.stack([fwd, fwd.T]), BF16)


def _rk_feat_body(x_ref, xp_ref, xn_ref, mod_ref, mu_ref, wrkv_ref, w1_ref, w2_ref, w0_ref,
                  a1_ref, a2_ref, a0_ref, g1_ref, g2_ref, kk_ref, ka_ref, rk_ref, e_ref, et_ref,
                  r_out, kkn_out, v_out, gate_out, bonus_out, lw_out, b_out, kd_out,
                  *, n_lat_tiles, tiles_per_batch, seq):
    i = pl.program_id(0)
    tm, d = x_ref.shape
    q = d // 4
    sh, sc = mod_ref[0:1, :], mod_ref[1:2, :]
    h = x_ref[...] * (1.0 + sc) + sh
    hp = xp_ref[...] * (1.0 + sc) + sh
    hn = xn_ref[...] * (1.0 + sc) + sh
    is_ctx = i >= n_lat_tiles
    t = lax.broadcasted_iota(jnp.int32, (tm, 1), 0)
    col = t & (GRID_W - 1)
    pos = (i % tiles_per_batch) * tm + t
    to_end = (tm - 1) - t

    def prev_tok(z):
        return pltpu.roll(z, 1, 0)

    def next_tok(z):
        return pltpu.roll(z, tm - 1, 0)

    h0, h1, h2, h3 = (h[:, k * q:(k + 1) * q] for k in range(4))
    up = jnp.concatenate([hp[:, 2 * q:3 * q], h2[:tm - GRID_W]], axis=0)
    down = jnp.concatenate([h3[GRID_W:], hn[:, 3 * q:]], axis=0)
    edge0 = jnp.where(is_ctx, t, col)
    edge1 = jnp.where(is_ctx, t, (GRID_W - 1) - col)
    edge2 = jnp.where(is_ctx, to_end, jnp.maximum(pos - (GRID_W - 1), 0))
    edge3 = jnp.where(is_ctx, to_end, jnp.maximum((seq - GRID_W) - pos, 0))
    s0 = jnp.where(edge0 == 0, 0.0, prev_tok(h0))
    s1 = jnp.where(edge1 == 0, 0.0, jnp.where(is_ctx, prev_tok(h1), next_tok(h1)))
    s2 = jnp.where(edge2 == 0, 0.0, jnp.where(is_ctx, next_tok(h2), up))
    s3 = jnp.where(edge3 == 0, 0.0, jnp.where(is_ctx, next_tok(h3), down))
    xx = jnp.concatenate([s0, s1, s2, s3], axis=-1) - h
    xr, xw, xk, xv, xa, xg = (h + xx * mu_ref[m:m + 1, :] for m in range(6))

    r = _dot(xr, wrkv_ref[0])
    k = _dot(xk, wrkv_ref[1])
    v = _dot(xv, wrkv_ref[2])
    dl = jnp.tanh(_dot(xw, w1_ref[...]))
    al = _dot(xa, a1_ref[...])
    kkr = k * kk_ref[...]
    ss = _segsum(kkr * kkr, e_ref, et_ref, split=False)
    kkn = kkr / jnp.maximum(jnp.sqrt(ss), 1e-12)
    kd_sum = jnp.zeros_like(k)
    for z in range(2):
        lw = -RK_DECAY_SCALE * _sigmoid(w0_ref[z:z + 1, :] + _dot(dl, w2_ref[z]))
        a = _sigmoid(a0_ref[z:z + 1, :] + _dot(al, a2_ref[z]))
        kd = k * (1.0 + (a - 1.0) * ka_ref[...])
        lw_out[z] = lw
        b_out[z] = kkn * a
        kd_out[z] = kd
        kd_sum = kd_sum + kd
    coef = _segsum(r * kd_sum * rk_ref[...], e_ref, et_ref)
    r_out[...] = r
    kkn_out[...] = kkn
    v_out[...] = v
    bonus_out[...] = coef * v
    gate_out[...] = _dot(_sigmoid(_dot(xg, g1_ref[...])), g2_ref[...])


def _rk_scan_chunks(streams, cum_ref, s_scr):
    c = CHUNK
    zero = jnp.float32(0.0)
    inv_masks = _inverse_masks(2 * c)
    lane = lax.broadcasted_iota(jnp.int32, (1, LANES), 1)
    m0, m1 = lane < c, lane >= c

    def pick(m_first, m_second, first, second):
        return jnp.where(m_first, first, jnp.where(m_second, second, zero))

    chains = []
    for direction, b, (r_ref, kk_ref, v_ref, lw_ref, b_ref, kd_ref, y_ref) in streams:
        lw = lw_ref[b]
        cs = _dot_rhs2(cum_ref[direction], lw)
        cs_last = cs[c - 1:c, :] if direction == 0 else cs[0:1, :]
        eg, en, el = jnp.exp(cs), jnp.exp(-cs), jnp.exp(cs_last - cs)
        rt = r_ref[b] * eg
        kkt = kk_ref[b] * jnp.exp(cs - lw)
        bb, kd, v = b_ref[b], kd_ref[b], v_ref[b]
        bt, kt, bh, kh = bb * en, kd * en, bb * el, kd * el
        gl = jnp.exp(cs_last)
        strict, incl, top, left = _order_masks(direction)
        tl, br = top & left, (~top) & (~left)
        tr, bl = top & (~left), (~top) & left
        masks = dict(bk=(tl & strict, br & strict), kk=(tr & strict, bl & strict),
                     rb=(bl & incl, tr & incl), rk=(br & incl, tl & incl), diag=top == left)
        for p in range(r_ref.shape[-1] // LANES):
            s = slice(p * LANES, (p + 1) * LANES)
            chains.append(dict(rt=rt[:, s], kkt=kkt[:, s], bt=bt[:, s], kt=kt[:, s], bh=bh[:, s],
                               kh=kh[:, s], v=v[:, s], gl=gl[:, s], m=masks, y_ref=y_ref,
                               where=(direction, b, p), lanes=s))

    s_old = [s_scr[ch["where"]] for ch in chains]
    wide = [_dot_nt(jnp.concatenate([ch["kkt"], ch["rt"]], 0),
                    jnp.concatenate([jnp.where(m0, jnp.concatenate([ch["bt"], ch["kt"]], 0), zero),
                                     jnp.where(m1, jnp.concatenate([ch["kt"], ch["bt"]], 0), zero), st], 0))
            for ch, st in zip(chains, s_old)]
    n2 = 2 * c
    r0 = [w[:, :n2] for w in wide]
    r1 = [jnp.concatenate([w[c:, n2:2 * n2], w[:c, n2:2 * n2]], 0) for w in wide]
    proj = [w[:, 2 * n2:] for w in wide]
    a_bk = [pick(*ch["m"]["bk"], x0, x1) for ch, x0, x1 in zip(chains, r0, r1)]
    a_kk = [pick(*ch["m"]["kk"], x0, x1) for ch, x0, x1 in zip(chains, r0, r1)]
    a_rbk = [jnp.concatenate([pick(*ch["m"]["rb"], x0, x1), pick(*ch["m"]["rk"], x0, x1)], 1)
             for ch, x0, x1 in zip(chains, r0, r1)]
    v_sw = [jnp.concatenate([jnp.where(m1, ch["v"], zero), jnp.where(m0, ch["v"], zero)], 0) for ch in chains]
    akkv = [_dot(a, x) for a, x in zip(a_kk, v_sw)]
    t_inv = _unit_tri_inverse(a_bk, inv_masks)
    ks_sm = [jnp.concatenate([jnp.where(m0, pj[:c], zero), jnp.where(m1, pj[:c], zero)], 0) for pj in proj]
    u_sm = [-_dot(t, k_ + a) for t, k_, a in zip(t_inv, ks_sm, akkv)]
    y_sm = [_dot(a, jnp.concatenate([u, x], 0)) for a, u, x in zip(a_rbk, u_sm, v_sw)]
    upd = [_dot_tn(jnp.concatenate([u[:c] + u[c:], ch["v"]], 0), jnp.concatenate([ch["bh"], ch["kh"]], 0))
           for ch, u in zip(chains, u_sm)]
    for i, ch in enumerate(chains):
        _, b, _ = ch["where"]
        ch["y_ref"][b, :, ch["lanes"]] = proj[i][c:] + y_sm[i][:c] + y_sm[i][c:]
        s_scr[ch["where"]] = s_old[i] * ch["gl"] + jnp.where(ch["m"]["diag"], upd[i], zero)


def _rk_scan_body(rf, rb, kkf, kkb, vf, vb, lwf, lwb, bf, bb, kdf, kdb, cum_ref, yf, yb, s_scr,
                  *, batches_per_trip):
    @pl.when(pl.program_id(0) == 0)
    def _():
        s_scr[...] = jnp.zeros(s_scr.shape, F32)

    fwd = (rf, kkf, vf, lwf, bf, kdf, yf)
    bwd = (rb, kkb, vb, lwb, bb, kdb, yb)

    def trip(i, carry):
        streams = []
        for k in range(batches_per_trip):
            b = i * batches_per_trip + k
            streams += [(0, b, fwd), (1, b, bwd)]
        _rk_scan_chunks(streams, cum_ref, s_scr)
        return carry

    lax.fori_loop(0, rf.shape[0] // batches_per_trip, trip, 0)


def _finish_sublayer(x1, mod_ref, rest, alpha):
    *ffn_refs, o_ref = rest
    o_ref[...] = _ffn_apply(x1, mod_ref, *ffn_refs, alpha, FFN_SPLIT) if ffn_refs else x1


def _rk_out_body(x_ref, yf_ref, yb_ref, bonus_ref, gate_ref, mod_ref, lxg_ref, lxb_ref, wo_ref, e_ref,
                 et_ref, lng_ref, lnb_ref, *rest, alpha, head):
    x = x_ref[...]
    y = yf_ref[...] + yb_ref[...]
    mean = _segsum(y, e_ref, et_ref) * (1.0 / head)
    cen = y - mean
    var = _segsum(cen * cen, e_ref, et_ref, split=False) * (1.0 / head)
    yn = cen * lax.rsqrt(var + RK_GN_EPS) * lxg_ref[...] + lxb_ref[...]
    out = _dot((yn + bonus_ref[...]) * gate_ref[...], wo_ref[...])
    gt = mod_ref[2:3, :]
    _finish_sublayer(_layer_norm(alpha * x + gt * out, lng_ref[...], lnb_ref[...]), mod_ref, rest, alpha)


def _rwkv_layer(geom, x, mods, mu, w_rkv, w0, w1, w2, a0, a1, a2, g1, g2, k_k, k_a, r_k,
                lnx_g, lnx_b, w_o, ln_g, ln_b, alpha, want_ctx=True, ffn=((), []), tm=256):
    t, d = x.shape
    ffn_arrays, ffn_specs = ffn
    t_out = t if want_ctx else geom.n_lat
    head = r_k.shape[-1]
    assert 2 * head == LANES and geom.ctx == tm and geom.seq % tm == 0
    lora_d, lora_a, lora_g = w1.shape[-1], a1.shape[-1], g1.shape[-1]
    e, et = _head_indicator(d, head)

    def cat_dirs(w):
        return jnp.concatenate([w[0], w[1]], axis=-1).astype(BF16)

    def pad_dirs(w):
        z = jnp.zeros_like(w[0])
        return jnp.stack([jnp.concatenate([w[0], z], 0), jnp.concatenate([z, w[1]], 0)]).astype(BF16)

    nb = t // GRID_W
    nt = geom.ctx + geom.seq
    row = lambda i: (i, 0)
    tile = pl.BlockSpec((tm, d), row)
    scan_tile = pl.BlockSpec((None, tm, d), lambda i: (*_scan_tile_index(geom, i, tm), 0))
    scan_tile2 = pl.BlockSpec((2, None, tm, d), lambda i: (0, *_scan_tile_index(geom, i, tm), 0))
    mod_spec = pl.BlockSpec((None, 6, d), lambda i: (geom.mod_row(i, tm), 0, 0))
    per = tm // GRID_W
    feat = pl.pallas_call(
        functools.partial(_rk_feat_body, n_lat_tiles=geom.n_lat // tm,
                          tiles_per_batch=geom.seq // tm, seq=geom.seq),
        grid=(t // tm,),
        in_specs=[tile,
                  pl.BlockSpec((GRID_W, d), lambda i: (jnp.maximum(i * per - 1, 0), 0)),
                  pl.BlockSpec((GRID_W, d), lambda i: (jnp.minimum(i * per + per, nb - 1), 0)),
                  mod_spec, _full((6, d)), _resident((3, d, d)),
                  _full((d, 2 * lora_d)), _full((2, 2 * lora_d, d)), _full((2, d)),
                  _full((d, 2 * lora_a)), _full((2, 2 * lora_a, d)), _full((2, d)),
                  _full((d, lora_g)), _full((lora_g, d)),
                  _full((1, d)), _full((1, d)), _full((1, d)), _full((d, LANES)), _full((LANES, d))],
        out_specs=[scan_tile, scan_tile, scan_tile, tile, tile, scan_tile2, scan_tile2, scan_tile2],
        out_shape=([jax.ShapeDtypeStruct((geom.b, nt, d), F32)] * 3 + [jax.ShapeDtypeStruct((t, d), F32)] * 2
                   + [jax.ShapeDtypeStruct((2, geom.b, nt, d), F32)] * 3),
        compiler_params=_params(("parallel",)),
        name="rwkv_feat",
    )
    r, kkn, v, gate, bonus, lw, bb, kd = feat(
        x, x, x, mods, mu, w_rkv.astype(BF16), cat_dirs(w1), pad_dirs(w2), w0,
        cat_dirs(a1), pad_dirs(a2), a0, g1.astype(BF16), g2.astype(BF16),
        k_k.reshape(1, d), k_a.reshape(1, d), r_k.reshape(1, d), e, et)

    n_steps = nt // CHUNK
    shared = [pl.BlockSpec((geom.b, CHUNK, d), lambda s, z=z: (0, _scan_step_chunk(geom, z, s), 0))
              for z in range(2)]
    per_dir = [pl.BlockSpec((None, geom.b, CHUNK, d), lambda s, z=z: (z, 0, _scan_step_chunk(geom, z, s), 0))
               for z in range(2)]
    yf, yb = pl.pallas_call(
        functools.partial(_rk_scan_body, batches_per_trip=1),
        grid=(n_steps,),
        in_specs=shared * 3 + per_dir * 3 + [_full((2, CHUNK, CHUNK))],
        out_specs=shared,
        out_shape=[jax.ShapeDtypeStruct((geom.b, nt, d), F32)] * 2,
        scratch_shapes=[pltpu.VMEM((2, geom.b, d // LANES, LANES, LANES), F32)],
        compiler_params=_params(("arbitrary",)),
        name="rwkv_scan",
    )(r, r, kkn, kkn, v, v, lw, lw, bb, bb, kd, kd, _cum_matrices())

    return pl.pallas_call(
        functools.partial(_rk_out_body, alpha=alpha, head=head),
        grid=(t_out // tm,),
        in_specs=[tile, scan_tile, scan_tile, tile, tile, mod_spec, _full((1, d)), _full((1, d)),
                  _resident((d, d)), _full((d, LANES)), _full((LANES, d)), _full((1, d)), _full((1, d))]
                 + ffn_specs,
        out_specs=tile,
        out_shape=jax.ShapeDtypeStruct((t_out, d), F32),
        compiler_params=_params(("parallel",)),
        name="rwkv_out",
    )(x, yf, yb, bonus, gate, mods, lnx_g.reshape(1, d), lnx_b.reshape(1, d), w_o.astype(BF16), e, et,
      ln_g.reshape(1, d), ln_b.reshape(1, d), *ffn_arrays)


HALO = 8


def _gdn_feat_body(x_ref, xp_ref, xn_ref, mod_ref, wq_ref, wz_ref, wab_ref, cw_ref, alog_ref, dtb_ref,
                   q_out, k_out, v_out, z_out, gb_out, proj_scr,
                   *, n_lat_tiles, tiles_per_batch, head, n_taps):
    i = pl.program_id(0)
    tm, d = x_ref.shape
    sh, sc = mod_ref[0:1, :], mod_ref[1:2, :]
    h = x_ref[...] * (1.0 + sc) + sh
    is_ctx = i >= n_lat_tiles
    j = i % tiles_per_batch
    has_prev = jnp.logical_and(jnp.logical_not(is_ctx), j > 0)
    has_next = jnp.logical_and(jnp.logical_not(is_ctx), j < tiles_per_batch - 1)
    hp = jnp.where(has_prev, xp_ref[...] * (1.0 + sc) + sh, 0.0)
    hn = jnp.where(has_next, xn_ref[...] * (1.0 + sc) + sh, 0.0)
    hb = h.astype(BF16)
    proj_scr[...] = _dot(jnp.concatenate([hp, h, hn], axis=0), wq_ref[...])
    left = n_taps // 2
    conv = jnp.zeros((tm, 3 * d), F32)
    for tap in range(n_taps):
        conv = conv + proj_scr[pl.ds(HALO + tap - left, tm), :] * cw_ref[tap:tap + 1, :]
    qkv = _silu(conv)
    n_heads = d // head
    for hd in range(n_heads):
        qs = qkv[:, hd * head:(hd + 1) * head]
        ks = qkv[:, d + hd * head:d + (hd + 1) * head]
        q_out[:, hd * head:(hd + 1) * head] = (
            qs * lax.rsqrt(jnp.sum(qs * qs, axis=-1, keepdims=True) + GDN_L2_EPS) * head ** -0.5)
        k_out[:, hd * head:(hd + 1) * head] = (
            ks * lax.rsqrt(jnp.sum(ks * ks, axis=-1, keepdims=True) + GDN_L2_EPS))
    v_out[...] = qkv[:, 2 * d:]
    z_out[...] = jnp.dot(hb, wz_ref[...], preferred_element_type=F32)
    ab = _dot3(h, wab_ref[...])
    lane = lax.broadcasted_iota(jnp.int32, ab.shape, 1)
    g = -jnp.exp(alog_ref[...]) * _softplus(ab + dtb_ref[...])
    gb_out[...] = jnp.where(lane < 2 * n_heads, g, _sigmoid(ab))


def _gdn_scan_chunks(streams, s_scr, n_heads):
    c = CHUNK
    zero = jnp.float32(0.0)
    inv_masks = _inverse_masks(2 * c)
    row = lax.broadcasted_iota(jnp.int32, (2 * c, 1), 0)
    top_rows = row < c

    chains = []
    for direction, b, (q_ref, k_ref, v_ref, gb_ref, y_ref) in streams:
        head = q_ref.shape[-1] // n_heads
        strict, incl, top, left = _order_masks(direction)
        same = top == left
        masks = dict(strict=same & strict, incl=same & incl,
                     cum=jnp.where(same & incl, 1.0, 0.0).astype(BF16),
                     nxt=jnp.where(same & strict, 1.0, 0.0))
        first, last = (0, c - 1) if direction == 0 else (c - 1, 0)
        gb = gb_ref[b]
        col = lambda idx: gb[:, idx:idx + 1]
        for p in range(n_heads // 2):
            h0, h1 = 2 * p, 2 * p + 1
            stack = lambda ref: jnp.concatenate([ref[b, :, h0 * head:(h0 + 1) * head],
                                                 ref[b, :, h1 * head:(h1 + 1) * head]], 0)
            chains.append(dict(
                g=jnp.concatenate([col(direction * n_heads + h0), col(direction * n_heads + h1)], 0),
                beta=jnp.concatenate([col((2 + direction) * n_heads + h0),
                                      col((2 + direction) * n_heads + h1)], 0),
                q=stack(q_ref), k=stack(k_ref), v=stack(v_ref),
                m=masks, first=first, last=last, y_ref=y_ref, b=b, head=head,
                where=((direction, b, h0), (direction, b, h1)),
                lanes=(slice(h0 * head, (h0 + 1) * head), slice(h1 * head, (h1 + 1) * head))))

    dlog = [_dot_rhs2(ch["m"]["cum"], ch["g"] * ch["m"]["nxt"]) for ch in chains]
    cs, cs_l0, cs_l1 = [], [], []
    for ch, dl in zip(chains, dlog):
        f, l, g = ch["first"], ch["last"], ch["g"]
        x = jnp.where(top_rows, dl[:, f:f + 1] + g[f:f + 1], dl[:, c + f:c + f + 1] + g[c + f:c + f + 1])
        cs.append(x)
        cs_l0.append(x[l:l + 1])
        cs_l1.append(x[c + l:c + l + 1])
    cs_end = [jnp.where(top_rows, x0, x1) for x0, x1 in zip(cs_l0, cs_l1)]
    decay = [jnp.exp(jnp.where(ch["m"]["incl"], x, -jnp.inf)) for ch, x in zip(chains, dlog)]
    kb = [ch["k"] * ch["beta"] for ch in chains]
    vb = [ch["v"] * ch["beta"] for ch in chains]
    kkt = [_dot_nt(x, ch["k"]) for x, ch in zip(kb, chains)]
    qkt = [_dot_nt(ch["q"], ch["k"]) for ch in chains]
    lower = [jnp.where(ch["m"]["strict"], x * dc, zero) for ch, x, dc in zip(chains, kkt, decay)]
    a_qk = [jnp.where(ch["m"]["incl"], x * dc, zero) for ch, x, dc in zip(chains, qkt, decay)]
    t_inv = _unit_tri_inverse(lower, inv_masks)
    eg = [jnp.exp(x) for x in cs]
    uw = [_dot(t, jnp.concatenate([x, y * e], axis=1)) for t, x, y, e in zip(t_inv, vb, kb, eg)]
    s_old = [(s_scr[ch["where"][0]], s_scr[ch["where"][1]]) for ch in chains]
    ws = [jnp.concatenate([_dot(x[:c, ch["head"]:], s0), _dot(x[c:, ch["head"]:], s1)], 0)
          for ch, x, (s0, s1) in zip(chains, uw, s_old)]
    qs = [jnp.concatenate([_dot((ch["q"] * e)[:c], s0), _dot((ch["q"] * e)[c:], s1)], 0)
          for ch, e, (s0, s1) in zip(chains, eg, s_old)]
    v_new = [x[:, :ch["head"]] - y for ch, x, y in zip(chains, uw, ws)]
    o = [x + _dot(a, y) for x, a, y in zip(qs, a_qk, v_new)]
    k_dec = [ch["k"] * jnp.exp(ce - y) for ch, ce, y in zip(chains, cs_end, cs)]
    for i, ch in enumerate(chains):
        ch["y_ref"][ch["b"], :, ch["lanes"][0]] = o[i][:c]
        ch["y_ref"][ch["b"], :, ch["lanes"][1]] = o[i][c:]
        s_scr[ch["where"][0]] = s_old[i][0] * jnp.exp(cs_l0[i]) + _dot_tn(k_dec[i][:c], v_new[i][:c])
        s_scr[ch["where"][1]] = s_old[i][1] * jnp.exp(cs_l1[i]) + _dot_tn(k_dec[i][c:], v_new[i][c:])


def _gdn_scan_body(qf, qb, kf, kb, vf, vb, gf, gb, yf, yb, s_scr, *, n_heads, batches_per_trip):
    @pl.when(pl.program_id(0) == 0)
    def _():
        s_scr[...] = jnp.zeros(s_scr.shape, F32)

    fwd = (qf, kf, vf, gf, yf)
    bwd = (qb, kb, vb, gb, yb)

    def trip(i, carry):
        streams = []
        for j in range(batches_per_trip):
            b = i * batches_per_trip + j
            streams += [(0, b, fwd), (1, b, bwd)]
        _gdn_scan_chunks(streams, s_scr, n_heads)
        return carry

    lax.fori_loop(0, qf.shape[0] // batches_per_trip, trip, 0)


def _gdn_out_body(x_ref, of_ref, ob_ref, z_ref, mod_ref, nw_ref, wo_ref, lng_ref, lnb_ref, *rest,
                  alpha, head):
    x = x_ref[...]
    o = of_ref[...] + ob_ref[...]
    z = z_ref[...]
    d = x.shape[1]
    parts = []
    for hd in range(d // head):
        oh = o[:, hd * head:(hd + 1) * head]
        on = oh * lax.rsqrt(jnp.mean(oh * oh, axis=-1, keepdims=True) + GDN_NORM_EPS) * nw_ref[...]
        parts.append(on * _silu(z[:, hd * head:(hd + 1) * head]))
    y = _dot(jnp.concatenate(parts, axis=-1), wo_ref[...])
    gt = mod_ref[2:3, :]
    _finish_sublayer(_layer_norm(alpha * x + gt * y, lng_ref[...], lnb_ref[...]), mod_ref, rest, alpha)


def _gdn_layer(geom, x, mods, w_in, conv_w, a_log, dt_bias, norm_w, w_o, ln_g, ln_b, alpha,
               ffn=((), []), tm=256):
    t, d = x.shape
    ffn_arrays, ffn_specs = ffn
    n_heads = a_log.shape[-1]
    head = d // n_heads
    n_taps = conv_w.shape[0]
    assert head == LANES and geom.ctx == tm and geom.seq % tm == 0 and n_heads % 2 == 0
    assert 4 * n_heads <= LANES
    w_qkv = w_in[:, :3 * d].astype(BF16)
    w_z = w_in[:, 3 * d:4 * d].astype(BF16)
    w_ab = jnp.zeros((d, LANES), F32).at[:, :4 * n_heads].set(w_in[:, 4 * d:])
    alog = jnp.zeros((1, LANES), F32).at[0, :2 * n_heads].set(a_log.reshape(-1))
    dtb = jnp.zeros((1, LANES), F32).at[0, :2 * n_heads].set(dt_bias.reshape(-1))

    nb = t // HALO
    nt = geom.ctx + geom.seq
    per = tm // HALO
    tile = pl.BlockSpec((tm, d), lambda i: (i, 0))
    scan_tile = pl.BlockSpec((None, tm, d), lambda i: (*_scan_tile_index(geom, i, tm), 0))
    small = pl.BlockSpec((None, tm, LANES), lambda i: (*_scan_tile_index(geom, i, tm), 0))
    mod_spec = pl.BlockSpec((None, 6, d), lambda i: (geom.mod_row(i, tm), 0, 0))
    q, k, v, z, gb = pl.pallas_call(
        functools.partial(_gdn_feat_body, n_lat_tiles=geom.n_lat // tm,
                          tiles_per_batch=geom.seq // tm, head=head, n_taps=n_taps),
        grid=(t // tm,),
        in_specs=[tile,
                  pl.BlockSpec((HALO, d), lambda i: (jnp.maximum(i * per - 1, 0), 0)),
                  pl.BlockSpec((HALO, d), lambda i: (jnp.minimum(i * per + per, nb - 1), 0)),
                  mod_spec, _resident((d, 3 * d)), _resident((d, d)), _full((d, LANES)),
                  _full((n_taps, 3 * d)), _full((1, LANES)), _full((1, LANES))],
        out_specs=[scan_tile, scan_tile, scan_tile, tile, small],
        out_shape=([jax.ShapeDtypeStruct((geom.b, nt, d), F32)] * 3 + [jax.ShapeDtypeStruct((t, d), F32)]
                   + [jax.ShapeDtypeStruct((geom.b, nt, LANES), F32)]),
        scratch_shapes=[pltpu.VMEM((tm + 2 * HALO, 3 * d), F32)],
        compiler_params=_params(("parallel",)),
        name="gdn_feat",
    )(x, x, x, mods, w_qkv, w_z, w_ab, conv_w, alog, dtb)

    n_steps = nt // CHUNK
    shared = [pl.BlockSpec((geom.b, CHUNK, d), lambda s, zz=zz: (0, _scan_step_chunk(geom, zz, s), 0))
              for zz in range(2)]
    shared_small = [pl.BlockSpec((geom.b, CHUNK, LANES), lambda s, zz=zz: (0, _scan_step_chunk(geom, zz, s), 0))
                    for zz in range(2)]
    of, ob = pl.pallas_call(
        functools.partial(_gdn_scan_body, n_heads=n_heads, batches_per_trip=2),
        grid=(n_steps,),
        in_specs=shared * 3 + shared_small,
        out_specs=shared,
        out_shape=[jax.ShapeDtypeStruct((geom.b, nt, d), F32)] * 2,
        scratch_shapes=[pltpu.VMEM((2, geom.b, n_heads, head, head), F32)],
        compiler_params=_params(("arbitrary",)),
        name="gdn_scan",
    )(q, q, k, k, v, v, gb, gb)

    return pl.pallas_call(
        functools.partial(_gdn_out_body, alpha=alpha, head=head),
        grid=(t // tm,),
        in_specs=[tile, scan_tile, scan_tile, tile, mod_spec, _full((1, head)), _resident((d, d)),
                  _full((1, d)), _full((1, d))] + ffn_specs,
        out_specs=tile,
        out_shape=jax.ShapeDtypeStruct((t, d), F32),
        compiler_params=_params(("parallel",)),
        name="gdn_out",
    )(x, of, ob, z, mods, norm_w.reshape(1, head), w_o.astype(BF16), ln_g.reshape(1, d), ln_b.reshape(1, d),
      *ffn_arrays)


def kernel(x, c, ctx, c_ctx, ada_w, ada_b, ln_g, ln_b, rk_mu, rk_w_rkv, rk_w0, rk_w1, rk_w2, rk_a0, rk_a1, rk_a2, rk_g1, rk_g2, rk_k_k, rk_k_a, rk_r_k, rk_lnx_g, rk_lnx_b, rk_w_o, pool_w, pool_scale, gdn_w_in, gdn_conv_w, gdn_a_log, gdn_dt_bias, gdn_norm_w, gdn_w_o, ffn_w1, ffn_w3, ffn_w2, moe_router_w, moe_router_b, moe_w1, moe_w3, moe_w2):
    batch, seq, d = x.shape
    ctx_len = ctx.shape[1]
    depth = ada_w.shape[0]
    geom = _Geom(batch, seq, ctx_len, d)
    alpha = (2 * depth) ** 0.25
    n_mixers = 3

    rows = -(-(batch + 1) // 8) * 8
    cvec = jnp.zeros((rows, d), F32).at[:batch].set(c).at[batch].set(c_ctx)
    table = _ada_table(cvec, ada_w, ada_b)
    mods_all = table[:, :batch + 1].reshape(depth, batch + 1, 6, d)

    xs = jnp.concatenate([x.reshape(batch * seq, d), ctx.reshape(batch * ctx_len, d)], axis=0)
    for i in range(depth):
        mods = mods_all[i]
        kind, j = i % n_mixers, i // n_mixers
        e = i // 2
        dense = i % 2 == 0
        fused = dense and kind != 1
        ffn = _ffn_operands(ffn_w1[e], ffn_w3[e], ffn_w2[e], ln_g[i, 1], ln_b[i, 1]) if fused else ((), [])
        if kind == 0:
            xs = _rwkv_layer(geom, xs, mods, rk_mu[j], rk_w_rkv[j], rk_w0[j], rk_w1[j], rk_w2[j],
                             rk_a0[j], rk_a1[j], rk_a2[j], rk_g1[j], rk_g2[j], rk_k_k[j], rk_k_a[j],
                             rk_r_k[j], rk_lnx_g[j], rk_lnx_b[j], rk_w_o[j], ln_g[i, 0], ln_b[i, 0], alpha,
                             want_ctx=i < depth - 1, ffn=ffn)
        elif kind == 1:
            xs = _pool_layer(geom, xs, mods, pool_w[j], pool_scale[j], ln_g[i, 0], ln_b[i, 0], alpha)
        else:
            xs = _gdn_layer(geom, xs, mods, gdn_w_in[j], gdn_conv_w[j], gdn_a_log[j], gdn_dt_bias[j],
                            gdn_norm_w[j], gdn_w_o[j], ln_g[i, 0], ln_b[i, 0], alpha, ffn=ffn)
        if fused:
            pass
        elif dense:
            xs = _ffn_layer(geom, xs, mods, ffn_w1[e], ffn_w3[e], ffn_w2[e], ln_g[i, 1], ln_b[i, 1], alpha)
        else:
            xs = _moe_layer(geom, xs, mods, moe_router_w[e], moe_router_b[e], moe_w1[e], moe_w3[e],
                            moe_w2[e], ln_g[i, 1], ln_b[i, 1], alpha)
    return xs[:batch * seq].reshape(batch, seq, d)
```
